```python
import math
import jax
import jax.numpy as jnp
from jax import lax
import numpy as np


D_MODEL = 1024
BATCH = 4
SEQ = 8192
DEPTH = 2

GRID_W = 64
CTX_LEN = 256
N_MIXERS = 2

DA_HEADS = 8
DA_HEAD_DIM = 64
DA_V_DIM = 2 * DA_HEAD_DIM
ROPE_THETA = 10000.0
Q_BLOCK = 128

SSM_D_INNER = 2 * D_MODEL
SSM_HEAD_DIM = 64
SSM_HEADS = SSM_D_INNER // SSM_HEAD_DIM
SSM_GROUPS = 4
HEADS_PER_GROUP = SSM_HEADS // SSM_GROUPS
SSM_STATE = 128
SSM_CONV = 5
SSM_CHUNK = 128
SSM_CONV_DIM = SSM_D_INNER + 2 * SSM_GROUPS * SSM_STATE
SSM_IN_DIM = SSM_D_INNER + SSM_CONV_DIM + 2 * SSM_HEADS

MOE_GROUPS = 4
MOE_PER_GROUP = 8
MOE_EXPERTS = MOE_GROUPS * MOE_PER_GROUP
MOE_TOP_K = 2
MOE_D_FF = 512
MOE_BLOCK = 128

LN_EPS = 1e-5
RMS_EPS = 1e-5
DEEPNORM_ALPHA = (2 * DEPTH) ** 0.25
DEEPNORM_BETA = (8 * DEPTH) ** -0.25

N_ATTN_LAYERS = (DEPTH + N_MIXERS - 1) // N_MIXERS
N_SSM_LAYERS = DEPTH // N_MIXERS

kernel_name = 'hybrid_diffattn_mamba2_hmoe_prefix_dit'


def layer_norm(x, g, b):
    xf = x.astype(jnp.float32)
    mu = jnp.mean(xf, axis=-1, keepdims=True)
    var = jnp.mean(jnp.square(xf - mu), axis=-1, keepdims=True)
    return ((xf - mu) * lax.rsqrt(var + LN_EPS) * g + b).astype(x.dtype)


def rms_norm(x, g):
    xf = x.astype(jnp.float32)
    ms = jnp.mean(jnp.square(xf), axis=-1, keepdims=True)
    return (xf * lax.rsqrt(ms + RMS_EPS) * g).astype(x.dtype)


def modulation(cond, w_mod, b_mod):
    m = jax.nn.silu(cond) @ w_mod + b_mod
    return jnp.split(m[..., None, :], 6, axis=-1)


def axial_rope_tables(rows):
    n = rows * GRID_W
    row = jnp.broadcast_to(jnp.arange(rows)[:, None], (rows, GRID_W)).reshape(n)
    col = jnp.broadcast_to(jnp.arange(GRID_W)[None, :], (rows, GRID_W)).reshape(n)
    axis_dims = DA_HEAD_DIM // 2
    inv = ROPE_THETA ** (-jnp.arange(0, axis_dims, 2, dtype=jnp.float32) / axis_dims)
    ang = jnp.stack([row[:, None] * inv, col[:, None] * inv], axis=1)
    return jnp.cos(ang), jnp.sin(ang)


def apply_rope(x, cos, sin):
    xs = x.astype(jnp.float32).reshape(x.shape[:-1] + (2, 2, DA_HEAD_DIM // 4))
    x1, x2 = xs[..., 0, :], xs[..., 1, :]
    c = cos[:, None, None]
    s = sin[:, None, None]
    out = jnp.stack([x1 * c - x2 * s, x2 * c + x1 * s], axis=-2)
    return out.reshape(x.shape).astype(x.dtype)


def diff_attention(h_ctx, h_lat, cos, sin, w_qkv, w_o, lq1, lk1, lq2, lk2, subln_g, lambda_init, need_ctx):
    B, S, _ = h_lat.shape

    def project(h):
        n = h.shape[1]
        q, k, v = jnp.split(h @ w_qkv, 3, axis=-1)
        return (q.reshape(B, n, DA_HEADS, 2, DA_HEAD_DIM),
                k.reshape(B, n, DA_HEADS, 2, DA_HEAD_DIM),
                v.reshape(B, n, DA_HEADS, DA_V_DIM))

    qc, kc, vc = project(h_ctx)
    ql, kl, vl = project(h_lat)
    ql = apply_rope(ql, cos, sin)
    kl = apply_rope(kl, cos, sin)
    lam = (jnp.exp(jnp.sum((lq1 * lk1).astype(jnp.float32)))
           - jnp.exp(jnp.sum((lq2 * lk2).astype(jnp.float32))) + lambda_init)
    scale = DA_HEAD_DIM ** -0.5

    def attend(q, k, v):
        s = jnp.einsum('bqhjd,bkhjd->bhjqk', q, k, preferred_element_type=jnp.float32) * scale
        p = jax.nn.softmax(s, axis=-1)
        a = p[:, :, 0] - lam * p[:, :, 1]
        o = jnp.einsum('bhqk,bkhe->bqhe', a.astype(v.dtype), v)
        o = rms_norm(o, subln_g) * (1.0 - lambda_init)
        return o.reshape(B, q.shape[1], DA_HEADS * DA_V_DIM) @ w_o

    k_all = jnp.concatenate([kc, kl], axis=1)
    v_all = jnp.concatenate([vc, vl], axis=1)
    qb = ql.reshape(B, S // Q_BLOCK, Q_BLOCK, DA_HEADS, 2, DA_HEAD_DIM).swapaxes(0, 1)
    ob = lax.map(lambda q: attend(q, k_all, v_all), qb)
    o_lat = ob.swapaxes(0, 1).reshape(B, S, D_MODEL)
    o_ctx = attend(qc, kc, vc) if need_ctx else None
    return o_ctx, o_lat


def centred_depthwise_conv(x, w, b):
    pad = SSM_CONV // 2
    y = lax.conv_general_dilated(x, w[:, None, :].astype(x.dtype), window_strides=(1,),
                                 padding=[(pad, pad)], dimension_numbers=('NWC', 'WIO', 'NWC'),
                                 feature_group_count=x.shape[-1])
    return y + b


def ssd_scan(x, dt, A, Bm, Cm, D_skip, state0):
    Bsz, n = x.shape[:2]
    nc = n // SSM_CHUNK

    def chunks(t):
        return t.astype(jnp.float32).reshape((Bsz, nc, SSM_CHUNK) + t.shape[2:]).swapaxes(0, 1)

    causal = jnp.tril(jnp.ones((SSM_CHUNK, SSM_CHUNK), dtype=bool))

    def step(state, inp):
        xc, dtc, bc, cc = inp
        a = jnp.cumsum(dtc * A, axis=1)
        xdt = xc * dtc[..., None]
        a_h = jnp.swapaxes(a, 1, 2)
        seg = a_h[..., :, None] - a_h[..., None, :]
        decay = jnp.exp(jnp.where(causal, seg, -jnp.inf))
        cb = jnp.repeat(jnp.einsum('btgn,bsgn->bgts', cc, bc), HEADS_PER_GROUP, axis=1)
        y = jnp.einsum('bhts,bshp->bthp', cb * decay, xdt)
        ch = jnp.repeat(cc, HEADS_PER_GROUP, axis=2)
        y = y + jnp.einsum('bthn,bhpn->bthp', ch, state) * jnp.exp(a)[..., None]
        a_last = a[:, -1]
        bh = jnp.repeat(bc, HEADS_PER_GROUP, axis=2) * jnp.exp(a_last[:, None] - a)[..., None]
        state = state * jnp.exp(a_last)[..., None, None] + jnp.einsum('bshn,bshp->bhpn', bh, xdt)
        return state, y + D_skip[:, None] * xc

    final, ys = lax.scan(step, state0, (chunks(x), chunks(dt), chunks(Bm), chunks(Cm)))
    return ys.swapaxes(0, 1).reshape(Bsz, n, SSM_HEADS, SSM_HEAD_DIM), final


def mamba2_bidir(h_ctx, h_lat, w_in, conv_w, conv_b, dt_bias, a_log, d_skip, norm_g, w_out, need_ctx):
    A = -jnp.exp(a_log.astype(jnp.float32))
    Dk = d_skip.astype(jnp.float32)

    def project(h):
        Bsz, n, _ = h.shape
        z, xbc, dt = jnp.split(h @ w_in, [SSM_D_INNER, SSM_D_INNER + SSM_CONV_DIM], axis=-1)
        xbc = jax.nn.silu(centred_depthwise_conv(xbc, conv_w, conv_b))
        xs, bm, cm = jnp.split(xbc, [SSM_D_INNER, SSM_D_INNER + SSM_GROUPS * SSM_STATE], axis=-1)
        dt = jax.nn.softplus(dt.astype(jnp.float32).reshape(Bsz, n, 2, SSM_HEADS) + dt_bias.astype(jnp.float32))
        return (z, xs.reshape(Bsz, n, SSM_HEADS, SSM_HEAD_DIM),
                bm.reshape(Bsz, n, SSM_GROUPS, SSM_STATE),
                cm.reshape(Bsz, n, SSM_GROUPS, SSM_STATE), dt)

    zc, xc, bc, cc, dtc = project(h_ctx)
    zl, xl, bl, cl, dtl = project(h_lat)
    flip = lambda t: jnp.flip(t, axis=1)
    s0 = jnp.zeros((h_lat.shape[0], SSM_HEADS, SSM_HEAD_DIM, SSM_STATE), jnp.float32)
    yc_f, sc_f = ssd_scan(xc, dtc[:, :, 0], A[0], bc, cc, Dk[0], s0)
    yl_f, _ = ssd_scan(xl, dtl[:, :, 0], A[0], bl, cl, Dk[0], sc_f)
    yc_b, sc_b = ssd_scan(flip(xc), flip(dtc[:, :, 1]), A[1], flip(bc), flip(cc), Dk[1], s0)
    yl_b, _ = ssd_scan(flip(xl), flip(dtl[:, :, 1]), A[1], flip(bl), flip(cl), Dk[1], sc_b)

    def finish(y, z):
        Bsz, n = y.shape[:2]
        y = y.reshape(Bsz, n, SSM_D_INNER).astype(z.dtype)
        return rms_norm(y * jax.nn.silu(z), norm_g) @ w_out

    o_lat = finish(yl_f + flip(yl_b), zl)
    o_ctx = finish(yc_f + flip(yc_b), zc) if need_ctx else None
    return o_ctx, o_lat


def hier_moe(h, w_group, b_group, w_expert, b_expert, w_gate, w_up, w_down):
    T, D = h.shape
    g_prob = jax.nn.softmax((h @ w_group).astype(jnp.float32) + b_group, axis=-1)
    g_top, g_idx = lax.top_k(g_prob, 1)
    e_logits = ((h @ w_expert).astype(jnp.float32) + b_expert).reshape(T, MOE_GROUPS, MOE_PER_GROUP)
    e_in = jnp.take_along_axis(e_logits, g_idx[:, :, None], axis=1)[:, 0]
    e_top, e_idx = lax.top_k(e_in, MOE_TOP_K)
    gate = g_top * jax.nn.softmax(e_top, axis=-1)
    expert = g_idx * MOE_PER_GROUP + e_idx

    A_n = T * MOE_TOP_K
    flat_e = expert.reshape(A_n)
    order = jnp.argsort(flat_e)
    se = flat_e[order]
    tok = order // MOE_TOP_K
    sizes = jnp.bincount(flat_e, length=MOE_EXPERTS)
    starts = jnp.cumsum(sizes) - sizes
    padded = (sizes + MOE_BLOCK - 1) // MOE_BLOCK * MOE_BLOCK
    pad_end = jnp.cumsum(padded)
    pad_start = pad_end - padded
    dest = pad_start[se] + jnp.arange(A_n) - starts[se]
    n_blocks = -(-A_n // MOE_BLOCK) + MOE_EXPERTS
    buf = jnp.zeros((n_blocks * MOE_BLOCK, D), h.dtype).at[dest].set(h[tok])
    blk_expert = jnp.minimum(jnp.searchsorted(pad_end, jnp.arange(n_blocks) * MOE_BLOCK, side='right'),
                             MOE_EXPERTS - 1)

    def run(args):
        xb, e = args
        return (jax.nn.silu(xb @ w_gate[e]) * (xb @ w_up[e])) @ w_down[e]

    yb = lax.map(run, (buf.reshape(n_blocks, MOE_BLOCK, D), blk_expert)).reshape(-1, D)
    w_sorted = gate.reshape(A_n)[order]
    out = jnp.zeros((T, D), jnp.float32).at[tok].add(w_sorted[:, None] * yb[dest].astype(jnp.float32))
    return out.astype(h.dtype)


def setup_inputs(seed: int = 0) -> dict:
    key = jax.random.key(seed)
    ks = iter(jax.random.split(key, 40))
    D = D_MODEL
    NA, NM, E, F, H = N_ATTN_LAYERS, N_SSM_LAYERS, MOE_EXPERTS, MOE_D_FF, SSM_HEADS

    def nrm(shape, s):
        return s * jax.random.normal(next(ks), shape, jnp.float32)

    def gain(shape):
        return 1.0 + nrm(shape, 0.02)

    inp = {}
    inp['x'] = nrm((BATCH, SEQ, D), 1.0)
    inp['c'] = nrm((BATCH, D), 1.0)
    inp['ctx'] = nrm((BATCH, CTX_LEN, D), 1.0)
    inp['c_ctx'] = nrm((D,), 1.0)
    inp['w_mod'] = nrm((DEPTH, D, 6 * D), 0.5 * D ** -0.5)
    inp['b_mod'] = nrm((DEPTH, 6 * D), 0.01)
    inp['ln1_g'] = gain((DEPTH, D))
    inp['ln1_b'] = nrm((DEPTH, D), 0.02)
    inp['ln2_g'] = gain((DEPTH, D))
    inp['ln2_b'] = nrm((DEPTH, D), 0.02)
    inp['attn_w_qkv'] = nrm((NA, D, 3 * D), D ** -0.5)
    inp['attn_w_o'] = nrm((NA, DA_HEADS * DA_V_DIM, D), DEEPNORM_BETA * (DA_HEADS * DA_V_DIM) ** -0.5)
    inp['attn_lq1'] = nrm((NA, DA_HEAD_DIM), 0.1)
    inp['attn_lk1'] = nrm((NA, DA_HEAD_DIM), 0.1)
    inp['attn_lq2'] = nrm((NA, DA_HEAD_DIM), 0.1)
    inp['attn_lk2'] = nrm((NA, DA_HEAD_DIM), 0.1)
    inp['attn_subln_g'] = gain((NA, DA_V_DIM))
    inp['ssm_w_in'] = nrm((NM, D, SSM_IN_DIM), D ** -0.5)
    inp['ssm_conv_w'] = nrm((NM, SSM_CONV, SSM_CONV_DIM), SSM_CONV ** -0.5)
    inp['ssm_conv_b'] = nrm((NM, SSM_CONV_DIM), 0.02)
    dt0 = jnp.exp(jax.random.uniform(next(ks), (NM, 2, H), jnp.float32,
                                     minval=math.log(1e-3), maxval=math.log(1e-1)))
    inp['ssm_dt_bias'] = dt0 + jnp.log(-jnp.expm1(-dt0))
    inp['ssm_a_log'] = jnp.log(jax.random.uniform(next(ks), (NM, 2, H), jnp.float32, minval=1.0, maxval=16.0))
    inp['ssm_d'] = gain((NM, 2, H))
    inp['ssm_norm_g'] = gain((NM, SSM_D_INNER))
    inp['ssm_w_out'] = nrm((NM, SSM_D_INNER, D), DEEPNORM_BETA * SSM_D_INNER ** -0.5)
    inp['moe_w_group'] = nrm((DEPTH, D, MOE_GROUPS), D ** -0.5)
    inp['moe_b_group'] = nrm((DEPTH, MOE_GROUPS), 0.01)
    inp['moe_w_expert'] = nrm((DEPTH, D, E), D ** -0.5)
    inp['moe_b_expert'] = nrm((DEPTH, E), 0.01)
    inp['moe_w_gate'] = nrm((DEPTH, E, D, F), D ** -0.5)
    inp['moe_w_up'] = nrm((DEPTH, E, D, F), D ** -0.5)
    inp['moe_w_down'] = nrm((DEPTH, E, F, D), DEEPNORM_BETA * F ** -0.5)
    return inp


def reference(x, c, ctx, c_ctx, w_mod, b_mod, ln1_g, ln1_b, ln2_g, ln2_b,
              attn_w_qkv, attn_w_o, attn_lq1, attn_lk1, attn_lq2, attn_lk2, attn_subln_g,
              ssm_w_in, ssm_conv_w, ssm_conv_b, ssm_dt_bias, ssm_a_log, ssm_d, ssm_norm_g, ssm_w_out,
              moe_w_group, moe_b_group, moe_w_expert, moe_b_expert, moe_w_gate, moe_w_up, moe_w_down):
    B, S, D = x.shape
    C = ctx.shape[1]
    rows = S // GRID_W
    cos, sin = axial_rope_tables(rows)
    for i in range(DEPTH):
        last = i == DEPTH - 1
        j = i // N_MIXERS
        sh1, sc1, g1, sh2, sc2, g2 = modulation(c, w_mod[i], b_mod[i])
        csh1, csc1, cg1, csh2, csc2, cg2 = modulation(c_ctx, w_mod[i], b_mod[i])
        h_lat = x * (1.0 + sc1) + sh1
        h_ctx = ctx * (1.0 + csc1) + csh1
        if i % N_MIXERS == 0:
            lambda_init = 0.8 - 0.6 * math.exp(-0.3 * i)
            o_ctx, o_lat = diff_attention(h_ctx, h_lat, cos, sin, attn_w_qkv[j], attn_w_o[j],
                                          attn_lq1[j], attn_lk1[j], attn_lq2[j], attn_lk2[j],
                                          attn_subln_g[j], lambda_init, not last)
        else:
            o_ctx, o_lat = mamba2_bidir(h_ctx, h_lat, ssm_w_in[j], ssm_conv_w[j], ssm_conv_b[j],
                                        ssm_dt_bias[j], ssm_a_log[j], ssm_d[j], ssm_norm_g[j],
                                        ssm_w_out[j], not last)
        x = layer_norm(DEEPNORM_ALPHA * x + g1 * o_lat, ln1_g[i], ln1_b[i])
        h_lat = x * (1.0 + sc2) + sh2
        moe_args = (moe_w_group[i], moe_b_group[i], moe_w_expert[i], moe_b_expert[i],
                    moe_w_gate[i], moe_w_up[i], moe_w_down[i])
        if last:
            f_lat = hier_moe(h_lat.reshape(B * S, D), *moe_args).reshape(B, S, D)
        else:
            ctx = layer_norm(DEEPNORM_ALPHA * ctx + cg1 * o_ctx, ln1_g[i], ln1_b[i])
            h_ctx = ctx * (1.0 + csc2) + csh2
            f_all = hier_moe(jnp.concatenate([h_ctx.reshape(B * C, D), h_lat.reshape(B * S, D)], axis=0),
                             *moe_args)
            f_ctx = f_all[:B * C].reshape(B, C, D)
            f_lat = f_all[B * C:].reshape(B, S, D)
            ctx = layer_norm(DEEPNORM_ALPHA * ctx + cg2 * f_ctx, ln2_g[i], ln2_b[i])
        x = layer_norm(DEEPNORM_ALPHA * x + g2 * f_lat, ln2_g[i], ln2_b[i])
    return x
```

```python
import functools
import math

import jax
import jax.numpy as jnp
from jax import lax
from jax.experimental import pallas as pl
from jax.experimental.pallas import tpu as pltpu

F32 = jnp.float32
BF16 = jnp.bfloat16
HIGHEST = lax.Precision.HIGHEST

GRID_W = 64
DA_HEADS = 8
DA_HEAD_DIM = 64
DA_V_DIM = 2 * DA_HEAD_DIM
ROPE_THETA = 10000.0
SSM_HEAD_DIM = 64
SSM_GROUPS = 4
SSM_STATE = 128
SSM_CONV = 5
SSM_CHUNK = 128
MOE_GROUPS = 4
MOE_PER_GROUP = 8
MOE_TOP_K = 2
LN_EPS = 1e-5
RMS_EPS = 1e-5
N_MIXERS = 2

LANES = 128
SUBLANES = 8
TM = 256
MOE_BM = 256
ATTN_TK = 768
VMEM_LIMIT = 56 * 1024 * 1024


def _cparams(*sem):
    return pltpu.CompilerParams(dimension_semantics=sem, vmem_limit_bytes=VMEM_LIMIT)


def _silu(v):
    return v / (1.0 + jnp.exp(-v))


def _layer_norm(r, g, b):
    mu = jnp.mean(r, axis=-1, keepdims=True)
    d = r - mu
    var = jnp.mean(d * d, axis=-1, keepdims=True)
    return d * lax.rsqrt(var + LN_EPS) * g + b


def _mod_row(ctx_row):
    return lambda b, t: (jnp.where(t == 0, ctx_row, b), 0, 0)


def _mod_kernel(c_ref, w_ref, b_ref, o_ref):
    s = _silu(c_ref[...])
    o_ref[0] = jnp.dot(s, w_ref[0], precision=HIGHEST, preferred_element_type=F32) + b_ref[0]


def _modulation(cond, w_mod, b_mod):
    L, D, D6 = w_mod.shape
    R = cond.shape[0]
    tn = 1536
    return pl.pallas_call(
        _mod_kernel,
        grid=(L, D6 // tn),
        in_specs=[pl.BlockSpec((R, D), lambda l, j: (0, 0)),
                  pl.BlockSpec((1, D, tn), lambda l, j: (l, 0, j)),
                  pl.BlockSpec((1, 1, tn), lambda l, j: (l, 0, j))],
        out_specs=pl.BlockSpec((1, R, tn), lambda l, j: (l, 0, j)),
        out_shape=jax.ShapeDtypeStruct((L, R, D6), F32),
        compiler_params=_cparams("parallel", "parallel"),
        name="modulation",
    )(cond, w_mod, b_mod.reshape(L, 1, D6))


def _qkv_kernel(x_ref, mod_ref, w_ref, cos_ref, sa_ref, sb_ref, o_ref, *, d_model, n_rope, n_scaled, scale):
    D = d_model
    x = x_ref[0]
    h = (x * (1.0 + mod_ref[0, :, D:2 * D]) + mod_ref[0, :, 0:D]).astype(BF16)
    cos = cos_ref[...]
    sa = sa_ref[...]
    sb = sb_ref[...]
    n_blocks = o_ref.shape[0]
    for jj in range(n_blocks // 2):
        acc2 = jnp.dot(h, w_ref[:, jj * 2 * LANES:(jj + 1) * 2 * LANES], preferred_element_type=F32)
        for half in range(2):
            j = 2 * jj + half
            acc = acc2[:, half * LANES:(half + 1) * LANES]
            if j < n_rope:
                acc = acc * cos + pltpu.roll(acc, LANES - 16, 1) * sa + pltpu.roll(acc, 16, 1) * sb
            if j < n_scaled:
                acc = acc * scale
            o_ref[j, 0] = acc.astype(BF16)


def _qkv_proj(x_all, mod, w_qkv, cos, sa, sb, ctx_row):
    B, N, D = x_all.shape
    n_out = w_qkv.shape[1] // LANES
    n_heads_lanes = DA_HEADS * DA_V_DIM // LANES
    kern = functools.partial(_qkv_kernel, d_model=D, n_rope=2 * n_heads_lanes, n_scaled=n_heads_lanes,
                             scale=DA_HEAD_DIM ** -0.5)
    return pl.pallas_call(
        kern,
        grid=(B, N // TM),
        in_specs=[pl.BlockSpec((1, TM, D), lambda b, t: (b, t, 0)),
                  pl.BlockSpec((1, 1, 6 * D), _mod_row(ctx_row)),
                  pl.BlockSpec((D, w_qkv.shape[1]), lambda b, t: (0, 0)),
                  pl.BlockSpec((TM, LANES), lambda b, t: (t, 0)),
                  pl.BlockSpec((TM, LANES), lambda b, t: (t, 0)),
                  pl.BlockSpec((TM, LANES), lambda b, t: (t, 0))],
        out_specs=pl.BlockSpec((n_out, 1, TM, LANES), lambda b, t: (0, b, t, 0)),
        out_shape=jax.ShapeDtypeStruct((n_out, B, N, LANES), BF16),
        compiler_params=_cparams("parallel", "parallel"),
        name="qkv_proj",
    )(x_all, mod, w_qkv, cos, sa, sb)


def _rope_tables(n_ctx, n_lat):
    t = jnp.arange(n_lat)
    pos = jnp.stack([t // GRID_W, t % GRID_W], axis=1).astype(F32)
    axis_dims = DA_HEAD_DIM // 2
    inv = ROPE_THETA ** (-jnp.arange(0, axis_dims, 2, dtype=F32) / axis_dims)
    lane = jnp.arange(LANES)
    d = lane % DA_HEAD_DIM
    axis = d // axis_dims
    second = (d % axis_dims) // (axis_dims // 2)
    ang = pos[:, axis] * inv[d % (axis_dims // 2)][None, :]
    cos = jnp.cos(ang)
    sin = jnp.sin(ang)
    sa = jnp.where(second[None, :] == 0, -sin, 0.0)
    sb = jnp.where(second[None, :] == 1, sin, 0.0)
    pad = lambda a, v: jnp.concatenate([jnp.full((n_ctx, LANES), v, F32), a], axis=0)
    return pad(cos, 1.0), pad(sa, 0.0), pad(sb, 0.0)


def _attn_tile(q, k_ref, v_ref, n_chunks, tk):
    tq = q.shape[0]
    hd = DA_HEAD_DIM
    qs = [q[:, 0:hd], q[:, hd:2 * hd]]

    def body(c, carry):
        off = 0 if n_chunks == 1 else pl.multiple_of(c * tk, tk)
        k = k_ref[0, 0, pl.ds(off, tk), :]
        v = v_ref[0, 0, pl.ds(off, tk), :]
        new = []
        for j in range(2):
            m, l, acc = carry[j]
            s = lax.dot_general(qs[j], k[:, j * hd:(j + 1) * hd], (((1,), (1,)), ((), ())),
                                preferred_element_type=F32)
            m_new = jnp.maximum(m, jnp.max(s, axis=-1, keepdims=True))
            alpha = jnp.exp(m - m_new)
            p = jnp.exp(s - m_new)
            l = alpha * l + jnp.sum(p, axis=-1, keepdims=True)
            acc = alpha * acc + jnp.dot(p.astype(BF16), v, preferred_element_type=F32)
            new.append((m_new, l, acc))
        return tuple(new)

    init = tuple((jnp.full((tq, 1), -jnp.inf, F32), jnp.zeros((tq, 1), F32), jnp.zeros((tq, DA_V_DIM), F32))
                 for _ in range(2))
    if n_chunks == 1:
        out = body(0, init)
    else:
        out = lax.fori_loop(0, n_chunks, body, init)
    return out[0][2] / out[0][1], out[1][2] / out[1][1]


def _attn_kernel(lam_ref, g_ref, q_ref, k_ref, v_ref, o_ref, *, n_ctx, n_all, tk, lambda_init):
    t = pl.program_id(2)
    lp = lam_ref[...]
    lam = (jnp.exp(jnp.sum(lp[0:1] * lp[1:2], axis=-1, keepdims=True))
           - jnp.exp(jnp.sum(lp[2:3] * lp[3:4], axis=-1, keepdims=True)) + lambda_init)

    def finish(o0, o1):
        o = o0 - lam * o1
        ms = jnp.mean(o * o, axis=-1, keepdims=True)
        o = o * lax.rsqrt(ms + RMS_EPS) * g_ref[...] * (1.0 - lambda_init)
        o_ref[0] = o.astype(o_ref.dtype)

    @pl.when(t == 0)
    def _():
        finish(*_attn_tile(q_ref[0, 0], k_ref, v_ref, 1, n_ctx))

    @pl.when(t > 0)
    def _():
        finish(*_attn_tile(q_ref[0, 0], k_ref, v_ref, n_all // tk, tk))


def _diff_attention(qkv, lam_params, subln_g, lambda_init, n_ctx):
    n_out, B, N, _ = qkv.shape
    H = DA_HEADS
    assert n_ctx == TM and N % ATTN_TK == 0
    kern = functools.partial(_attn_kernel, n_ctx=n_ctx, n_all=N, tk=ATTN_TK, lambda_init=lambda_init)
    return pl.pallas_call(
        kern,
        grid=(B, H, N // TM),
        in_specs=[pl.BlockSpec(lam_params.shape, lambda b, h, t: (0, 0)),
                  pl.BlockSpec((1, DA_V_DIM), lambda b, h, t: (0, 0)),
                  pl.BlockSpec((1, 1, TM, LANES), lambda b, h, t: (h, b, t, 0)),
                  pl.BlockSpec((1, 1, N, LANES), lambda b, h, t: (H + h, b, 0, 0)),
                  pl.BlockSpec((1, 1, N, LANES), lambda b, h, t: (2 * H + h, b, 0, 0))],
        out_specs=pl.BlockSpec((1, TM, DA_V_DIM), lambda b, h, t: (b, t, h)),
        out_shape=jax.ShapeDtypeStruct((B, N, H * DA_V_DIM), BF16),
        compiler_params=_cparams("parallel", "parallel", "parallel"),
        name="diff_attention",
    )(lam_params, subln_g.reshape(1, DA_V_DIM), qkv, qkv, qkv)


def _proj_ln_tail(a, x_ref, mod_ref, w_ref, lng_ref, lnb_ref, wr_ref, br_ref, x_out, h_out, lg_out, *, d_model, alpha):
    D = d_model
    o = jnp.dot(a, w_ref[...], preferred_element_type=F32)
    r = alpha * x_ref[0] + mod_ref[0, :, 2 * D:3 * D] * o
    xn = _layer_norm(r, lng_ref[...], lnb_ref[...])
    x_out[0] = xn
    h2 = xn * (1.0 + mod_ref[0, :, 4 * D:5 * D]) + mod_ref[0, :, 3 * D:4 * D]
    h_out[0] = h2.astype(h_out.dtype)
    lg_out[0] = jnp.dot(h2, wr_ref[...], precision=HIGHEST, preferred_element_type=F32) + br_ref[...]


def _attn_out_kernel(a_ref, *refs, **kw):
    _proj_ln_tail(a_ref[0], *refs, **kw)


def _ssm_out_kernel(yf_ref, yb_ref, z_ref, ng_ref, *refs, **kw):
    y = yf_ref[0] + yb_ref[0]
    gz = y * _silu(z_ref[0].astype(F32))
    ms = jnp.mean(gz * gz, axis=-1, keepdims=True)
    a = (gz * lax.rsqrt(ms + RMS_EPS) * ng_ref[...]).astype(BF16)
    _proj_ln_tail(a, *refs, **kw)


def _proj_ln(kernel_fn, lead_args, lead_specs, x_all, mod, w, ln_g, ln_b, w_router, b_router, ctx_row, alpha, name):
    B, N, D = x_all.shape
    row = lambda b, t: (b, t, 0)
    const2 = lambda b, t: (0, 0)
    kern = functools.partial(kernel_fn, d_model=D, alpha=alpha)
    return pl.pallas_call(
        kern,
        grid=(B, N // TM),
        in_specs=lead_specs + [
            pl.BlockSpec((1, TM, D), row),
            pl.BlockSpec((1, 1, 6 * D), _mod_row(ctx_row)),
            pl.BlockSpec(w.shape, const2),
            pl.BlockSpec((1, D), const2),
            pl.BlockSpec((1, D), const2),
            pl.BlockSpec(w_router.shape, const2),
            pl.BlockSpec((1, LANES), const2)],
        out_specs=[pl.BlockSpec((1, TM, D), row), pl.BlockSpec((1, TM, D), row), pl.BlockSpec((1, TM, LANES), row)],
        out_shape=[jax.ShapeDtypeStruct((B, N, D), F32), jax.ShapeDtypeStruct((B, N, D), BF16),
                   jax.ShapeDtypeStruct((B, N, LANES), F32)],
        compiler_params=_cparams("parallel", "parallel"),
        name=name,
    )(*lead_args, x_all, mod, w, ln_g.reshape(1, D), ln_b.reshape(1, D), w_router, b_router)


def _moe_kernel(be_ref, nu_ref, x_ref, wgu_ref, wd_ref, o_ref, *, d_ff):
    i = pl.program_id(0)

    @pl.when(i < nu_ref[0])
    def _():
        gu = jnp.dot(x_ref[...], wgu_ref[0], preferred_element_type=F32)
        mid = (_silu(gu[:, :d_ff]) * gu[:, d_ff:]).astype(BF16)
        o_ref[...] = jnp.dot(mid, wd_ref[0], preferred_element_type=F32)

    @pl.when(i >= nu_ref[0])
    def _():
        o_ref[...] = jnp.zeros_like(o_ref)


def _moe_experts(buf, blk_expert, n_used, wgu, wd):
    n_rows, D = buf.shape
    d_ff = wd.shape[1]
    grid_spec = pltpu.PrefetchScalarGridSpec(
        num_scalar_prefetch=2,
        grid=(n_rows // MOE_BM,),
        in_specs=[pl.BlockSpec((MOE_BM, D), lambda i, be, nu: (i, 0)),
                  pl.BlockSpec((1, D, 2 * d_ff), lambda i, be, nu: (be[i], 0, 0)),
                  pl.BlockSpec((1, d_ff, D), lambda i, be, nu: (be[i], 0, 0))],
        out_specs=pl.BlockSpec((MOE_BM, D), lambda i, be, nu: (i, 0)))
    return pl.pallas_call(
        functools.partial(_moe_kernel, d_ff=d_ff),
        grid_spec=grid_spec,
        out_shape=jax.ShapeDtypeStruct((n_rows, D), F32),
        compiler_params=_cparams("arbitrary"),
        name="moe_experts",
    )(blk_expert, n_used, buf, wgu, wd)


def _route(logits, n_experts):
    T = logits.shape[0]
    g_prob = jax.nn.softmax(logits[:, :MOE_GROUPS], axis=-1)
    g_top, g_idx = lax.top_k(g_prob, 1)
    e_logits = logits[:, MOE_GROUPS:MOE_GROUPS + n_experts].reshape(T, MOE_GROUPS, MOE_PER_GROUP)
    e_in = jnp.take_along_axis(e_logits, g_idx[:, :, None], axis=1)[:, 0]
    e_top, e_idx = lax.top_k(e_in, MOE_TOP_K)
    gate = g_top * jax.nn.softmax(e_top, axis=-1)
    expert = g_idx * MOE_PER_GROUP + e_idx
    return expert.astype(jnp.int32), gate


def _dispatch(expert, n_experts):
    T, K = expert.shape
    A = T * K
    flat_e = expert.reshape(A)
    order = jnp.argsort(flat_e)
    se = flat_e[order]
    starts = jnp.searchsorted(se, jnp.arange(n_experts, dtype=jnp.int32), side='left').astype(jnp.int32)
    ends = jnp.concatenate([starts[1:], jnp.array([A], jnp.int32)])
    sizes = ends - starts
    padded = (sizes + MOE_BM - 1) // MOE_BM * MOE_BM
    pad_end = jnp.cumsum(padded)
    pad_start = pad_end - padded
    n_blocks = -(-A // MOE_BM) + n_experts
    blk_start = jnp.arange(n_blocks, dtype=jnp.int32) * MOE_BM
    blk_expert = jnp.minimum(jnp.searchsorted(pad_end, blk_start, side='right'), n_experts - 1).astype(jnp.int32)
    n_used = (pad_end[-1] // MOE_BM).astype(jnp.int32).reshape(1)
    slot = jnp.arange(n_blocks * MOE_BM, dtype=jnp.int32)
    s_e = jnp.repeat(blk_expert, MOE_BM)
    within = slot - pad_start[s_e]
    src = jnp.clip(starts[s_e] + within, 0, A - 1)
    slot_tok = jnp.where(within < sizes[s_e], order[src] // K, 0).astype(jnp.int32)
    dest_sorted = pad_start[se] + jnp.arange(A, dtype=jnp.int32) - starts[se]
    dest = jnp.zeros((A,), jnp.int32).at[order].set(dest_sorted.astype(jnp.int32), unique_indices=True)
    return slot_tok, blk_expert, n_used, dest.reshape(T, K)


def _combine_kernel(x_ref, y0_ref, y1_ref, gt_ref, mod_ref, lng_ref, lnb_ref, o_ref, *, d_model, alpha):
    D = d_model
    gt = gt_ref[0]
    f = gt[:, 0:1] * y0_ref[0, 0] + gt[:, 1:2] * y1_ref[0, 0]
    r = alpha * x_ref[0] + mod_ref[0, :, 5 * D:6 * D] * f
    o_ref[0] = _layer_norm(r, lng_ref[...], lnb_ref[...])


def _combine_ln(x_all, ys, gates, mod, ln_g, ln_b, ctx_row, alpha, t0):
    B, N, D = x_all.shape
    nt = N // TM - t0
    tiles_per_batch = N // TM
    row = lambda b, t: (b, t + t0, 0)
    const2 = lambda b, t: (0, 0)
    mrow = _mod_row(ctx_row)
    return pl.pallas_call(
        functools.partial(_combine_kernel, d_model=D, alpha=alpha),
        grid=(B, nt),
        in_specs=[pl.BlockSpec((1, TM, D), row),
                  pl.BlockSpec((1, 1, TM, D), lambda b, t: (0, b * tiles_per_batch + t + t0, 0, 0)),
                  pl.BlockSpec((1, 1, TM, D), lambda b, t: (1, b * tiles_per_batch + t + t0, 0, 0)),
                  pl.BlockSpec((1, TM, MOE_TOP_K), row),
                  pl.BlockSpec((1, 1, 6 * D), lambda b, t: mrow(b, t + t0)),
                  pl.BlockSpec((1, D), const2),
                  pl.BlockSpec((1, D), const2)],
        out_specs=pl.BlockSpec((1, TM, D), lambda b, t: (b, t, 0)),
        out_shape=jax.ShapeDtypeStruct((B, nt * TM, D), F32),
        compiler_params=_cparams("parallel", "parallel"),
        name="moe_combine_ln",
    )(x_all, ys, ys, gates, mod, ln_g.reshape(1, D), ln_b.reshape(1, D))


def _moe_layer(x1, h2, logits, mod, w_gate, w_up, w_down, ln_g, ln_b, ctx_row, alpha, t0):
    B, N, D = x1.shape
    E = w_gate.shape[0]
    T = B * N
    expert, gate = _route(logits.reshape(T, LANES), E)
    slot_tok, blk_expert, n_used, dest = _dispatch(expert, E)
    buf = jnp.take(h2.reshape(T, D), slot_tok, axis=0, mode='clip')
    wgu = jnp.concatenate([w_gate, w_up], axis=-1).astype(BF16)
    yb = _moe_experts(buf, blk_expert, n_used, wgu, w_down.astype(BF16))
    ys = jnp.take(yb, dest.T, axis=0, mode='clip').reshape(MOE_TOP_K, T // TM, TM, D)
    return _combine_ln(x1, ys, gate.reshape(B, N, MOE_TOP_K), mod, ln_g, ln_b, ctx_row, alpha, t0)


def _softplus(v):
    return jnp.maximum(v, 0.0) + jnp.log1p(jnp.exp(-jnp.abs(v)))


def _ssm_in_kernel(x_ref, mod_ref, w_ref, dtb_ref, z_out, xbc_out, dt_out, dtt_out, *, d_model, d_inner, d_conv):
    D = d_model
    h = (x_ref[0] * (1.0 + mod_ref[0, :, D:2 * D]) + mod_ref[0, :, 0:D]).astype(BF16)
    z_out[0] = jnp.dot(h, w_ref[:, 0:d_inner], preferred_element_type=F32).astype(z_out.dtype)
    cw = 512
    for j in range(d_conv // cw):
        xbc_out[0, :, j * cw:(j + 1) * cw] = jnp.dot(
            h, w_ref[:, d_inner + j * cw:d_inner + (j + 1) * cw], preferred_element_type=F32)
    dt = jnp.dot(h, w_ref[:, d_inner + d_conv:], preferred_element_type=F32) + dtb_ref[...]
    dt = _softplus(dt)
    dt_out[0] = dt
    dtt_out[0] = dt.T


def _ssm_in_proj(x_all, mod, w_in_p, dt_bias_p, ctx_row, d_inner, d_conv):
    B, N, D = x_all.shape
    row = lambda b, t: (b, t, 0)
    kern = functools.partial(_ssm_in_kernel, d_model=D, d_inner=d_inner, d_conv=d_conv)
    return pl.pallas_call(
        kern,
        grid=(B, N // TM),
        in_specs=[pl.BlockSpec((1, TM, D), row),
                  pl.BlockSpec((1, 1, 6 * D), _mod_row(ctx_row)),
                  pl.BlockSpec(w_in_p.shape, lambda b, t: (0, 0)),
                  pl.BlockSpec((1, LANES), lambda b, t: (0, 0))],
        out_specs=[pl.BlockSpec((1, TM, d_inner), row), pl.BlockSpec((1, TM, d_conv), row),
                   pl.BlockSpec((1, TM, LANES), row), pl.BlockSpec((1, LANES, TM), lambda b, t: (b, 0, t))],
        out_shape=[jax.ShapeDtypeStruct((B, N, d_inner), BF16), jax.ShapeDtypeStruct((B, N, d_conv), F32),
                   jax.ShapeDtypeStruct((B, N, LANES), F32), jax.ShapeDtypeStruct((B, LANES, N), F32)],
        compiler_params=_cparams("parallel", "parallel"),
        name="ssm_in_proj",
    )(x_all, mod, w_in_p, dt_bias_p)


def _conv_kernel(x_ref, prev_ref, next_ref, w_ref, b_ref, o_ref, ext_ref, *, n_tiles):
    t = pl.program_id(1)
    halo = SUBLANES
    pad = SSM_CONV // 2
    has_prev = t >= 2
    has_next = jnp.logical_and(t >= 1, t < n_tiles - 1)
    ext_ref[0:halo, :] = jnp.where(has_prev, prev_ref[0], 0.0)
    ext_ref[halo:halo + TM, :] = x_ref[0]
    ext_ref[halo + TM:, :] = jnp.where(has_next, next_ref[0], 0.0)
    cw = 512
    for j in range(o_ref.shape[2] // cw):
        cols = slice(j * cw, (j + 1) * cw)
        acc = b_ref[:, cols] + w_ref[0:1, cols] * ext_ref[halo - pad:halo - pad + TM, cols]
        for k in range(1, SSM_CONV):
            acc = acc + w_ref[k:k + 1, cols] * ext_ref[halo - pad + k:halo - pad + k + TM, cols]
        o_ref[0, :, cols] = _silu(acc)


def _ssm_conv(xbc, conv_w_p, conv_b):
    B, N, C = xbc.shape
    n_tiles = N // TM
    hb = TM // SUBLANES
    last_hb = N // SUBLANES - 1
    return pl.pallas_call(
        functools.partial(_conv_kernel, n_tiles=n_tiles),
        grid=(B, n_tiles),
        in_specs=[pl.BlockSpec((1, TM, C), lambda b, t: (b, t, 0)),
                  pl.BlockSpec((1, SUBLANES, C), lambda b, t: (b, jnp.maximum(t * hb - 1, 0), 0)),
                  pl.BlockSpec((1, SUBLANES, C), lambda b, t: (b, jnp.minimum((t + 1) * hb, last_hb), 0)),
                  pl.BlockSpec(conv_w_p.shape, lambda b, t: (0, 0)),
                  pl.BlockSpec((1, C), lambda b, t: (0, 0))],
        out_specs=pl.BlockSpec((1, TM, C), lambda b, t: (b, t, 0)),
        out_shape=jax.ShapeDtypeStruct((B, N, C), F32),
        scratch_shapes=[pltpu.VMEM((TM + 2 * SUBLANES, C), F32)],
        compiler_params=_cparams("parallel", "parallel"),
        name="ssm_conv",
    )(xbc, xbc, xbc, conv_w_p, conv_b.reshape(1, C))


def _ssd_kernel(x_ref, b_ref, c_ref, dt_ref, dtt_ref, alr_ref, alc_ref, dsk_ref, y_ref, state_ref, xw_ref, *,
                direction, n_heads):
    L = SSM_CHUNK
    P = SSM_HEAD_DIM
    NS = SSM_STATE
    hpg = n_heads // SSM_GROUPS
    gw = hpg * P
    reverse = direction == 1

    @pl.when(pl.program_id(1) == 0)
    def _():
        state_ref[...] = jnp.zeros_like(state_ref)

    x = x_ref[0]
    dt = dt_ref[0]
    ti = lax.broadcasted_iota(jnp.int32, (L, L), 0)
    si = lax.broadcasted_iota(jnp.int32, (L, L), 1)
    mask = (si >= ti) if reverse else (si <= ti)
    mask_t = (ti >= si) if reverse else (ti <= si)
    a = jnp.dot(mask.astype(F32), dt * -jnp.exp(alr_ref[...]), precision=HIGHEST, preferred_element_type=F32)
    at = jnp.dot(dtt_ref[0] * -jnp.exp(alc_ref[...]), mask_t.astype(F32), precision=HIGHEST,
                 preferred_element_type=F32)
    tot = a[0:1, :] if reverse else a[L - 1:L, :]
    ea = jnp.exp(a)
    wgt = jnp.exp(tot - a)
    etot = jnp.exp(tot)
    dx = x * dsk_ref[direction:direction + 1, :]

    for g in range(SSM_GROUPS):
        bg = b_ref[0][:, g * NS:(g + 1) * NS].astype(BF16)
        cg = c_ref[0][:, g * NS:(g + 1) * NS].astype(BF16)
        cb = lax.dot_general(cg, bg, (((1,), (1,)), ((), ())), preferred_element_type=F32)
        yoff = jnp.dot(cg, state_ref[:, g * gw:(g + 1) * gw].astype(BF16), preferred_element_type=F32)
        for hh in range(hpg):
            h = g * hpg + hh
            ln = direction * n_heads + h
            cols = slice(h * P, (h + 1) * P)
            decay = jnp.where(mask, jnp.exp(a[:, ln:ln + 1] - at[ln:ln + 1, :]), 0.0)
            m = (cb * decay).astype(BF16)
            xdt = x[:, cols] * dt[:, ln:ln + 1]
            y_ref[0, :, cols] = (jnp.dot(m, xdt.astype(BF16), preferred_element_type=F32)
                                 + yoff[:, hh * P:(hh + 1) * P] * ea[:, ln:ln + 1] + dx[:, cols])
            xw_ref[:, hh * P:(hh + 1) * P] = xdt * wgt[:, ln:ln + 1]
        upd = lax.dot_general(bg, xw_ref[...].astype(BF16), (((0,), (0,)), ((), ())),
                              preferred_element_type=F32)
        for hh in range(hpg):
            h = g * hpg + hh
            ln = direction * n_heads + h
            cols = slice(h * P, (h + 1) * P)
            state_ref[:, cols] = state_ref[:, cols] * etot[:, ln:ln + 1] + upd[:, hh * P:(hh + 1) * P]


def _ssd_scan(xbc, dt, dtt, a_log, d_skip_x, direction, n_ctx):
    B, N, _ = xbc.shape
    n_heads = a_log.shape[1]
    d_inner = n_heads * SSM_HEAD_DIM
    gs = SSM_GROUPS * SSM_STATE
    L = SSM_CHUNK
    nc = N // L
    ncc = n_ctx // L
    if direction == 0:
        chunk = lambda c: c
    else:
        chunk = lambda c: jnp.where(c < ncc, ncc - 1 - c, nc - 1 - (c - ncc))
    kern = functools.partial(_ssd_kernel, direction=direction, n_heads=n_heads)
    a_log_lanes = jnp.pad(a_log.reshape(-1), (0, LANES - a_log.size))
    return pl.pallas_call(
        kern,
        grid=(B, nc),
        in_specs=[pl.BlockSpec((1, L, d_inner), lambda b, c: (b, chunk(c), 0)),
                  pl.BlockSpec((1, L, gs), lambda b, c: (b, chunk(c), d_inner // gs)),
                  pl.BlockSpec((1, L, gs), lambda b, c: (b, chunk(c), d_inner // gs + 1)),
                  pl.BlockSpec((1, L, LANES), lambda b, c: (b, chunk(c), 0)),
                  pl.BlockSpec((1, LANES, L), lambda b, c: (b, 0, chunk(c))),
                  pl.BlockSpec((1, LANES), lambda b, c: (0, 0)),
                  pl.BlockSpec((LANES, 1), lambda b, c: (0, 0)),
                  pl.BlockSpec(d_skip_x.shape, lambda b, c: (0, 0))],
        out_specs=pl.BlockSpec((1, L, d_inner), lambda b, c: (b, chunk(c), 0)),
        out_shape=jax.ShapeDtypeStruct((B, N, d_inner), F32),
        scratch_shapes=[pltpu.VMEM((SSM_STATE, d_inner), F32),
                        pltpu.VMEM((L, d_inner // SSM_GROUPS), F32)],
        compiler_params=_cparams("parallel", "arbitrary"),
        name=f"ssd_scan_dir{direction}",
    )(xbc, xbc, xbc, dt, dtt, a_log_lanes.reshape(1, LANES), a_log_lanes.reshape(LANES, 1), d_skip_x)


def kernel(x, c, ctx, c_ctx, w_mod, b_mod, ln1_g, ln1_b, ln2_g, ln2_b, attn_w_qkv, attn_w_o, attn_lq1, attn_lk1, attn_lq2, attn_lk2, attn_subln_g, ssm_w_in, ssm_conv_w, ssm_conv_b, ssm_dt_bias, ssm_a_log, ssm_d, ssm_norm_g, ssm_w_out, moe_w_group, moe_b_group, moe_w_expert, moe_b_expert, moe_w_gate, moe_w_up, moe_w_down):
    B, S, D = x.shape
    C = ctx.shape[1]
    depth = w_mod.shape[0]
    E = moe_w_expert.shape[-1]
    assert C == TM and S % TM == 0 and B < SUBLANES
    alpha = (2 * depth) ** 0.25
    ctx_row = B

    cond = jnp.zeros((SUBLANES, D), F32).at[:B].set(c).at[B].set(c_ctx)
    mods = _modulation(cond, w_mod, b_mod)
    x_all = jnp.concatenate([ctx, x], axis=1)
    cos, sa, sb = _rope_tables(C, S)

    for i in range(depth):
        last = i == depth - 1
        j = i // N_MIXERS
        mod = mods[i].reshape(SUBLANES, 1, 6 * D)
        w_router = jnp.zeros((D, LANES), F32).at[:, :MOE_GROUPS].set(moe_w_group[i]) \
            .at[:, MOE_GROUPS:MOE_GROUPS + E].set(moe_w_expert[i])
        b_router = jnp.zeros((1, LANES), F32).at[0, :MOE_GROUPS].set(moe_b_group[i]) \
            .at[0, MOE_GROUPS:MOE_GROUPS + E].set(moe_b_expert[i])
        if i % N_MIXERS == 0:
            lambda_init = 0.8 - 0.6 * math.exp(-0.3 * i)
            qkv = _qkv_proj(x_all, mod, attn_w_qkv[j].astype(BF16), cos, sa, sb, ctx_row)
            lam_params = jnp.stack([attn_lq1[j], attn_lk1[j], attn_lq2[j], attn_lk2[j]])
            o = _diff_attention(qkv, lam_params, attn_subln_g[j], lambda_init, C)
            lead_args = [o]
            lead_specs = [pl.BlockSpec((1, TM, o.shape[-1]), lambda b, t: (b, t, 0))]
            x1, h2, logits = _proj_ln(_attn_out_kernel, lead_args, lead_specs, x_all, mod,
                                      attn_w_o[j].astype(BF16), ln1_g[i], ln1_b[i], w_router, b_router,
                                      ctx_row, alpha, "attn_out_ln")
        else:
            n_heads = ssm_a_log.shape[-1]
            d_inner = n_heads * SSM_HEAD_DIM
            d_conv = d_inner + 2 * SSM_GROUPS * SSM_STATE
            w_in = ssm_w_in[j]
            n_dt = w_in.shape[1] - d_inner - d_conv
            w_in_p = jnp.pad(w_in, ((0, 0), (0, LANES - n_dt))).astype(BF16)
            dt_bias_p = jnp.pad(ssm_dt_bias[j].reshape(1, n_dt), ((0, 0), (0, LANES - n_dt)))
            z, xbc, dt, dtt = _ssm_in_proj(x_all, mod, w_in_p, dt_bias_p, ctx_row, d_inner, d_conv)
            conv_w_p = jnp.pad(ssm_conv_w[j], ((0, SUBLANES - SSM_CONV), (0, 0)))
            xbc = _ssm_conv(xbc, conv_w_p, ssm_conv_b[j])
            d_skip_x = jnp.repeat(ssm_d[j].astype(F32), SSM_HEAD_DIM, axis=1)
            y_f = _ssd_scan(xbc, dt, dtt, ssm_a_log[j], d_skip_x, 0, C)
            y_b = _ssd_scan(xbc, dt, dtt, ssm_a_log[j], d_skip_x, 1, C)
            row = lambda b, t: (b, t, 0)
            lead_args = [y_f, y_b, z, ssm_norm_g[j].reshape(1, d_inner)]
            lead_specs = [pl.BlockSpec((1, TM, d_inner), row), pl.BlockSpec((1, TM, d_inner), row),
                          pl.BlockSpec((1, TM, d_inner), row), pl.BlockSpec((1, d_inner), lambda b, t: (0, 0))]
            x1, h2, logits = _proj_ln(_ssm_out_kernel, lead_args, lead_specs, x_all, mod,
                                      ssm_w_out[j].astype(BF16), ln1_g[i], ln1_b[i], w_router, b_router,
                                      ctx_row, alpha, "ssm_out_ln")
        x_all = _moe_layer(x1, h2, logits, mod, moe_w_gate[i], moe_w_up[i], moe_w_down[i],
                           ln2_g[i], ln2_b[i], ctx_row, alpha, 1 if last else 0)
    return x_all
```

```python
import functools
import math

import jax
import jax.numpy as jnp
from jax import lax
from jax.experimental import pallas as pl
from jax.experimental.pallas import tpu as pltpu

F32 = jnp.float32
BF16 = jnp.bfloat16
HIGHEST = lax.Precision.HIGHEST

GRID_W = 64
DA_HEADS = 8
DA_HEAD_DIM = 64
DA_V_DIM = 2 * DA_HEAD_DIM
ROPE_THETA = 10000.0
SSM_HEAD_DIM = 64
SSM_GROUPS = 4
SSM_STATE = 128
SSM_CONV = 5
SSM_CHUNK = 128
MOE_GROUPS = 4
MOE_PER_GROUP = 8
MOE_TOP_K = 2
LN_EPS = 1e-5
RMS_EPS = 1e-5
N_MIXERS = 2

LANES = 128
SUBLANES = 8
TM = 256
MOE_BM = 256
ATTN_TK = 768
VT_ROWS = DA_V_DIM + 16
VMEM_LIMIT = 56 * 1024 * 1024


def _cparams(*sem):
    return pltpu.CompilerParams(dimension_semantics=sem, vmem_limit_bytes=VMEM_LIMIT)


def _silu(v):
    return v / (1.0 + jnp.exp(-v))


def _layer_norm(r, g, b):
    mu = jnp.mean(r, axis=-1, keepdims=True)
    d = r - mu
    var = jnp.mean(d * d, axis=-1, keepdims=True)
    return d * lax.rsqrt(var + LN_EPS) * g + b


def _mod_row(ctx_row):
    return lambda b, t: (jnp.where(t == 0, ctx_row, b), 0, 0)


def _mod_kernel(c_ref, w_ref, b_ref, o_ref):
    s = _silu(c_ref[...])
    o_ref[0] = jnp.dot(s, w_ref[0], precision=HIGHEST, preferred_element_type=F32) + b_ref[0]


def _modulation(cond, w_mod, b_mod):
    L, D, D6 = w_mod.shape
    R = cond.shape[0]
    tn = 1536
    return pl.pallas_call(
        _mod_kernel,
        grid=(L, D6 // tn),
        in_specs=[pl.BlockSpec((R, D), lambda l, j: (0, 0)),
                  pl.BlockSpec((1, D, tn), lambda l, j: (l, 0, j)),
                  pl.BlockSpec((1, 1, tn), lambda l, j: (l, 0, j))],
        out_specs=pl.BlockSpec((1, R, tn), lambda l, j: (l, 0, j)),
        out_shape=jax.ShapeDtypeStruct((L, R, D6), F32),
        compiler_params=_cparams("parallel", "parallel"),
        name="modulation",
    )(cond, w_mod, b_mod.reshape(L, 1, D6))


def _qkv_kernel(x_ref, mod_ref, w_ref, cos_ref, sa_ref, sb_ref, qt_ref, k_ref, vt_ref, *, d_model, q_scale):
    D = d_model
    H = k_ref.shape[0]
    x = x_ref[0]
    h = (x * (1.0 + mod_ref[0, :, D:2 * D]) + mod_ref[0, :, 0:D]).astype(BF16)
    cos = cos_ref[...]
    sa = sa_ref[...]
    sb = sb_ref[...]
    first_map = lax.broadcasted_iota(jnp.int32, (LANES, TM), 0) < DA_HEAD_DIM
    ones_rows = (lax.broadcasted_iota(jnp.int32, (VT_ROWS - DA_V_DIM, TM), 0) == 0).astype(BF16)

    def rope(a):
        return a * cos + pltpu.roll(a, LANES - 16, 1) * sa + pltpu.roll(a, 16, 1) * sb

    for jj in range(3 * H // 2):
        acc2 = jnp.dot(h, w_ref[:, jj * 2 * LANES:(jj + 1) * 2 * LANES], preferred_element_type=F32)
        for half in range(2):
            j = 2 * jj + half
            acc = acc2[:, half * LANES:(half + 1) * LANES]
            hd = j % H
            if j < H:
                qt = (rope(acc) * q_scale).T
                qt_ref[hd, 0, 0] = jnp.where(first_map, qt, 0.0).astype(BF16)
                qt_ref[hd, 0, 1] = jnp.where(first_map, 0.0, qt).astype(BF16)
            elif j < 2 * H:
                k_ref[hd, 0] = rope(acc).astype(BF16)
            else:
                vt_ref[hd, 0, 0:DA_V_DIM, :] = acc.T.astype(BF16)
                vt_ref[hd, 0, DA_V_DIM:VT_ROWS, :] = ones_rows


def _qkv_proj(x_all, mod, w_qkv, cos, sa, sb, ctx_row):
    B, N, D = x_all.shape
    H = DA_HEADS
    kern = functools.partial(_qkv_kernel, d_model=D, q_scale=DA_HEAD_DIM ** -0.5 * math.log2(math.e))
    return pl.pallas_call(
        kern,
        grid=(B, N // TM),
        in_specs=[pl.BlockSpec((1, TM, D), lambda b, t: (b, t, 0)),
                  pl.BlockSpec((1, 1, 6 * D), _mod_row(ctx_row)),
                  pl.BlockSpec((D, w_qkv.shape[1]), lambda b, t: (0, 0)),
                  pl.BlockSpec((TM, LANES), lambda b, t: (t, 0)),
                  pl.BlockSpec((TM, LANES), lambda b, t: (t, 0)),
                  pl.BlockSpec((TM, LANES), lambda b, t: (t, 0))],
        out_specs=[pl.BlockSpec((H, 1, 2, LANES, TM), lambda b, t: (0, b, 0, 0, t)),
                   pl.BlockSpec((H, 1, TM, LANES), lambda b, t: (0, b, t, 0)),
                   pl.BlockSpec((H, 1, VT_ROWS, TM), lambda b, t: (0, b, 0, t))],
        out_shape=[jax.ShapeDtypeStruct((H, B, 2, LANES, N), BF16),
                   jax.ShapeDtypeStruct((H, B, N, LANES), BF16),
                   jax.ShapeDtypeStruct((H, B, VT_ROWS, N), BF16)],
        compiler_params=_cparams("parallel", "parallel"),
        name="qkv_proj",
    )(x_all, mod, w_qkv, cos, sa, sb)


def _rope_tables(n_ctx, n_lat):
    t = jnp.arange(n_lat)
    pos = jnp.stack([t // GRID_W, t % GRID_W], axis=1).astype(F32)
    axis_dims = DA_HEAD_DIM // 2
    inv = ROPE_THETA ** (-jnp.arange(0, axis_dims, 2, dtype=F32) / axis_dims)
    lane = jnp.arange(LANES)
    d = lane % DA_HEAD_DIM
    axis = d // axis_dims
    second = (d % axis_dims) // (axis_dims // 2)
    ang = pos[:, axis] * inv[d % (axis_dims // 2)][None, :]
    cos = jnp.cos(ang)
    sin = jnp.sin(ang)
    sa = jnp.where(second[None, :] == 0, -sin, 0.0)
    sb = jnp.where(second[None, :] == 1, sin, 0.0)
    pad = lambda a, v: jnp.concatenate([jnp.full((n_ctx, LANES), v, F32), a], axis=0)
    return pad(cos, 1.0), pad(sa, 0.0), pad(sb, 0.0)


def _attn_tile(qt_ref, k_ref, vt_ref, bufs, n_chunks, tk):
    tq = qt_ref.shape[-1]

    def scores(off, slot):
        k = k_ref[0, 0, pl.ds(off, tk), :]
        for j in range(2):
            bufs[slot][j, 0:tk, :] = jnp.dot(k, qt_ref[0, 0, j], preferred_element_type=F32)

    def softmax_pv(off, slot, carry):
        vt = vt_ref[0, 0, :, pl.ds(off, tk)]
        new = []
        for j in range(2):
            m, acc = carry[j]
            s = bufs[slot][j, 0:tk, :]
            m_new = jnp.maximum(m, jnp.max(s, axis=0, keepdims=True))
            p = jnp.exp2(s - m_new).astype(BF16)
            acc = jnp.exp2(m - m_new) * acc + jnp.dot(vt, p, preferred_element_type=F32)
            new.append((m_new, acc))
        return tuple(new)

    carry = tuple((jnp.full((1, tq), -jnp.inf, F32), jnp.zeros((VT_ROWS, tq), F32)) for _ in range(2))
    scores(0, 0)
    n_pairs = (n_chunks - 1) // 2
    if n_pairs > 0:
        def body(i, carry):
            base = pl.multiple_of(2 * i * tk, tk)
            scores(base + tk, 1)
            carry = softmax_pv(base, 0, carry)
            scores(base + 2 * tk, 0)
            return softmax_pv(base + tk, 1, carry)
        carry = lax.fori_loop(0, n_pairs, body, carry)
    done = 2 * n_pairs
    if n_chunks - done == 2:
        scores((done + 1) * tk, 1)
        carry = softmax_pv(done * tk, 0, carry)
        carry = softmax_pv((done + 1) * tk, 1, carry)
    else:
        carry = softmax_pv(done * tk, 0, carry)
    return tuple(acc[0:DA_V_DIM] / acc[DA_V_DIM:DA_V_DIM + 1] for _, acc in carry)


def _attn_kernel(lam_ref, g_ref, qt_ref, k_ref, vt_ref, o_ref, s0_ref, s1_ref, *, n_ctx, n_all, tk, lambda_init):
    t = pl.program_id(2)
    lp = lam_ref[...]
    lam = (jnp.exp(jnp.sum(lp[0:1] * lp[1:2], axis=-1, keepdims=True))
           - jnp.exp(jnp.sum(lp[2:3] * lp[3:4], axis=-1, keepdims=True)) + lambda_init)

    def finish(o0, o1):
        o = o0 - lam * o1
        ms = jnp.mean(o * o, axis=0, keepdims=True)
        o = o * lax.rsqrt(ms + RMS_EPS) * (g_ref[...] * (1.0 - lambda_init))
        o_ref[0] = o.T.astype(o_ref.dtype)

    @pl.when(t == 0)
    def _():
        finish(*_attn_tile(qt_ref, k_ref, vt_ref, (s0_ref, s1_ref), 1, n_ctx))

    @pl.when(t > 0)
    def _():
        finish(*_attn_tile(qt_ref, k_ref, vt_ref, (s0_ref, s1_ref), n_all // tk, tk))


def _diff_attention(qt, k, vt, lam_params, subln_g, lambda_init, n_ctx):
    H, B, N, _ = k.shape
    tk = ATTN_TK
    assert n_ctx == TM and N % tk == 0 and n_ctx <= tk
    kern = functools.partial(_attn_kernel, n_ctx=n_ctx, n_all=N, tk=tk, lambda_init=lambda_init)
    return pl.pallas_call(
        kern,
        grid=(B, H, N // TM),
        in_specs=[pl.BlockSpec(lam_params.shape, lambda b, h, t: (0, 0)),
                  pl.BlockSpec((DA_V_DIM, 1), lambda b, h, t: (0, 0)),
                  pl.BlockSpec((1, 1, 2, LANES, TM), lambda b, h, t: (h, b, 0, 0, t)),
                  pl.BlockSpec((1, 1, N, LANES), lambda b, h, t: (h, b, 0, 0)),
                  pl.BlockSpec((1, 1, VT_ROWS, N), lambda b, h, t: (h, b, 0, 0))],
        out_specs=pl.BlockSpec((1, TM, DA_V_DIM), lambda b, h, t: (b, t, h)),
        out_shape=jax.ShapeDtypeStruct((B, N, H * DA_V_DIM), BF16),
        scratch_shapes=[pltpu.VMEM((2, tk, TM), F32), pltpu.VMEM((2, tk, TM), F32)],
        compiler_params=_cparams("parallel", "parallel", "parallel"),
        name="diff_attention",
    )(lam_params, subln_g.reshape(DA_V_DIM, 1), qt, k, vt)


def _proj_ln_tail(a, x_ref, mod_ref, w_ref, lng_ref, lnb_ref, wr_ref, br_ref, x_out, h_out, lg_out, *, d_model, alpha):
    D = d_model
    o = jnp.dot(a, w_ref[...], preferred_element_type=F32)
    r = alpha * x_ref[0] + mod_ref[0, :, 2 * D:3 * D] * o
    xn = _layer_norm(r, lng_ref[...], lnb_ref[...])
    x_out[0] = xn
    h2 = xn * (1.0 + mod_ref[0, :, 4 * D:5 * D]) + mod_ref[0, :, 3 * D:4 * D]
    h_out[0] = h2.astype(h_out.dtype)
    lg_out[0] = jnp.dot(h2, wr_ref[...], precision=HIGHEST, preferred_element_type=F32) + br_ref[...]


def _attn_out_kernel(a_ref, *refs, **kw):
    _proj_ln_tail(a_ref[0], *refs, **kw)


def _ssm_out_kernel(yf_ref, yb_ref, z_ref, ng_ref, *refs, **kw):
    y = yf_ref[0] + yb_ref[0]
    gz = y * _silu(z_ref[0].astype(F32))
    ms = jnp.mean(gz * gz, axis=-1, keepdims=True)
    a = (gz * lax.rsqrt(ms + RMS_EPS) * ng_ref[...]).astype(BF16)
    _proj_ln_tail(a, *refs, **kw)


def _proj_ln(kernel_fn, lead_args, lead_specs, x_all, mod, w, ln_g, ln_b, w_router, b_router, ctx_row, alpha, name):
    B, N, D = x_all.shape
    row = lambda b, t: (b, t, 0)
    const2 = lambda b, t: (0, 0)
    kern = functools.partial(kernel_fn, d_model=D, alpha=alpha)
    return pl.pallas_call(
        kern,
        grid=(B, N // TM),
        in_specs=lead_specs + [
            pl.BlockSpec((1, TM, D), row),
            pl.BlockSpec((1, 1, 6 * D), _mod_row(ctx_row)),
            pl.BlockSpec(w.shape, const2),
            pl.BlockSpec((1, D), const2),
            pl.BlockSpec((1, D), const2),
            pl.BlockSpec(w_router.shape, const2),
            pl.BlockSpec((1, LANES), const2)],
        out_specs=[pl.BlockSpec((1, TM, D), row), pl.BlockSpec((1, TM, D), row), pl.BlockSpec((1, TM, LANES), row)],
        out_shape=[jax.ShapeDtypeStruct((B, N, D), F32), jax.ShapeDtypeStruct((B, N, D), BF16),
                   jax.ShapeDtypeStruct((B, N, LANES), F32)],
        compiler_params=_cparams("parallel", "parallel"),
        name=name,
    )(*lead_args, x_all, mod, w, ln_g.reshape(1, D), ln_b.reshape(1, D), w_router, b_router)


def _moe_kernel(be_ref, nu_ref, x_ref, wgu_ref, wd_ref, o_ref, *, d_ff):
    i = pl.program_id(0)

    @pl.when(i < nu_ref[0])
    def _():
        gu = jnp.dot(x_ref[...], wgu_ref[0], preferred_element_type=F32)
        mid = (_silu(gu[:, :d_ff]) * gu[:, d_ff:]).astype(BF16)
        o_ref[...] = jnp.dot(mid, wd_ref[0], preferred_element_type=F32)

    @pl.when(i >= nu_ref[0])
    def _():
        o_ref[...] = jnp.zeros_like(o_ref)


def _moe_experts(buf, blk_expert, n_used, wgu, wd):
    n_rows, D = buf.shape
    d_ff = wd.shape[1]
    grid_spec = pltpu.PrefetchScalarGridSpec(
        num_scalar_prefetch=2,
        grid=(n_rows // MOE_BM,),
        in_specs=[pl.BlockSpec((MOE_BM, D), lambda i, be, nu: (i, 0)),
                  pl.BlockSpec((1, D, 2 * d_ff), lambda i, be, nu: (be[i], 0, 0)),
                  pl.BlockSpec((1, d_ff, D), lambda i, be, nu: (be[i], 0, 0))],
        out_specs=pl.BlockSpec((MOE_BM, D), lambda i, be, nu: (i, 0)))
    return pl.pallas_call(
        functools.partial(_moe_kernel, d_ff=d_ff),
        grid_spec=grid_spec,
        out_shape=jax.ShapeDtypeStruct((n_rows, D), F32),
        compiler_params=_cparams("arbitrary"),
        name="moe_experts",
    )(blk_expert, n_used, buf, wgu, wd)


def _route(logits, n_experts):
    T = logits.shape[0]
    g_prob = jax.nn.softmax(logits[:, :MOE_GROUPS], axis=-1)
    g_top, g_idx = lax.top_k(g_prob, 1)
    e_logits = logits[:, MOE_GROUPS:MOE_GROUPS + n_experts].reshape(T, MOE_GROUPS, MOE_PER_GROUP)
    e_in = jnp.take_along_axis(e_logits, g_idx[:, :, None], axis=1)[:, 0]
    e_top, e_idx = lax.top_k(e_in, MOE_TOP_K)
    gate = g_top * jax.nn.softmax(e_top, axis=-1)
    expert = g_idx * MOE_PER_GROUP + e_idx
    return expert.astype(jnp.int32), gate


def _dispatch(expert, n_experts):
    T, K = expert.shape
    A = T * K
    flat_e = expert.reshape(A)
    order = jnp.argsort(flat_e)
    se = flat_e[order]
    starts = jnp.searchsorted(se, jnp.arange(n_experts, dtype=jnp.int32), side='left').astype(jnp.int32)
    ends = jnp.concatenate([starts[1:], jnp.array([A], jnp.int32)])
    sizes = ends - starts
    padded = (sizes + MOE_BM - 1) // MOE_BM * MOE_BM
    pad_end = jnp.cumsum(padded)
    pad_start = pad_end - padded
    n_blocks = -(-A // MOE_BM) + n_experts
    blk_start = jnp.arange(n_blocks, dtype=jnp.int32) * MOE_BM
    blk_expert = jnp.minimum(jnp.searchsorted(pad_end, blk_start, side='right'), n_experts - 1).astype(jnp.int32)
    n_used = (pad_end[-1] // MOE_BM).astype(jnp.int32).reshape(1)
    slot = jnp.arange(n_blocks * MOE_BM, dtype=jnp.int32)
    s_e = jnp.repeat(blk_expert, MOE_BM)
    within = slot - pad_start[s_e]
    src = jnp.clip(starts[s_e] + within, 0, A - 1)
    slot_tok = jnp.where(within < sizes[s_e], order[src] // K, 0).astype(jnp.int32)
    dest_sorted = pad_start[se] + jnp.arange(A, dtype=jnp.int32) - starts[se]
    dest = jnp.zeros((A,), jnp.int32).at[order].set(dest_sorted.astype(jnp.int32), unique_indices=True)
    return slot_tok, blk_expert, n_used, dest.reshape(T, K)


def _combine_kernel(x_ref, y0_ref, y1_ref, gt_ref, mod_ref, lng_ref, lnb_ref, o_ref, *, d_model, alpha):
    D = d_model
    gt = gt_ref[0]
    f = gt[:, 0:1] * y0_ref[0, 0] + gt[:, 1:2] * y1_ref[0, 0]
    r = alpha * x_ref[0] + mod_ref[0, :, 5 * D:6 * D] * f
    o_ref[0] = _layer_norm(r, lng_ref[...], lnb_ref[...])


def _combine_ln(x_all, ys, gates, mod, ln_g, ln_b, ctx_row, alpha, t0):
    B, N, D = x_all.shape
    nt = N // TM - t0
    tiles_per_batch = N // TM
    row = lambda b, t: (b, t + t0, 0)
    const2 = lambda b, t: (0, 0)
    mrow = _mod_row(ctx_row)
    return pl.pallas_call(
        functools.partial(_combine_kernel, d_model=D, alpha=alpha),
        grid=(B, nt),
        in_specs=[pl.BlockSpec((1, TM, D), row),
                  pl.BlockSpec((1, 1, TM, D), lambda b, t: (0, b * tiles_per_batch + t + t0, 0, 0)),
                  pl.BlockSpec((1, 1, TM, D), lambda b, t: (1, b * tiles_per_batch + t + t0, 0, 0)),
                  pl.BlockSpec((1, TM, MOE_TOP_K), row),
                  pl.BlockSpec((1, 1, 6 * D), lambda b, t: mrow(b, t + t0)),
                  pl.BlockSpec((1, D), const2),
                  pl.BlockSpec((1, D), const2)],
        out_specs=pl.BlockSpec((1, TM, D), lambda b, t: (b, t, 0)),
        out_shape=jax.ShapeDtypeStruct((B, nt * TM, D), F32),
        compiler_params=_cparams("parallel", "parallel"),
        name="moe_combine_ln",
    )(x_all, ys, ys, gates, mod, ln_g.reshape(1, D), ln_b.reshape(1, D))


def _moe_layer(x1, h2, logits, mod, w_gate, w_up, w_down, ln_g, ln_b, ctx_row, alpha, t0):
    B, N, D = x1.shape
    E = w_gate.shape[0]
    T = B * N
    expert, gate = _route(logits.reshape(T, LANES), E)
    slot_tok, blk_expert, n_used, dest = _dispatch(expert, E)
    buf = jnp.take(h2.reshape(T, D), slot_tok, axis=0, mode='clip')
    wgu = jnp.concatenate([w_gate, w_up], axis=-1).astype(BF16)
    yb = _moe_experts(buf, blk_expert, n_used, wgu, w_down.astype(BF16))
    ys = jnp.take(yb, dest.T, axis=0, mode='clip').reshape(MOE_TOP_K, T // TM, TM, D)
    return _combine_ln(x1, ys, gate.reshape(B, N, MOE_TOP_K), mod, ln_g, ln_b, ctx_row, alpha, t0)


def _softplus(v):
    return jnp.maximum(v, 0.0) + jnp.log1p(jnp.exp(-jnp.abs(v)))


def _ssm_in_kernel(x_ref, mod_ref, w_ref, dtb_ref, z_out, xbc_out, dt_out, dtt_out, *, d_model, d_inner, d_conv):
    D = d_model
    h = (x_ref[0] * (1.0 + mod_ref[0, :, D:2 * D]) + mod_ref[0, :, 0:D]).astype(BF16)
    z_out[0] = jnp.dot(h, w_ref[:, 0:d_inner], preferred_element_type=F32).astype(z_out.dtype)
    cw = 512
    for j in range(d_conv // cw):
        xbc_out[0, :, j * cw:(j + 1) * cw] = jnp.dot(
            h, w_ref[:, d_inner + j * cw:d_inner + (j + 1) * cw], preferred_element_type=F32)
    dt = jnp.dot(h, w_ref[:, d_inner + d_conv:], preferred_element_type=F32) + dtb_ref[...]
    dt = _softplus(dt)
    dt_out[0] = dt
    dtt_out[0] = dt.T


def _ssm_in_proj(x_all, mod, w_in_p, dt_bias_p, ctx_row, d_inner, d_conv):
    B, N, D = x_all.shape
    row = lambda b, t: (b, t, 0)
    kern = functools.partial(_ssm_in_kernel, d_model=D, d_inner=d_inner, d_conv=d_conv)
    return pl.pallas_call(
        kern,
        grid=(B, N // TM),
        in_specs=[pl.BlockSpec((1, TM, D), row),
                  pl.BlockSpec((1, 1, 6 * D), _mod_row(ctx_row)),
                  pl.BlockSpec(w_in_p.shape, lambda b, t: (0, 0)),
                  pl.BlockSpec((1, LANES), lambda b, t: (0, 0))],
        out_specs=[pl.BlockSpec((1, TM, d_inner), row), pl.BlockSpec((1, TM, d_conv), row),
                   pl.BlockSpec((1, TM, LANES), row), pl.BlockSpec((1, LANES, TM), lambda b, t: (b, 0, t))],
        out_shape=[jax.ShapeDtypeStruct((B, N, d_inner), BF16), jax.ShapeDtypeStruct((B, N, d_conv), F32),
                   jax.ShapeDtypeStruct((B, N, LANES), F32), jax.ShapeDtypeStruct((B, LANES, N), F32)],
        compiler_params=_cparams("parallel", "parallel"),
        name="ssm_in_proj",
    )(x_all, mod, w_in_p, dt_bias_p)


def _conv_kernel(x_ref, prev_ref, next_ref, w_ref, b_ref, o_ref, ext_ref, *, n_tiles):
    t = pl.program_id(1)
    halo = SUBLANES
    pad = SSM_CONV // 2
    has_prev = t >= 2
    has_next = jnp.logical_and(t >= 1, t < n_tiles - 1)
    ext_ref[0:halo, :] = jnp.where(has_prev, prev_ref[0], 0.0)
    ext_ref[halo:halo + TM, :] = x_ref[0]
    ext_ref[halo + TM:, :] = jnp.where(has_next, next_ref[0], 0.0)
    cw = 512
    for j in range(o_ref.shape[2] // cw):
        cols = slice(j * cw, (j + 1) * cw)
        acc = b_ref[:, cols] + w_ref[0:1, cols] * ext_ref[halo - pad:halo - pad + TM, cols]
        for k in range(1, SSM_CONV):
            acc = acc + w_ref[k:k + 1, cols] * ext_ref[halo - pad + k:halo - pad + k + TM, cols]
        o_ref[0, :, cols] = _silu(acc)


def _ssm_conv(xbc, conv_w_p, conv_b):
    B, N, C = xbc.shape
    n_tiles = N // TM
    hb = TM // SUBLANES
    last_hb = N // SUBLANES - 1
    return pl.pallas_call(
        functools.partial(_conv_kernel, n_tiles=n_tiles),
        grid=(B, n_tiles),
        in_specs=[pl.BlockSpec((1, TM, C), lambda b, t: (b, t, 0)),
                  pl.BlockSpec((1, SUBLANES, C), lambda b, t: (b, jnp.maximum(t * hb - 1, 0), 0)),
                  pl.BlockSpec((1, SUBLANES, C), lambda b, t: (b, jnp.minimum((t + 1) * hb, last_hb), 0)),
                  pl.BlockSpec(conv_w_p.shape, lambda b, t: (0, 0)),
                  pl.BlockSpec((1, C), lambda b, t: (0, 0))],
        out_specs=pl.BlockSpec((1, TM, C), lambda b, t: (b, t, 0)),
        out_shape=jax.ShapeDtypeStruct((B, N, C), F32),
        scratch_shapes=[pltpu.VMEM((TM + 2 * SUBLANES, C), F32)],
        compiler_params=_cparams("parallel", "parallel"),
        name="ssm_conv",
    )(xbc, xbc, xbc, conv_w_p, conv_b.reshape(1, C))


def _ssd_kernel(x_ref, b_ref, c_ref, dt_ref, dtt_ref, alr_ref, alc_ref, dsk_ref, y_ref, state_ref, xw_ref, *,
                direction, n_heads):
    L = SSM_CHUNK
    P = SSM_HEAD_DIM
    NS = SSM_STATE
    hpg = n_heads // SSM_GROUPS
    gw = hpg * P
    reverse = direction == 1

    @pl.when(pl.program_id(1) == 0)
    def _():
        state_ref[...] = jnp.zeros_like(state_ref)

    x = x_ref[0]
    dt = dt_ref[0]
    ti = lax.broadcasted_iota(jnp.int32, (L, L), 0)
    si = lax.broadcasted_iota(jnp.int32, (L, L), 1)
    mask = (si >= ti) if reverse else (si <= ti)
    mask_t = (ti >= si) if reverse else (ti <= si)
    a = jnp.dot(mask.astype(F32), dt * -jnp.exp(alr_ref[...]), precision=HIGHEST, preferred_element_type=F32)
    at = jnp.dot(dtt_ref[0] * -jnp.exp(alc_ref[...]), mask_t.astype(F32), precision=HIGHEST,
                 preferred_element_type=F32)
    tot = a[0:1, :] if reverse else a[L - 1:L, :]
    ea = jnp.exp(a)
    wgt = jnp.exp(tot - a)
    etot = jnp.exp(tot)
    dx = x * dsk_ref[direction:direction + 1, :]

    for g in range(SSM_GROUPS):
        bg = b_ref[0][:, g * NS:(g + 1) * NS].astype(BF16)
        cg = c_ref[0][:, g * NS:(g + 1) * NS].astype(BF16)
        cb = lax.dot_general(cg, bg, (((1,), (1,)), ((), ())), preferred_element_type=F32)
        yoff = jnp.dot(cg, state_ref[:, g * gw:(g + 1) * gw].astype(BF16), preferred_element_type=F32)
        for hh in range(hpg):
            h = g * hpg + hh
            ln = direction * n_heads + h
            cols = slice(h * P, (h + 1) * P)
            decay = jnp.where(mask, jnp.exp(a[:, ln:ln + 1] - at[ln:ln + 1, :]), 0.0)
            m = (cb * decay).astype(BF16)
            xdt = x[:, cols] * dt[:, ln:ln + 1]
            y_ref[0, :, cols] = (jnp.dot(m, xdt.astype(BF16), preferred_element_type=F32)
                                 + yoff[:, hh * P:(hh + 1) * P] * ea[:, ln:ln + 1] + dx[:, cols])
            xw_ref[:, hh * P:(hh + 1) * P] = xdt * wgt[:, ln:ln + 1]
        upd = lax.dot_general(bg, xw_ref[...].astype(BF16), (((0,), (0,)), ((), ())),
                              preferred_element_type=F32)
        for hh in range(hpg):
            h = g * hpg + hh
            ln = direction * n_heads + h
            cols = slice(h * P, (h + 1) * P)
            state_ref[:, cols] = state_ref[:, cols] * etot[:, ln:ln + 1] + upd[:, hh * P:(hh + 1) * P]


def _ssd_scan(xbc, dt, dtt, a_log, d_skip_x, direction, n_ctx):
    B, N, _ = xbc.shape
    n_heads = a_log.shape[1]
    d_inner = n_heads * SSM_HEAD_DIM
    gs = SSM_GROUPS * SSM_STATE
    L = SSM_CHUNK
    nc = N // L
    ncc = n_ctx // L
    if direction == 0:
        chunk = lambda c: c
    else:
        chunk = lambda c: jnp.where(c < ncc, ncc - 1 - c, nc - 1 - (c - ncc))
    kern = functools.partial(_ssd_kernel, direction=direction, n_heads=n_heads)
    a_log_lanes = jnp.pad(a_log.reshape(-1), (0, LANES - a_log.size))
    return pl.pallas_call(
        kern,
        grid=(B, nc),
        in_specs=[pl.BlockSpec((1, L, d_inner), lambda b, c: (b, chunk(c), 0)),
                  pl.BlockSpec((1, L, gs), lambda b, c: (b, chunk(c), d_inner // gs)),
                  pl.BlockSpec((1, L, gs), lambda b, c: (b, chunk(c), d_inner // gs + 1)),
                  pl.BlockSpec((1, L, LANES), lambda b, c: (b, chunk(c), 0)),
                  pl.BlockSpec((1, LANES, L), lambda b, c: (b, 0, chunk(c))),
                  pl.BlockSpec((1, LANES), lambda b, c: (0, 0)),
                  pl.BlockSpec((LANES, 1), lambda b, c: (0, 0)),
                  pl.BlockSpec(d_skip_x.shape, lambda b, c: (0, 0))],
        out_specs=pl.BlockSpec((1, L, d_inner), lambda b, c: (b, chunk(c), 0)),
        out_shape=jax.ShapeDtypeStruct((B, N, d_inner), F32),
        scratch_shapes=[pltpu.VMEM((SSM_STATE, d_inner), F32),
                        pltpu.VMEM((L, d_inner // SSM_GROUPS), F32)],
        compiler_params=_cparams("parallel", "arbitrary"),
        name=f"ssd_scan_dir{direction}",
    )(xbc, xbc, xbc, dt, dtt, a_log_lanes.reshape(1, LANES), a_log_lanes.reshape(LANES, 1), d_skip_x)


def kernel(x, c, ctx, c_ctx, w_mod, b_mod, ln1_g, ln1_b, ln2_g, ln2_b, attn_w_qkv, attn_w_o, attn_lq1, attn_lk1, attn_lq2, attn_lk2, attn_subln_g, ssm_w_in, ssm_conv_w, ssm_conv_b, ssm_dt_bias, ssm_a_log, ssm_d, ssm_norm_g, ssm_w_out, moe_w_group, moe_b_group, moe_w_expert, moe_b_expert, moe_w_gate, moe_w_up, moe_w_down):
    B, S, D = x.shape
    C = ctx.shape[1]
    depth = w_mod.shape[0]
    E = moe_w_expert.shape[-1]
    assert C == TM and S % TM == 0 and B < SUBLANES
    alpha = (2 * depth) ** 0.25
    ctx_row = B

    cond = jnp.zeros((SUBLANES, D), F32).at[:B].set(c).at[B].set(c_ctx)
    mods = _modulation(cond, w_mod, b_mod)
    x_all = jnp.concatenate([ctx, x], axis=1)
    cos, sa, sb = _rope_tables(C, S)

    for i in range(depth):
        last = i == depth - 1
        j = i // N_MIXERS
        mod = mods[i].reshape(SUBLANES, 1, 6 * D)
        w_router = jnp.zeros((D, LANES), F32).at[:, :MOE_GROUPS].set(moe_w_group[i]) \
            .at[:, MOE_GROUPS:MOE_GROUPS + E].set(moe_w_expert[i])
        b_router = jnp.zeros((1, LANES), F32).at[0, :MOE_GROUPS].set(moe_b_group[i]) \
            .at[0, MOE_GROUPS:MOE_GROUPS + E].set(moe_b_expert[i])
        if i % N_MIXERS == 0:
            lambda_init = 0.8 - 0.6 * math.exp(-0.3 * i)
            qt, k, vt = _qkv_proj(x_all, mod, attn_w_qkv[j].astype(BF16), cos, sa, sb, ctx_row)
            lam_params = jnp.stack([attn_lq1[j], attn_lk1[j], attn_lq2[j], attn_lk2[j]])
            o = _diff_attention(qt, k, vt, lam_params, attn_subln_g[j], lambda_init, C)
            lead_args = [o]
            lead_specs = [pl.BlockSpec((1, TM, o.shape[-1]), lambda b, t: (b, t, 0))]
            x1, h2, logits = _proj_ln(_attn_out_kernel, lead_args, lead_specs, x_all, mod,
                                      attn_w_o[j].astype(BF16), ln1_g[i], ln1_b[i], w_router, b_router,
                                      ctx_row, alpha, "attn_out_ln")
        else:
            n_heads = ssm_a_log.shape[-1]
            d_inner = n_heads * SSM_HEAD_DIM
            d_conv = d_inner + 2 * SSM_GROUPS * SSM_STATE
            w_in = ssm_w_in[j]
            n_dt = w_in.shape[1] - d_inner - d_conv
            w_in_p = jnp.pad(w_in, ((0, 0), (0, LANES - n_dt))).astype(BF16)
            dt_bias_p = jnp.pad(ssm_dt_bias[j].reshape(1, n_dt), ((0, 0), (0, LANES - n_dt)))
            z, xbc, dt, dtt = _ssm_in_proj(x_all, mod, w_in_p, dt_bias_p, ctx_row, d_inner, d_conv)
            conv_w_p = jnp.pad(ssm_conv_w[j], ((0, SUBLANES - SSM_CONV), (0, 0)))
            xbc = _ssm_conv(xbc, conv_w_p, ssm_conv_b[j])
            d_skip_x = jnp.repeat(ssm_d[j].astype(F32), SSM_HEAD_DIM, axis=1)
            y_f = _ssd_scan(xbc, dt, dtt, ssm_a_log[j], d_skip_x, 0, C)
            y_b = _ssd_scan(xbc, dt, dtt, ssm_a_log[j], d_skip_x, 1, C)
            row = lambda b, t: (b, t, 0)
            lead_args = [y_f, y_b, z, ssm_norm_g[j].reshape(1, d_inner)]
            lead_specs = [pl.BlockSpec((1, TM, d_inner), row), pl.BlockSpec((1, TM, d_inner), row),
                          pl.BlockSpec((1, TM, d_inner), row), pl.BlockSpec((1, d_inner), lambda b, t: (0, 0))]
            x1, h2, logits = _proj_ln(_ssm_out_kernel, lead_args, lead_specs, x_all, mod,
                                      ssm_w_out[j].astype(BF16), ln1_g[i], ln1_b[i], w_router, b_router,
                                      ctx_row, alpha, "ssm_out_ln")
        x_all = _moe_layer(x1, h2, logits, mod, moe_w_gate[i], moe_w_up[i], moe_w_down[i],
                           ln2_g[i], ln2_b[i], ctx_row, alpha, 1 if last else 0)
    return x_all
```

```python
import functools
import math

import jax
import jax.numpy as jnp
from jax import lax
from jax.experimental import pallas as pl
from jax.experimental.pallas import tpu as pltpu

F32 = jnp.float32
BF16 = jnp.bfloat16
HIGHEST = lax.Precision.HIGHEST

GRID_W = 64
DA_HEADS = 8
DA_HEAD_DIM = 64
DA_V_DIM = 2 * DA_HEAD_DIM
ROPE_THETA = 10000.0
SSM_HEAD_DIM = 64
SSM_GROUPS = 4
SSM_STATE = 128
SSM_CONV = 5
SSM_CHUNK = 128
MOE_GROUPS = 4
MOE_PER_GROUP = 8
MOE_TOP_K = 2
LN_EPS = 1e-5
RMS_EPS = 1e-5
N_MIXERS = 2

LANES = 128
SUBLANES = 8
TM = 256
MOE_BM = 256
ATTN_TK = 768
VT_ROWS = DA_V_DIM + 16
VMEM_LIMIT = 56 * 1024 * 1024


def _cparams(*sem):
    return pltpu.CompilerParams(dimension_semantics=sem, vmem_limit_bytes=VMEM_LIMIT)


def _silu(v):
    return v / (1.0 + jnp.exp(-v))


def _layer_norm(r, g, b):
    mu = jnp.mean(r, axis=-1, keepdims=True)
    d = r - mu
    var = jnp.mean(d * d, axis=-1, keepdims=True)
    return d * lax.rsqrt(var + LN_EPS) * g + b


def _mod_row(ctx_row):
    return lambda b, t: (jnp.where(t == 0, ctx_row, b), 0, 0)


def _mod_kernel(c_ref, w_ref, b_ref, o_ref):
    s = _silu(c_ref[...])
    o_ref[0] = jnp.dot(s, w_ref[0], precision=HIGHEST, preferred_element_type=F32) + b_ref[0]


def _modulation(cond, w_mod, b_mod):
    L, D, D6 = w_mod.shape
    R = cond.shape[0]
    tn = 1536
    return pl.pallas_call(
        _mod_kernel,
        grid=(L, D6 // tn),
        in_specs=[pl.BlockSpec((R, D), lambda l, j: (0, 0)),
                  pl.BlockSpec((1, D, tn), lambda l, j: (l, 0, j)),
                  pl.BlockSpec((1, 1, tn), lambda l, j: (l, 0, j))],
        out_specs=pl.BlockSpec((1, R, tn), lambda l, j: (l, 0, j)),
        out_shape=jax.ShapeDtypeStruct((L, R, D6), F32),
        compiler_params=_cparams("parallel", "parallel"),
        name="modulation",
    )(cond, w_mod, b_mod.reshape(L, 1, D6))


def _qkv_kernel(x_ref, mod_ref, w_ref, cos_ref, sa_ref, sb_ref, qt_ref, k_ref, vt_ref, *, d_model, q_scale):
    D = d_model
    H = k_ref.shape[0]
    x = x_ref[0]
    h = (x * (1.0 + mod_ref[0, :, D:2 * D]) + mod_ref[0, :, 0:D]).astype(BF16)
    cos = cos_ref[...]
    sa = sa_ref[...]
    sb = sb_ref[...]
    first_map = lax.broadcasted_iota(jnp.int32, (LANES, TM), 0) < DA_HEAD_DIM
    ones_rows = (lax.broadcasted_iota(jnp.int32, (VT_ROWS - DA_V_DIM, TM), 0) == 0).astype(BF16)

    def rope(a):
        return a * cos + pltpu.roll(a, LANES - 16, 1) * sa + pltpu.roll(a, 16, 1) * sb

    for jj in range(3 * H // 2):
        acc2 = jnp.dot(h, w_ref[:, jj * 2 * LANES:(jj + 1) * 2 * LANES], preferred_element_type=F32)
        for half in range(2):
            j = 2 * jj + half
            acc = acc2[:, half * LANES:(half + 1) * LANES]
            hd = j % H
            if j < H:
                qt = (rope(acc) * q_scale).T
                qt_ref[hd, 0, 0] = jnp.where(first_map, qt, 0.0).astype(BF16)
                qt_ref[hd, 0, 1] = jnp.where(first_map, 0.0, qt).astype(BF16)
            elif j < 2 * H:
                k_ref[hd, 0] = rope(acc).astype(BF16)
            else:
                vt_ref[hd, 0, 0:DA_V_DIM, :] = acc.T.astype(BF16)
                vt_ref[hd, 0, DA_V_DIM:VT_ROWS, :] = ones_rows


def _qkv_proj(x_all, mod, w_qkv, cos, sa, sb, ctx_row):
    B, N, D = x_all.shape
    H = DA_HEADS
    kern = functools.partial(_qkv_kernel, d_model=D, q_scale=DA_HEAD_DIM ** -0.5 * math.log2(math.e))
    return pl.pallas_call(
        kern,
        grid=(B, N // TM),
        in_specs=[pl.BlockSpec((1, TM, D), lambda b, t: (b, t, 0)),
                  pl.BlockSpec((1, 1, 6 * D), _mod_row(ctx_row)),
                  pl.BlockSpec((D, w_qkv.shape[1]), lambda b, t: (0, 0)),
                  pl.BlockSpec((TM, LANES), lambda b, t: (t, 0)),
                  pl.BlockSpec((TM, LANES), lambda b, t: (t, 0)),
                  pl.BlockSpec((TM, LANES), lambda b, t: (t, 0))],
        out_specs=[pl.BlockSpec((H, 1, 2, LANES, TM), lambda b, t: (0, b, 0, 0, t)),
                   pl.BlockSpec((H, 1, TM, LANES), lambda b, t: (0, b, t, 0)),
                   pl.BlockSpec((H, 1, VT_ROWS, TM), lambda b, t: (0, b, 0, t))],
        out_shape=[jax.ShapeDtypeStruct((H, B, 2, LANES, N), BF16),
                   jax.ShapeDtypeStruct((H, B, N, LANES), BF16),
                   jax.ShapeDtypeStruct((H, B, VT_ROWS, N), BF16)],
        compiler_params=_cparams("parallel", "parallel"),
        name="qkv_proj",
    )(x_all, mod, w_qkv, cos, sa, sb)


def _rope_tables(n_ctx, n_lat):
    t = jnp.arange(n_lat)
    pos = jnp.stack([t // GRID_W, t % GRID_W], axis=1).astype(F32)
    axis_dims = DA_HEAD_DIM // 2
    inv = ROPE_THETA ** (-jnp.arange(0, axis_dims, 2, dtype=F32) / axis_dims)
    lane = jnp.arange(LANES)
    d = lane % DA_HEAD_DIM
    axis = d // axis_dims
    second = (d % axis_dims) // (axis_dims // 2)
    ang = pos[:, axis] * inv[d % (axis_dims // 2)][None, :]
    cos = jnp.cos(ang)
    sin = jnp.sin(ang)
    sa = jnp.where(second[None, :] == 0, -sin, 0.0)
    sb = jnp.where(second[None, :] == 1, sin, 0.0)
    pad = lambda a, v: jnp.concatenate([jnp.full((n_ctx, LANES), v, F32), a], axis=0)
    return pad(cos, 1.0), pad(sa, 0.0), pad(sb, 0.0)


def _attn_tile(qt_ref, k_ref, vt_ref, bufs, n_chunks, tk):
    tq = qt_ref.shape[-1]

    def scores(off, slot):
        k = k_ref[0, 0, pl.ds(off, tk), :]
        for j in range(2):
            bufs[slot][j, 0:tk, :] = jnp.dot(k, qt_ref[0, 0, j], preferred_element_type=F32)

    def softmax_pv(off, slot, carry):
        vt = vt_ref[0, 0, :, pl.ds(off, tk)]
        new = []
        for j in range(2):
            m, acc = carry[j]
            s = bufs[slot][j, 0:tk, :]
            m_new = jnp.maximum(m, jnp.max(s, axis=0, keepdims=True))
            p = jnp.exp2(s - m_new).astype(BF16)
            acc = jnp.exp2(m - m_new) * acc + jnp.dot(vt, p, preferred_element_type=F32)
            new.append((m_new, acc))
        return tuple(new)

    carry = tuple((jnp.full((1, tq), -jnp.inf, F32), jnp.zeros((VT_ROWS, tq), F32)) for _ in range(2))
    scores(0, 0)
    n_pairs = (n_chunks - 1) // 2
    if n_pairs > 0:
        def body(i, carry):
            base = pl.multiple_of(2 * i * tk, tk)
            scores(base + tk, 1)
            carry = softmax_pv(base, 0, carry)
            scores(base + 2 * tk, 0)
            return softmax_pv(base + tk, 1, carry)
        carry = lax.fori_loop(0, n_pairs, body, carry)
    done = 2 * n_pairs
    if n_chunks - done == 2:
        scores((done + 1) * tk, 1)
        carry = softmax_pv(done * tk, 0, carry)
        carry = softmax_pv((done + 1) * tk, 1, carry)
    else:
        carry = softmax_pv(done * tk, 0, carry)
    return tuple(acc[0:DA_V_DIM] / acc[DA_V_DIM:DA_V_DIM + 1] for _, acc in carry)


def _attn_kernel(lam_ref, g_ref, qt_ref, k_ref, vt_ref, o_ref, s0_ref, s1_ref, *, n_ctx, n_all, tk, lambda_init):
    t = pl.program_id(2)
    lp = lam_ref[...]
    lam = (jnp.exp(jnp.sum(lp[0:1] * lp[1:2], axis=-1, keepdims=True))
           - jnp.exp(jnp.sum(lp[2:3] * lp[3:4], axis=-1, keepdims=True)) + lambda_init)

    def finish(o0, o1):
        o = o0 - lam * o1
        ms = jnp.mean(o * o, axis=0, keepdims=True)
        o = o * lax.rsqrt(ms + RMS_EPS) * (g_ref[...] * (1.0 - lambda_init))
        o_ref[0] = o.T.astype(o_ref.dtype)

    @pl.when(t == 0)
    def _():
        finish(*_attn_tile(qt_ref, k_ref, vt_ref, (s0_ref, s1_ref), 1, n_ctx))

    @pl.when(t > 0)
    def _():
        finish(*_attn_tile(qt_ref, k_ref, vt_ref, (s0_ref, s1_ref), n_all // tk, tk))


def _diff_attention(qt, k, vt, lam_params, subln_g, lambda_init, n_ctx):
    H, B, N, _ = k.shape
    tk = ATTN_TK
    assert n_ctx == TM and N % tk == 0 and n_ctx <= tk
    kern = functools.partial(_attn_kernel, n_ctx=n_ctx, n_all=N, tk=tk, lambda_init=lambda_init)
    return pl.pallas_call(
        kern,
        grid=(B, H, N // TM),
        in_specs=[pl.BlockSpec(lam_params.shape, lambda b, h, t: (0, 0)),
                  pl.BlockSpec((DA_V_DIM, 1), lambda b, h, t: (0, 0)),
                  pl.BlockSpec((1, 1, 2, LANES, TM), lambda b, h, t: (h, b, 0, 0, t)),
                  pl.BlockSpec((1, 1, N, LANES), lambda b, h, t: (h, b, 0, 0)),
                  pl.BlockSpec((1, 1, VT_ROWS, N), lambda b, h, t: (h, b, 0, 0))],
        out_specs=pl.BlockSpec((1, TM, DA_V_DIM), lambda b, h, t: (b, t, h)),
        out_shape=jax.ShapeDtypeStruct((B, N, H * DA_V_DIM), BF16),
        scratch_shapes=[pltpu.VMEM((2, tk, TM), F32), pltpu.VMEM((2, tk, TM), F32)],
        compiler_params=_cparams("parallel", "parallel", "parallel"),
        name="diff_attention",
    )(lam_params, subln_g.reshape(DA_V_DIM, 1), qt, k, vt)


def _proj_ln_tail(a, x_ref, mod_ref, w_ref, lng_ref, lnb_ref, wr_ref, br_ref, x_out, h_out, lg_out, *, d_model, alpha):
    D = d_model
    o = jnp.dot(a, w_ref[...], preferred_element_type=F32)
    r = alpha * x_ref[0] + mod_ref[0, :, 2 * D:3 * D] * o
    xn = _layer_norm(r, lng_ref[...], lnb_ref[...])
    x_out[0] = xn
    h2 = xn * (1.0 + mod_ref[0, :, 4 * D:5 * D]) + mod_ref[0, :, 3 * D:4 * D]
    h_out[0] = h2.astype(h_out.dtype)
    lg_out[0] = jnp.dot(h2, wr_ref[...], precision=HIGHEST, preferred_element_type=F32) + br_ref[...]


def _attn_out_kernel(a_ref, *refs, **kw):
    _proj_ln_tail(a_ref[0], *refs, **kw)


def _ssm_out_kernel(yf_ref, yb_ref, z_ref, ng_ref, *refs, **kw):
    y = (yf_ref[0].astype(F32) + yb_ref[0].astype(F32)).T
    gz = y * _silu(z_ref[0].astype(F32))
    ms = jnp.mean(gz * gz, axis=-1, keepdims=True)
    a = (gz * lax.rsqrt(ms + RMS_EPS) * ng_ref[...]).astype(BF16)
    _proj_ln_tail(a, *refs, **kw)


def _proj_ln(kernel_fn, lead_args, lead_specs, x_all, mod, w, ln_g, ln_b, w_router, b_router, ctx_row, alpha, name):
    B, N, D = x_all.shape
    row = lambda b, t: (b, t, 0)
    const2 = lambda b, t: (0, 0)
    kern = functools.partial(kernel_fn, d_model=D, alpha=alpha)
    return pl.pallas_call(
        kern,
        grid=(B, N // TM),
        in_specs=lead_specs + [
            pl.BlockSpec((1, TM, D), row),
            pl.BlockSpec((1, 1, 6 * D), _mod_row(ctx_row)),
            pl.BlockSpec(w.shape, const2),
            pl.BlockSpec((1, D), const2),
            pl.BlockSpec((1, D), const2),
            pl.BlockSpec(w_router.shape, const2),
            pl.BlockSpec((1, LANES), const2)],
        out_specs=[pl.BlockSpec((1, TM, D), row), pl.BlockSpec((1, TM, D), row), pl.BlockSpec((1, TM, LANES), row)],
        out_shape=[jax.ShapeDtypeStruct((B, N, D), F32), jax.ShapeDtypeStruct((B, N, D), BF16),
                   jax.ShapeDtypeStruct((B, N, LANES), F32)],
        compiler_params=_cparams("parallel", "parallel"),
        name=name,
    )(*lead_args, x_all, mod, w, ln_g.reshape(1, D), ln_b.reshape(1, D), w_router, b_router)


def _moe_kernel(be_ref, nu_ref, x_ref, wgu_ref, wd_ref, o_ref, *, d_ff):
    i = pl.program_id(0)

    @pl.when(i < nu_ref[0])
    def _():
        gu = jnp.dot(x_ref[...], wgu_ref[0], preferred_element_type=F32)
        mid = (_silu(gu[:, :d_ff]) * gu[:, d_ff:]).astype(BF16)
        o_ref[...] = jnp.dot(mid, wd_ref[0], preferred_element_type=F32)

    @pl.when(i >= nu_ref[0])
    def _():
        o_ref[...] = jnp.zeros_like(o_ref)


def _moe_experts(buf, blk_expert, n_used, wgu, wd):
    n_rows, D = buf.shape
    d_ff = wd.shape[1]
    grid_spec = pltpu.PrefetchScalarGridSpec(
        num_scalar_prefetch=2,
        grid=(n_rows // MOE_BM,),
        in_specs=[pl.BlockSpec((MOE_BM, D), lambda i, be, nu: (i, 0)),
                  pl.BlockSpec((1, D, 2 * d_ff), lambda i, be, nu: (be[i], 0, 0)),
                  pl.BlockSpec((1, d_ff, D), lambda i, be, nu: (be[i], 0, 0))],
        out_specs=pl.BlockSpec((MOE_BM, D), lambda i, be, nu: (i, 0)))
    return pl.pallas_call(
        functools.partial(_moe_kernel, d_ff=d_ff),
        grid_spec=grid_spec,
        out_shape=jax.ShapeDtypeStruct((n_rows, D), F32),
        compiler_params=_cparams("arbitrary"),
        name="moe_experts",
    )(blk_expert, n_used, buf, wgu, wd)


def _route(logits, n_experts):
    T = logits.shape[0]
    g_prob = jax.nn.softmax(logits[:, :MOE_GROUPS], axis=-1)
    g_top, g_idx = lax.top_k(g_prob, 1)
    e_logits = logits[:, MOE_GROUPS:MOE_GROUPS + n_experts].reshape(T, MOE_GROUPS, MOE_PER_GROUP)
    e_in = jnp.take_along_axis(e_logits, g_idx[:, :, None], axis=1)[:, 0]
    e_top, e_idx = lax.top_k(e_in, MOE_TOP_K)
    gate = g_top * jax.nn.softmax(e_top, axis=-1)
    expert = g_idx * MOE_PER_GROUP + e_idx
    return expert.astype(jnp.int32), gate


def _dispatch(expert, n_experts):
    T, K = expert.shape
    A = T * K
    flat_e = expert.reshape(A)
    order = jnp.argsort(flat_e)
    se = flat_e[order]
    starts = jnp.searchsorted(se, jnp.arange(n_experts, dtype=jnp.int32), side='left').astype(jnp.int32)
    ends = jnp.concatenate([starts[1:], jnp.array([A], jnp.int32)])
    sizes = ends - starts
    padded = (sizes + MOE_BM - 1) // MOE_BM * MOE_BM
    pad_end = jnp.cumsum(padded)
    pad_start = pad_end - padded
    n_blocks = -(-A // MOE_BM) + n_experts
    blk_start = jnp.arange(n_blocks, dtype=jnp.int32) * MOE_BM
    blk_expert = jnp.minimum(jnp.searchsorted(pad_end, blk_start, side='right'), n_experts - 1).astype(jnp.int32)
    n_used = (pad_end[-1] // MOE_BM).astype(jnp.int32).reshape(1)
    slot = jnp.arange(n_blocks * MOE_BM, dtype=jnp.int32)
    s_e = jnp.repeat(blk_expert, MOE_BM)
    within = slot - pad_start[s_e]
    src = jnp.clip(starts[s_e] + within, 0, A - 1)
    slot_tok = jnp.where(within < sizes[s_e], order[src] // K, 0).astype(jnp.int32)
    dest_sorted = pad_start[se] + jnp.arange(A, dtype=jnp.int32) - starts[se]
    dest = jnp.zeros((A,), jnp.int32).at[order].set(dest_sorted.astype(jnp.int32), unique_indices=True)
    return slot_tok, blk_expert, n_used, dest.reshape(T, K)


def _combine_kernel(x_ref, y0_ref, y1_ref, gt_ref, mod_ref, lng_ref, lnb_ref, o_ref, *, d_model, alpha):
    D = d_model
    gt = gt_ref[0]
    f = gt[:, 0:1] * y0_ref[0, 0] + gt[:, 1:2] * y1_ref[0, 0]
    r = alpha * x_ref[0] + mod_ref[0, :, 5 * D:6 * D] * f
    o_ref[0] = _layer_norm(r, lng_ref[...], lnb_ref[...])


def _combine_ln(x_all, ys, gates, mod, ln_g, ln_b, ctx_row, alpha, t0):
    B, N, D = x_all.shape
    nt = N // TM - t0
    tiles_per_batch = N // TM
    row = lambda b, t: (b, t + t0, 0)
    const2 = lambda b, t: (0, 0)
    mrow = _mod_row(ctx_row)
    return pl.pallas_call(
        functools.partial(_combine_kernel, d_model=D, alpha=alpha),
        grid=(B, nt),
        in_specs=[pl.BlockSpec((1, TM, D), row),
                  pl.BlockSpec((1, 1, TM, D), lambda b, t: (0, b * tiles_per_batch + t + t0, 0, 0)),
                  pl.BlockSpec((1, 1, TM, D), lambda b, t: (1, b * tiles_per_batch + t + t0, 0, 0)),
                  pl.BlockSpec((1, TM, MOE_TOP_K), row),
                  pl.BlockSpec((1, 1, 6 * D), lambda b, t: mrow(b, t + t0)),
                  pl.BlockSpec((1, D), const2),
                  pl.BlockSpec((1, D), const2)],
        out_specs=pl.BlockSpec((1, TM, D), lambda b, t: (b, t, 0)),
        out_shape=jax.ShapeDtypeStruct((B, nt * TM, D), F32),
        compiler_params=_cparams("parallel", "parallel"),
        name="moe_combine_ln",
    )(x_all, ys, ys, gates, mod, ln_g.reshape(1, D), ln_b.reshape(1, D))


def _moe_layer(x1, h2, logits, mod, w_gate, w_up, w_down, ln_g, ln_b, ctx_row, alpha, t0):
    B, N, D = x1.shape
    E = w_gate.shape[0]
    T = B * N
    expert, gate = _route(logits.reshape(T, LANES), E)
    slot_tok, blk_expert, n_used, dest = _dispatch(expert, E)
    buf = jnp.take(h2.reshape(T, D), slot_tok, axis=0, mode='clip')
    wgu = jnp.concatenate([w_gate, w_up], axis=-1).astype(BF16)
    yb = _moe_experts(buf, blk_expert, n_used, wgu, w_down.astype(BF16))
    ys = jnp.take(yb, dest.T, axis=0, mode='clip').reshape(MOE_TOP_K, T // TM, TM, D)
    return _combine_ln(x1, ys, gate.reshape(B, N, MOE_TOP_K), mod, ln_g, ln_b, ctx_row, alpha, t0)


def _softplus(v):
    return jnp.maximum(v, 0.0) + jnp.log1p(jnp.exp(-jnp.abs(v)))


def _ssm_in_kernel(x_ref, mod_ref, w_ref, dtb_ref, z_out, xbc_out, dt_out, dtt_out, *, d_model, d_inner, d_conv):
    D = d_model
    h = (x_ref[0] * (1.0 + mod_ref[0, :, D:2 * D]) + mod_ref[0, :, 0:D]).astype(BF16)
    z_out[0] = jnp.dot(h, w_ref[:, 0:d_inner], preferred_element_type=F32).astype(z_out.dtype)
    cw = 512
    for j in range(d_conv // cw):
        xbc_out[0, :, j * cw:(j + 1) * cw] = jnp.dot(
            h, w_ref[:, d_inner + j * cw:d_inner + (j + 1) * cw], preferred_element_type=F32).astype(xbc_out.dtype)
    dt = jnp.dot(h, w_ref[:, d_inner + d_conv:], preferred_element_type=F32) + dtb_ref[...]
    dt = _softplus(dt)
    dt_out[0] = dt
    dtt_out[0] = dt.T


def _ssm_in_proj(x_all, mod, w_in_p, dt_bias_p, ctx_row, d_inner, d_conv):
    B, N, D = x_all.shape
    row = lambda b, t: (b, t, 0)
    kern = functools.partial(_ssm_in_kernel, d_model=D, d_inner=d_inner, d_conv=d_conv)
    return pl.pallas_call(
        kern,
        grid=(B, N // TM),
        in_specs=[pl.BlockSpec((1, TM, D), row),
                  pl.BlockSpec((1, 1, 6 * D), _mod_row(ctx_row)),
                  pl.BlockSpec(w_in_p.shape, lambda b, t: (0, 0)),
                  pl.BlockSpec((1, LANES), lambda b, t: (0, 0))],
        out_specs=[pl.BlockSpec((1, TM, d_inner), row), pl.BlockSpec((1, TM, d_conv), row),
                   pl.BlockSpec((1, TM, LANES), row), pl.BlockSpec((1, LANES, TM), lambda b, t: (b, 0, t))],
        out_shape=[jax.ShapeDtypeStruct((B, N, d_inner), BF16), jax.ShapeDtypeStruct((B, N, d_conv), BF16),
                   jax.ShapeDtypeStruct((B, N, LANES), F32), jax.ShapeDtypeStruct((B, LANES, N), F32)],
        compiler_params=_cparams("parallel", "parallel"),
        name="ssm_in_proj",
    )(x_all, mod, w_in_p, dt_bias_p)


CONV_HALO = 16


def _conv_kernel(x_ref, prev_ref, next_ref, w_ref, b_ref, xt_ref, bc_ref, ext_ref, *, n_tiles, d_inner):
    t = pl.program_id(1)
    halo = CONV_HALO
    pad = SSM_CONV // 2
    has_prev = t >= 2
    has_next = jnp.logical_and(t >= 1, t < n_tiles - 1)
    ext_ref[0:halo, :] = jnp.where(has_prev, prev_ref[0].astype(F32), 0.0)
    ext_ref[halo:halo + TM, :] = x_ref[0].astype(F32)
    ext_ref[halo + TM:, :] = jnp.where(has_next, next_ref[0].astype(F32), 0.0)
    cw = 512
    for j in range(ext_ref.shape[1] // cw):
        cols = slice(j * cw, (j + 1) * cw)
        acc = b_ref[:, cols] + w_ref[0:1, cols] * ext_ref[halo - pad:halo - pad + TM, cols]
        for k in range(1, SSM_CONV):
            acc = acc + w_ref[k:k + 1, cols] * ext_ref[halo - pad + k:halo - pad + k + TM, cols]
        act = _silu(acc)
        if j * cw < d_inner:
            xt_ref[0, j * cw:(j + 1) * cw, :] = act.T.astype(xt_ref.dtype)
        else:
            bc_ref[0, :, j * cw - d_inner:(j + 1) * cw - d_inner] = act.astype(bc_ref.dtype)


def _ssm_conv(xbc, conv_w_p, conv_b, d_inner):
    B, N, C = xbc.shape
    n_tiles = N // TM
    hb = TM // CONV_HALO
    last_hb = N // CONV_HALO - 1
    return pl.pallas_call(
        functools.partial(_conv_kernel, n_tiles=n_tiles, d_inner=d_inner),
        grid=(B, n_tiles),
        in_specs=[pl.BlockSpec((1, TM, C), lambda b, t: (b, t, 0)),
                  pl.BlockSpec((1, CONV_HALO, C), lambda b, t: (b, jnp.maximum(t * hb - 1, 0), 0)),
                  pl.BlockSpec((1, CONV_HALO, C), lambda b, t: (b, jnp.minimum((t + 1) * hb, last_hb), 0)),
                  pl.BlockSpec(conv_w_p.shape, lambda b, t: (0, 0)),
                  pl.BlockSpec((1, C), lambda b, t: (0, 0))],
        out_specs=[pl.BlockSpec((1, d_inner, TM), lambda b, t: (b, 0, t)),
                   pl.BlockSpec((1, TM, C - d_inner), lambda b, t: (b, t, 0))],
        out_shape=[jax.ShapeDtypeStruct((B, d_inner, N), BF16), jax.ShapeDtypeStruct((B, N, C - d_inner), BF16)],
        scratch_shapes=[pltpu.VMEM((TM + 2 * CONV_HALO, C), F32)],
        compiler_params=_cparams("parallel", "parallel"),
        name="ssm_conv",
    )(xbc, xbc, xbc, conv_w_p, conv_b.reshape(1, C))


def _ssd_kernel(xt_ref, bc_ref, dt_ref, dtt_ref, alr_ref, alc_ref, dsk_ref, y_ref, state_ref, xw_ref, *,
                direction, n_heads):
    L = SSM_CHUNK
    P = SSM_HEAD_DIM
    NS = SSM_STATE
    hpg = n_heads // SSM_GROUPS
    gw = hpg * P
    reverse = direction == 1
    l0 = direction * n_heads

    @pl.when(pl.program_id(1) == 0)
    def _():
        state_ref[...] = jnp.zeros_like(state_ref)

    ri = lax.broadcasted_iota(jnp.int32, (L, L), 0)
    ci = lax.broadcasted_iota(jnp.int32, (L, L), 1)
    before = (ri >= ci) if reverse else (ri <= ci)
    after = (ci >= ri) if reverse else (ci <= ri)
    a = jnp.dot(after.astype(F32), dt_ref[0] * -jnp.exp(alr_ref[...]), precision=HIGHEST, preferred_element_type=F32)
    at = jnp.dot(dtt_ref[0] * -jnp.exp(alc_ref[...]), before.astype(F32), precision=HIGHEST,
                 preferred_element_type=F32)[l0:l0 + n_heads]
    last = 0 if reverse else L - 1
    tot = jnp.broadcast_to(at[:, last:last + 1], (n_heads, L))
    ea = jnp.exp(at)
    wgt = jnp.exp(tot - at)
    etot = jnp.exp(tot)
    dtt = dtt_ref[0][l0:l0 + n_heads]
    dsk = dsk_ref[direction]

    for g in range(SSM_GROUPS):
        bg = bc_ref[0][:, g * NS:(g + 1) * NS]
        cg = bc_ref[0][:, (SSM_GROUPS + g) * NS:(SSM_GROUPS + g + 1) * NS]
        cbt = lax.dot_general(bg, cg, (((1,), (1,)), ((), ())), preferred_element_type=F32)
        yoff = lax.dot_general(state_ref[g * gw:(g + 1) * gw, :].astype(BF16), cg, (((1,), (1,)), ((), ())),
                               preferred_element_type=F32)
        for hh in range(hpg):
            h = g * hpg + hh
            rows = slice(h * P, (h + 1) * P)
            decay = jnp.where(before, jnp.exp(at[h:h + 1, :] - a[:, l0 + h:l0 + h + 1]), 0.0)
            mt = (cbt * decay).astype(BF16)
            xh = xt_ref[0, rows, :].astype(F32)
            xdt = xh * dtt[h:h + 1, :]
            y = (jnp.dot(xdt.astype(BF16), mt, preferred_element_type=F32)
                 + yoff[hh * P:(hh + 1) * P, :] * ea[h:h + 1, :] + xh * dsk[h:h + 1, :])
            y_ref[0, rows, :] = y.astype(y_ref.dtype)
            xw_ref[hh * P:(hh + 1) * P, :] = (xdt * wgt[h:h + 1, :]).astype(xw_ref.dtype)
        upd = jnp.dot(xw_ref[...], bg, preferred_element_type=F32)
        for hh in range(hpg):
            h = g * hpg + hh
            rows = slice(h * P, (h + 1) * P)
            state_ref[rows, :] = state_ref[rows, :] * etot[h:h + 1, :] + upd[hh * P:(hh + 1) * P, :]


def _ssd_scan(xt, bc, dt, dtt, a_log, d_skip, direction, n_ctx):
    B, d_inner, N = xt.shape
    n_heads = a_log.shape[1]
    L = SSM_CHUNK
    nc = N // L
    ncc = n_ctx // L
    if direction == 0:
        chunk = lambda c: c
    else:
        chunk = lambda c: jnp.where(c < ncc, ncc - 1 - c, nc - 1 - (c - ncc))
    kern = functools.partial(_ssd_kernel, direction=direction, n_heads=n_heads)
    a_log_lanes = jnp.pad(a_log.reshape(-1), (0, LANES - a_log.size))
    d_skip_lanes = jnp.broadcast_to(d_skip.astype(F32)[:, :, None], d_skip.shape + (L,))
    return pl.pallas_call(
        kern,
        grid=(B, nc),
        in_specs=[pl.BlockSpec((1, d_inner, L), lambda b, c: (b, 0, chunk(c))),
                  pl.BlockSpec((1, L, bc.shape[2]), lambda b, c: (b, chunk(c), 0)),
                  pl.BlockSpec((1, L, LANES), lambda b, c: (b, chunk(c), 0)),
                  pl.BlockSpec((1, LANES, L), lambda b, c: (b, 0, chunk(c))),
                  pl.BlockSpec((1, LANES), lambda b, c: (0, 0)),
                  pl.BlockSpec((LANES, 1), lambda b, c: (0, 0)),
                  pl.BlockSpec(d_skip_lanes.shape, lambda b, c: (0, 0, 0))],
        out_specs=pl.BlockSpec((1, d_inner, L), lambda b, c: (b, 0, chunk(c))),
        out_shape=jax.ShapeDtypeStruct((B, d_inner, N), BF16),
        scratch_shapes=[pltpu.VMEM((d_inner, SSM_STATE), F32),
                        pltpu.VMEM((d_inner // SSM_GROUPS, L), BF16)],
        compiler_params=_cparams("parallel", "arbitrary"),
        name=f"ssd_scan_dir{direction}",
    )(xt, bc, dt, dtt, a_log_lanes.reshape(1, LANES), a_log_lanes.reshape(LANES, 1), d_skip_lanes)


def kernel(x, c, ctx, c_ctx, w_mod, b_mod, ln1_g, ln1_b, ln2_g, ln2_b, attn_w_qkv, attn_w_o, attn_lq1, attn_lk1, attn_lq2, attn_lk2, attn_subln_g, ssm_w_in, ssm_conv_w, ssm_conv_b, ssm_dt_bias, ssm_a_log, ssm_d, ssm_norm_g, ssm_w_out, moe_w_group, moe_b_group, moe_w_expert, moe_b_expert, moe_w_gate, moe_w_up, moe_w_down):
    B, S, D = x.shape
    C = ctx.shape[1]
    depth = w_mod.shape[0]
    E = moe_w_expert.shape[-1]
    assert C == TM and S % TM == 0 and B < SUBLANES
    alpha = (2 * depth) ** 0.25
    ctx_row = B

    cond = jnp.zeros((SUBLANES, D), F32).at[:B].set(c).at[B].set(c_ctx)
    mods = _modulation(cond, w_mod, b_mod)
    x_all = jnp.concatenate([ctx, x], axis=1)
    cos, sa, sb = _rope_tables(C, S)

    for i in range(depth):
        last = i == depth - 1
        j = i // N_MIXERS
        mod = mods[i].reshape(SUBLANES, 1, 6 * D)
        w_router = jnp.zeros((D, LANES), F32).at[:, :MOE_GROUPS].set(moe_w_group[i]) \
            .at[:, MOE_GROUPS:MOE_GROUPS + E].set(moe_w_expert[i])
        b_router = jnp.zeros((1, LANES), F32).at[0, :MOE_GROUPS].set(moe_b_group[i]) \
            .at[0, MOE_GROUPS:MOE_GROUPS + E].set(moe_b_expert[i])
        if i % N_MIXERS == 0:
            lambda_init = 0.8 - 0.6 * math.exp(-0.3 * i)
            qt, k, vt = _qkv_proj(x_all, mod, attn_w_qkv[j].astype(BF16), cos, sa, sb, ctx_row)
            lam_params = jnp.stack([attn_lq1[j], attn_lk1[j], attn_lq2[j], attn_lk2[j]])
            o = _diff_attention(qt, k, vt, lam_params, attn_subln_g[j], lambda_init, C)
            lead_args = [o]
            lead_specs = [pl.BlockSpec((1, TM, o.shape[-1]), lambda b, t: (b, t, 0))]
            x1, h2, logits = _proj_ln(_attn_out_kernel, lead_args, lead_specs, x_all, mod,
                                      attn_w_o[j].astype(BF16), ln1_g[i], ln1_b[i], w_router, b_router,
                                      ctx_row, alpha, "attn_out_ln")
        else:
            n_heads = ssm_a_log.shape[-1]
            d_inner = n_heads * SSM_HEAD_DIM
            d_conv = d_inner + 2 * SSM_GROUPS * SSM_STATE
            w_in = ssm_w_in[j]
            n_dt = w_in.shape[1] - d_inner - d_conv
            w_in_p = jnp.pad(w_in, ((0, 0), (0, LANES - n_dt))).astype(BF16)
            dt_bias_p = jnp.pad(ssm_dt_bias[j].reshape(1, n_dt), ((0, 0), (0, LANES - n_dt)))
            z, xbc, dt, dtt = _ssm_in_proj(x_all, mod, w_in_p, dt_bias_p, ctx_row, d_inner, d_conv)
            conv_w_p = jnp.pad(ssm_conv_w[j], ((0, SUBLANES - SSM_CONV), (0, 0)))
            xt, bc = _ssm_conv(xbc, conv_w_p, ssm_conv_b[j], d_inner)
            y_f = _ssd_scan(xt, bc, dt, dtt, ssm_a_log[j], ssm_d[j], 0, C)
            y_b = _ssd_scan(xt, bc, dt, dtt, ssm_a_log[j], ssm_d[j], 1, C)
            row = lambda b, t: (b, t, 0)
            col = lambda b, t: (b, 0, t)
            lead_args = [y_f, y_b, z, ssm_norm_g[j].reshape(1, d_inner)]
            lead_specs = [pl.BlockSpec((1, d_inner, TM), col), pl.BlockSpec((1, d_inner, TM), col),
                          pl.BlockSpec((1, TM, d_inner), row), pl.BlockSpec((1, d_inner), lambda b, t: (0, 0))]
            x1, h2, logits = _proj_ln(_ssm_out_kernel, lead_args, lead_specs, x_all, mod,
                                      ssm_w_out[j].astype(BF16), ln1_g[i], ln1_b[i], w_router, b_router,
                                      ctx_row, alpha, "ssm_out_ln")
        x_all = _moe_layer(x1, h2, logits, mod, moe_w_gate[i], moe_w_up[i], moe_w_down[i],
                           ln2_g[i], ln2_b[i], ctx_row, alpha, 1 if last else 0)
    return x_all
```

```python
import functools
import math

import jax
import jax.numpy as jnp
from jax import lax
from jax.experimental import pallas as pl
from jax.experimental.pallas import tpu as pltpu

F32 = jnp.float32
BF16 = jnp.bfloat16
HIGHEST = lax.Precision.HIGHEST

GRID_W = 64
DA_HEADS = 8
DA_HEAD_DIM = 64
DA_V_DIM = 2 * DA_HEAD_DIM
ROPE_THETA = 10000.0
SSM_HEAD_DIM = 64
SSM_GROUPS = 4
SSM_STATE = 128
SSM_CONV = 5
SSM_CHUNK = 128
MOE_GROUPS = 4
MOE_PER_GROUP = 8
MOE_TOP_K = 2
LN_EPS = 1e-5
RMS_EPS = 1e-5
N_MIXERS = 2

LANES = 128
SUBLANES = 8
TM = 256
MOE_BM = 256
ATTN_TK = 2816
VT_ROWS = DA_V_DIM + 16
VMEM_LIMIT = 56 * 1024 * 1024


def _cparams(*sem):
    return pltpu.CompilerParams(dimension_semantics=sem, vmem_limit_bytes=VMEM_LIMIT)


def _silu(v):
    return v / (1.0 + jnp.exp(-v))


def _layer_norm(r, g, b):
    mu = jnp.mean(r, axis=-1, keepdims=True)
    d = r - mu
    var = jnp.mean(d * d, axis=-1, keepdims=True)
    return d * lax.rsqrt(var + LN_EPS) * g + b


def _mod_row(ctx_row):
    return lambda b, t: (jnp.where(t == 0, ctx_row, b), 0, 0)


def _mod_kernel(c_ref, w_ref, b_ref, o_ref):
    s = _silu(c_ref[...])
    o_ref[0] = jnp.dot(s, w_ref[0], precision=HIGHEST, preferred_element_type=F32) + b_ref[0]


def _modulation(cond, w_mod, b_mod):
    L, D, D6 = w_mod.shape
    R = cond.shape[0]
    tn = 1536
    return pl.pallas_call(
        _mod_kernel,
        grid=(L, D6 // tn),
        in_specs=[pl.BlockSpec((R, D), lambda l, j: (0, 0)),
                  pl.BlockSpec((1, D, tn), lambda l, j: (l, 0, j)),
                  pl.BlockSpec((1, 1, tn), lambda l, j: (l, 0, j))],
        out_specs=pl.BlockSpec((1, R, tn), lambda l, j: (l, 0, j)),
        out_shape=jax.ShapeDtypeStruct((L, R, D6), F32),
        compiler_params=_cparams("parallel", "parallel"),
        name="modulation",
    )(cond, w_mod, b_mod.reshape(L, 1, D6))


def _qkv_kernel(x_ref, mod_ref, w_ref, cos_ref, sa_ref, sb_ref, qt_ref, k_ref, vt_ref, *, d_model, q_scale):
    D = d_model
    H = k_ref.shape[0]
    x = x_ref[0]
    h = (x * (1.0 + mod_ref[0, :, D:2 * D]) + mod_ref[0, :, 0:D]).astype(BF16)
    cos = cos_ref[...]
    sa = sa_ref[...]
    sb = sb_ref[...]
    first_map = lax.broadcasted_iota(jnp.int32, (LANES, TM), 0) < DA_HEAD_DIM
    ones_rows = (lax.broadcasted_iota(jnp.int32, (VT_ROWS - DA_V_DIM, TM), 0) == 0).astype(BF16)

    def rope(a):
        return a * cos + pltpu.roll(a, LANES - 16, 1) * sa + pltpu.roll(a, 16, 1) * sb

    for jj in range(3 * H // 2):
        acc2 = jnp.dot(h, w_ref[:, jj * 2 * LANES:(jj + 1) * 2 * LANES], preferred_element_type=F32)
        for half in range(2):
            j = 2 * jj + half
            acc = acc2[:, half * LANES:(half + 1) * LANES]
            hd = j % H
            if j < H:
                qt = (rope(acc) * q_scale).T
                qt_ref[hd, 0, 0] = jnp.where(first_map, qt, 0.0).astype(BF16)
                qt_ref[hd, 0, 1] = jnp.where(first_map, 0.0, qt).astype(BF16)
            elif j < 2 * H:
                k_ref[hd, 0] = rope(acc).astype(BF16)
            else:
                vt_ref[hd, 0, 0:DA_V_DIM, :] = acc.T.astype(BF16)
                vt_ref[hd, 0, DA_V_DIM:VT_ROWS, :] = ones_rows


def _qkv_proj(x_all, mod, w_qkv, cos, sa, sb, ctx_row):
    B, N, D = x_all.shape
    H = DA_HEADS
    kern = functools.partial(_qkv_kernel, d_model=D, q_scale=DA_HEAD_DIM ** -0.5 * math.log2(math.e))
    return pl.pallas_call(
        kern,
        grid=(B, N // TM),
        in_specs=[pl.BlockSpec((1, TM, D), lambda b, t: (b, t, 0)),
                  pl.BlockSpec((1, 1, 6 * D), _mod_row(ctx_row)),
                  pl.BlockSpec((D, w_qkv.shape[1]), lambda b, t: (0, 0)),
                  pl.BlockSpec((TM, LANES), lambda b, t: (t, 0)),
                  pl.BlockSpec((TM, LANES), lambda b, t: (t, 0)),
                  pl.BlockSpec((TM, LANES), lambda b, t: (t, 0))],
        out_specs=[pl.BlockSpec((H, 1, 2, LANES, TM), lambda b, t: (0, b, 0, 0, t)),
                   pl.BlockSpec((H, 1, TM, LANES), lambda b, t: (0, b, t, 0)),
                   pl.BlockSpec((H, 1, VT_ROWS, TM), lambda b, t: (0, b, 0, t))],
        out_shape=[jax.ShapeDtypeStruct((H, B, 2, LANES, N), BF16),
                   jax.ShapeDtypeStruct((H, B, N, LANES), BF16),
                   jax.ShapeDtypeStruct((H, B, VT_ROWS, N), BF16)],
        compiler_params=_cparams("parallel", "parallel"),
        name="qkv_proj",
    )(x_all, mod, w_qkv, cos, sa, sb)


def _rope_tables(n_ctx, n_lat):
    t = jnp.arange(n_lat)
    pos = jnp.stack([t // GRID_W, t % GRID_W], axis=1).astype(F32)
    axis_dims = DA_HEAD_DIM // 2
    inv = ROPE_THETA ** (-jnp.arange(0, axis_dims, 2, dtype=F32) / axis_dims)
    lane = jnp.arange(LANES)
    d = lane % DA_HEAD_DIM
    axis = d // axis_dims
    second = (d % axis_dims) // (axis_dims // 2)
    ang = pos[:, axis] * inv[d % (axis_dims // 2)][None, :]
    cos = jnp.cos(ang)
    sin = jnp.sin(ang)
    sa = jnp.where(second[None, :] == 0, -sin, 0.0)
    sb = jnp.where(second[None, :] == 1, sin, 0.0)
    pad = lambda a, v: jnp.concatenate([jnp.full((n_ctx, LANES), v, F32), a], axis=0)
    return pad(cos, 1.0), pad(sa, 0.0), pad(sb, 0.0)


def _attn_tile(qt_ref, k_ref, vt_ref, bufs, n_chunks, tk):
    tq = qt_ref.shape[-1]

    def scores(off, slot):
        k = k_ref[0, 0, pl.ds(off, tk), :]
        for j in range(2):
            bufs[slot][j, 0:tk, :] = jnp.dot(k, qt_ref[0, 0, j], preferred_element_type=F32)

    def softmax_pv(off, slot, carry):
        vt = vt_ref[0, 0, :, pl.ds(off, tk)]
        new = []
        for j in range(2):
            m, acc = carry[j]
            s = bufs[slot][j, 0:tk, :]
            m_new = jnp.maximum(m, jnp.max(s, axis=0, keepdims=True))
            p = jnp.exp2(s - m_new).astype(BF16)
            acc = jnp.exp2(m - m_new) * acc + jnp.dot(vt, p, preferred_element_type=F32)
            new.append((m_new, acc))
        return tuple(new)

    carry = tuple((jnp.full((1, tq), -jnp.inf, F32), jnp.zeros((VT_ROWS, tq), F32)) for _ in range(2))
    scores(0, 0)
    n_pairs = (n_chunks - 1) // 2
    if n_pairs > 0:
        def body(i, carry):
            base = pl.multiple_of(2 * i * tk, tk)
            scores(base + tk, 1)
            carry = softmax_pv(base, 0, carry)
            scores(base + 2 * tk, 0)
            return softmax_pv(base + tk, 1, carry)
        carry = lax.fori_loop(0, n_pairs, body, carry)
    done = 2 * n_pairs
    if n_chunks - done == 2:
        scores((done + 1) * tk, 1)
        carry = softmax_pv(done * tk, 0, carry)
        carry = softmax_pv((done + 1) * tk, 1, carry)
    else:
        carry = softmax_pv(done * tk, 0, carry)
    return tuple(acc[0:DA_V_DIM] / acc[DA_V_DIM:DA_V_DIM + 1] for _, acc in carry)


def _attn_kernel(lam_ref, g_ref, qt_ref, k_ref, vt_ref, o_ref, s0_ref, s1_ref, *, n_ctx, n_all, tk, lambda_init):
    t = pl.program_id(2)
    lp = lam_ref[...]
    lam = (jnp.exp(jnp.sum(lp[0:1] * lp[1:2], axis=-1, keepdims=True))
           - jnp.exp(jnp.sum(lp[2:3] * lp[3:4], axis=-1, keepdims=True)) + lambda_init)

    def finish(o0, o1):
        o = o0 - lam * o1
        ms = jnp.mean(o * o, axis=0, keepdims=True)
        o = o * lax.rsqrt(ms + RMS_EPS) * (g_ref[...] * (1.0 - lambda_init))
        o_ref[0] = o.T.astype(o_ref.dtype)

    @pl.when(t == 0)
    def _():
        finish(*_attn_tile(qt_ref, k_ref, vt_ref, (s0_ref, s1_ref), 1, n_ctx))

    @pl.when(t > 0)
    def _():
        finish(*_attn_tile(qt_ref, k_ref, vt_ref, (s0_ref, s1_ref), n_all // tk, tk))


def _diff_attention(qt, k, vt, lam_params, subln_g, lambda_init, n_ctx):
    H, B, N, _ = k.shape
    tk = ATTN_TK
    assert n_ctx == TM and N % tk == 0 and n_ctx <= tk
    kern = functools.partial(_attn_kernel, n_ctx=n_ctx, n_all=N, tk=tk, lambda_init=lambda_init)
    return pl.pallas_call(
        kern,
        grid=(B, H, N // TM),
        in_specs=[pl.BlockSpec(lam_params.shape, lambda b, h, t: (0, 0)),
                  pl.BlockSpec((DA_V_DIM, 1), lambda b, h, t: (0, 0)),
                  pl.BlockSpec((1, 1, 2, LANES, TM), lambda b, h, t: (h, b, 0, 0, t)),
                  pl.BlockSpec((1, 1, N, LANES), lambda b, h, t: (h, b, 0, 0)),
                  pl.BlockSpec((1, 1, VT_ROWS, N), lambda b, h, t: (h, b, 0, 0))],
        out_specs=pl.BlockSpec((1, TM, DA_V_DIM), lambda b, h, t: (b, t, h)),
        out_shape=jax.ShapeDtypeStruct((B, N, H * DA_V_DIM), BF16),
        scratch_shapes=[pltpu.VMEM((2, tk, TM), F32), pltpu.VMEM((2, tk, TM), F32)],
        compiler_params=_cparams("parallel", "parallel", "parallel"),
        name="diff_attention",
    )(lam_params, subln_g.reshape(DA_V_DIM, 1), qt, k, vt)


ROUTE_ID, ROUTE_GATE, ROUTE_RANK = 0, MOE_TOP_K, 2 * MOE_TOP_K
ROUTE_ROWS = SUBLANES
ROUTER_EXPERT_LANE = SUBLANES


def _route_tile(lg, cnt_ref, n_experts):
    G, PER = MOE_GROUPS, MOE_PER_GROUP
    tm = lg.shape[0]
    lt = lg.T
    row = lax.broadcasted_iota(jnp.int32, (SUBLANES, tm), 0)
    ninf = -jnp.inf
    first = lambda hit: jnp.min(jnp.where(hit, row, SUBLANES), axis=0, keepdims=True)
    gl = jnp.where(row < G, lt[0:SUBLANES], ninf)
    gmax = jnp.max(gl, axis=0, keepdims=True)
    g_top = 1.0 / jnp.sum(jnp.exp(gl - gmax), axis=0, keepdims=True)
    g_idx = first(gl == gmax)
    el = lt[ROUTER_EXPERT_LANE:ROUTER_EXPERT_LANE + PER]
    for g in range(1, G):
        el = jnp.where(g_idx == g, lt[ROUTER_EXPERT_LANE + g * PER:ROUTER_EXPERT_LANE + (g + 1) * PER], el)
    e1 = jnp.max(el, axis=0, keepdims=True)
    i1 = first(el == e1)
    el2 = jnp.where(row == i1, ninf, el)
    e2 = jnp.max(el2, axis=0, keepdims=True)
    i2 = first(el2 == e2)
    id1 = g_idx * PER + i1
    id2 = g_idx * PER + i2
    w2 = jnp.exp(e2 - e1)
    den = 1.0 + w2
    erow = lax.broadcasted_iota(jnp.int32, (n_experts, tm), 0)
    hit1 = erow == id1
    hit2 = erow == id2
    onehot = jnp.logical_or(hit1, hit2)
    ui = lax.broadcasted_iota(jnp.int32, (tm, tm), 0)
    ti = lax.broadcasted_iota(jnp.int32, (tm, tm), 1)
    earlier = jnp.dot(onehot.astype(BF16), (ui < ti).astype(BF16), preferred_element_type=F32) + cnt_ref[...]
    r1 = jnp.sum(jnp.where(hit1, earlier, 0.0), axis=0, keepdims=True)
    r2 = jnp.sum(jnp.where(hit2, earlier, 0.0), axis=0, keepdims=True)
    cnt_ref[...] = cnt_ref[...] + jnp.sum(onehot.astype(F32), axis=1, keepdims=True)
    rec = jnp.zeros((ROUTE_ROWS, tm), F32)
    for r, val in ((ROUTE_ID, id1.astype(F32)), (ROUTE_ID + 1, id2.astype(F32)),
                   (ROUTE_GATE, g_top / den), (ROUTE_GATE + 1, g_top * w2 / den),
                   (ROUTE_RANK, r1), (ROUTE_RANK + 1, r2)):
        rec = jnp.where(row == r, val, rec)
    return rec


def _proj_ln_tail(a, x_ref, mod_ref, w_ref, lng_ref, lnb_ref, wr_ref, br_ref, x_out, h_out, rt_out, cnt_out, *,
                  d_model, alpha, n_experts):
    D = d_model

    @pl.when(jnp.logical_and(pl.program_id(0) == 0, pl.program_id(1) == 0))
    def _():
        cnt_out[...] = jnp.zeros_like(cnt_out)

    o = jnp.dot(a, w_ref[...], preferred_element_type=F32)
    r = alpha * x_ref[0] + mod_ref[0, :, 2 * D:3 * D] * o
    xn = _layer_norm(r, lng_ref[...], lnb_ref[...])
    x_out[0] = xn
    h2 = xn * (1.0 + mod_ref[0, :, 4 * D:5 * D]) + mod_ref[0, :, 3 * D:4 * D]
    h_out[0] = h2.astype(h_out.dtype)
    lg = jnp.dot(h2, wr_ref[...], precision=HIGHEST, preferred_element_type=F32) + br_ref[...]
    rt_out[0] = _route_tile(lg, cnt_out, n_experts)


def _attn_out_kernel(a_ref, *refs, **kw):
    _proj_ln_tail(a_ref[0], *refs, **kw)


def _ssm_out_kernel(yf_ref, yb_ref, z_ref, ng_ref, *refs, **kw):
    y = (yf_ref[0].astype(F32) + yb_ref[0].astype(F32)).T
    gz = y * _silu(z_ref[0].astype(F32))
    ms = jnp.mean(gz * gz, axis=-1, keepdims=True)
    a = (gz * lax.rsqrt(ms + RMS_EPS) * ng_ref[...]).astype(BF16)
    _proj_ln_tail(a, *refs, **kw)


def _proj_ln(kernel_fn, lead_args, lead_specs, x_all, mod, w, ln_g, ln_b, w_router, b_router, ctx_row, alpha,
             n_experts, name):
    B, N, D = x_all.shape
    row = lambda b, t: (b, t, 0)
    const2 = lambda b, t: (0, 0)
    kern = functools.partial(kernel_fn, d_model=D, alpha=alpha, n_experts=n_experts)
    return pl.pallas_call(
        kern,
        grid=(B, N // TM),
        in_specs=lead_specs + [
            pl.BlockSpec((1, TM, D), row),
            pl.BlockSpec((1, 1, 6 * D), _mod_row(ctx_row)),
            pl.BlockSpec(w.shape, const2),
            pl.BlockSpec((1, D), const2),
            pl.BlockSpec((1, D), const2),
            pl.BlockSpec(w_router.shape, const2),
            pl.BlockSpec((1, LANES), const2)],
        out_specs=[pl.BlockSpec((1, TM, D), row), pl.BlockSpec((1, TM, D), row),
                   pl.BlockSpec((1, ROUTE_ROWS, TM), lambda b, t: (b, 0, t)), pl.BlockSpec((n_experts, 1), const2)],
        out_shape=[jax.ShapeDtypeStruct((B, N, D), F32), jax.ShapeDtypeStruct((B, N, D), BF16),
                   jax.ShapeDtypeStruct((B, ROUTE_ROWS, N), F32), jax.ShapeDtypeStruct((n_experts, 1), F32)],
        compiler_params=_cparams("arbitrary", "arbitrary"),
        name=name,
    )(*lead_args, x_all, mod, w, ln_g.reshape(1, D), ln_b.reshape(1, D), w_router, b_router)


def _moe_kernel(be_ref, nu_ref, x_ref, wg_ref, wu_ref, wd_ref, o_ref, wgu_bf, wd_bf, *, d_ff):
    i = pl.program_id(0)
    active = i < nu_ref[0]
    new_expert = jnp.logical_or(i == 0, be_ref[i] != be_ref[jnp.maximum(i - 1, 0)])

    @pl.when(jnp.logical_and(active, new_expert))
    def _():
        wgu_bf[:, 0:d_ff] = wg_ref[0].astype(BF16)
        wgu_bf[:, d_ff:] = wu_ref[0].astype(BF16)
        wd_bf[...] = wd_ref[0].astype(BF16)

    @pl.when(active)
    def _():
        gu = jnp.dot(x_ref[...], wgu_bf[...], preferred_element_type=F32)
        mid = (_silu(gu[:, :d_ff]) * gu[:, d_ff:]).astype(BF16)
        o_ref[...] = jnp.dot(mid, wd_bf[...], preferred_element_type=F32).astype(o_ref.dtype)

    @pl.when(jnp.logical_not(active))
    def _():
        o_ref[...] = jnp.zeros_like(o_ref)


def _moe_experts(buf, blk_expert, n_used, w_gate, w_up, w_down):
    n_rows, D = buf.shape
    d_ff = w_down.shape[1]
    grid_spec = pltpu.PrefetchScalarGridSpec(
        num_scalar_prefetch=2,
        grid=(n_rows // MOE_BM,),
        in_specs=[pl.BlockSpec((MOE_BM, D), lambda i, be, nu: (i, 0)),
                  pl.BlockSpec((1, D, d_ff), lambda i, be, nu: (be[i], 0, 0)),
                  pl.BlockSpec((1, D, d_ff), lambda i, be, nu: (be[i], 0, 0)),
                  pl.BlockSpec((1, d_ff, D), lambda i, be, nu: (be[i], 0, 0))],
        out_specs=pl.BlockSpec((MOE_BM, D), lambda i, be, nu: (i, 0)),
        scratch_shapes=[pltpu.VMEM((D, 2 * d_ff), BF16), pltpu.VMEM((d_ff, D), BF16)])
    return pl.pallas_call(
        functools.partial(_moe_kernel, d_ff=d_ff),
        grid_spec=grid_spec,
        out_shape=jax.ShapeDtypeStruct((n_rows, D), BF16),
        compiler_params=_cparams("arbitrary"),
        name="moe_experts",
    )(blk_expert, n_used, buf, w_gate, w_up, w_down)


def _dispatch(eid, rank, sizes):
    K, T = eid.shape
    A = T * K
    E = sizes.shape[0]
    padded = (sizes + MOE_BM - 1) // MOE_BM * MOE_BM
    pad_end = jnp.cumsum(padded)
    pad_start = pad_end - padded
    starts = jnp.cumsum(sizes) - sizes
    dest = jnp.take(pad_start, eid, mode='clip') + rank
    n_blocks = -(-A // MOE_BM) + E
    blk_start = jnp.arange(n_blocks, dtype=jnp.int32) * MOE_BM
    blk_expert = jnp.minimum(jnp.sum(blk_start[:, None] >= pad_end[None, :], axis=1), E - 1).astype(jnp.int32)
    n_used = (pad_end[-1] // MOE_BM).astype(jnp.int32).reshape(1)
    order = jnp.argsort(eid.T.reshape(A))
    within = jnp.arange(MOE_BM, dtype=jnp.int32)[None, :] + (blk_start - pad_start[blk_expert])[:, None]
    src = jnp.clip(starts[blk_expert][:, None] + within, 0, A - 1)
    slot_tok = jnp.where(within < sizes[blk_expert][:, None], jnp.take(order, src, mode='clip') // K, 0)
    return slot_tok.reshape(-1).astype(jnp.int32), blk_expert, n_used, dest.astype(jnp.int32)


def _combine_kernel(x_ref, y0_ref, y1_ref, rt_ref, mod_ref, lng_ref, lnb_ref, o_ref, *, d_model, alpha):
    D = d_model
    rt = rt_ref[0].T
    f = (rt[:, ROUTE_GATE:ROUTE_GATE + 1] * y0_ref[0, 0].astype(F32)
         + rt[:, ROUTE_GATE + 1:ROUTE_GATE + 2] * y1_ref[0, 0].astype(F32))
    r = alpha * x_ref[0] + mod_ref[0, :, 5 * D:6 * D] * f
    o_ref[0] = _layer_norm(r, lng_ref[...], lnb_ref[...])


def _combine_ln(x_all, ys, route, mod, ln_g, ln_b, ctx_row, alpha, t0):
    B, N, D = x_all.shape
    nt = N // TM - t0
    tiles_per_batch = N // TM
    row = lambda b, t: (b, t + t0, 0)
    const2 = lambda b, t: (0, 0)
    mrow = _mod_row(ctx_row)
    return pl.pallas_call(
        functools.partial(_combine_kernel, d_model=D, alpha=alpha),
        grid=(B, nt),
        in_specs=[pl.BlockSpec((1, TM, D), row),
                  pl.BlockSpec((1, 1, TM, D), lambda b, t: (0, b * tiles_per_batch + t + t0, 0, 0)),
                  pl.BlockSpec((1, 1, TM, D), lambda b, t: (1, b * tiles_per_batch + t + t0, 0, 0)),
                  pl.BlockSpec((1, ROUTE_ROWS, TM), lambda b, t: (b, 0, t + t0)),
                  pl.BlockSpec((1, 1, 6 * D), lambda b, t: mrow(b, t + t0)),
                  pl.BlockSpec((1, D), const2),
                  pl.BlockSpec((1, D), const2)],
        out_specs=pl.BlockSpec((1, TM, D), lambda b, t: (b, t, 0)),
        out_shape=jax.ShapeDtypeStruct((B, nt * TM, D), F32),
        compiler_params=_cparams("parallel", "parallel"),
        name="moe_combine_ln",
    )(x_all, ys, ys, route, mod, ln_g.reshape(1, D), ln_b.reshape(1, D))


def _moe_layer(x1, h2, route, counts, mod, w_gate, w_up, w_down, ln_g, ln_b, ctx_row, alpha, t0):
    B, N, D = x1.shape
    T = B * N
    by_row = lambda r: jnp.swapaxes(route[:, r:r + MOE_TOP_K, :], 0, 1).reshape(MOE_TOP_K, T).astype(jnp.int32)
    eid = by_row(ROUTE_ID)
    rank = by_row(ROUTE_RANK)
    sizes = counts[:, 0].astype(jnp.int32)
    slot_tok, blk_expert, n_used, dest = _dispatch(eid, rank, sizes)
    buf = jnp.take(h2.reshape(T, D), slot_tok, axis=0, mode='clip')
    yb = _moe_experts(buf, blk_expert, n_used, w_gate, w_up, w_down)
    ys = jnp.take(yb, dest, axis=0, mode='clip').reshape(MOE_TOP_K, T // TM, TM, D)
    return _combine_ln(x1, ys, route, mod, ln_g, ln_b, ctx_row, alpha, t0)


def _softplus(v):
    return jnp.maximum(v, 0.0) + jnp.log1p(jnp.exp(-jnp.abs(v)))


def _ssm_in_kernel(x_ref, mod_ref, w_ref, dtb_ref, z_out, xbc_out, dt_out, dtt_out, *, d_model, d_inner, d_conv):
    D = d_model
    h = (x_ref[0] * (1.0 + mod_ref[0, :, D:2 * D]) + mod_ref[0, :, 0:D]).astype(BF16)
    z_out[0] = jnp.dot(h, w_ref[:, 0:d_inner], preferred_element_type=F32).astype(z_out.dtype)
    cw = 512
    for j in range(d_conv // cw):
        xbc_out[0, :, j * cw:(j + 1) * cw] = jnp.dot(
            h, w_ref[:, d_inner + j * cw:d_inner + (j + 1) * cw], preferred_element_type=F32).astype(xbc_out.dtype)
    dt = jnp.dot(h, w_ref[:, d_inner + d_conv:], preferred_element_type=F32) + dtb_ref[...]
    dt = _softplus(dt)
    dt_out[0] = dt
    dtt_out[0] = dt.T


def _ssm_in_proj(x_all, mod, w_in_p, dt_bias_p, ctx_row, d_inner, d_conv):
    B, N, D = x_all.shape
    row = lambda b, t: (b, t, 0)
    kern = functools.partial(_ssm_in_kernel, d_model=D, d_inner=d_inner, d_conv=d_conv)
    return pl.pallas_call(
        kern,
        grid=(B, N // TM),
        in_specs=[pl.BlockSpec((1, TM, D), row),
                  pl.BlockSpec((1, 1, 6 * D), _mod_row(ctx_row)),
                  pl.BlockSpec(w_in_p.shape, lambda b, t: (0, 0)),
                  pl.BlockSpec((1, LANES), lambda b, t: (0, 0))],
        out_specs=[pl.BlockSpec((1, TM, d_inner), row), pl.BlockSpec((1, TM, d_conv), row),
                   pl.BlockSpec((1, TM, LANES), row), pl.BlockSpec((1, LANES, TM), lambda b, t: (b, 0, t))],
        out_shape=[jax.ShapeDtypeStruct((B, N, d_inner), BF16), jax.ShapeDtypeStruct((B, N, d_conv), BF16),
                   jax.ShapeDtypeStruct((B, N, LANES), F32), jax.ShapeDtypeStruct((B, LANES, N), F32)],
        compiler_params=_cparams("parallel", "parallel"),
        name="ssm_in_proj",
    )(x_all, mod, w_in_p, dt_bias_p)


CONV_HALO = 16


def _conv_kernel(x_ref, prev_ref, next_ref, w_ref, b_ref, xt_ref, bc_ref, ext_ref, *, n_tiles, d_inner):
    t = pl.program_id(1)
    halo = CONV_HALO
    pad = SSM_CONV // 2
    has_prev = t >= 2
    has_next = jnp.logical_and(t >= 1, t < n_tiles - 1)
    ext_ref[0:halo, :] = jnp.where(has_prev, prev_ref[0].astype(F32), 0.0)
    ext_ref[halo:halo + TM, :] = x_ref[0].astype(F32)
    ext_ref[halo + TM:, :] = jnp.where(has_next, next_ref[0].astype(F32), 0.0)
    cw = 512
    for j in range(ext_ref.shape[1] // cw):
        cols = slice(j * cw, (j + 1) * cw)
        acc = b_ref[:, cols] + w_ref[0:1, cols] * ext_ref[halo - pad:halo - pad + TM, cols]
        for k in range(1, SSM_CONV):
            acc = acc + w_ref[k:k + 1, cols] * ext_ref[halo - pad + k:halo - pad + k + TM, cols]
        act = _silu(acc)
        if j * cw < d_inner:
            xt_ref[0, j * cw:(j + 1) * cw, :] = act.T.astype(xt_ref.dtype)
        else:
            bc_ref[0, :, j * cw - d_inner:(j + 1) * cw - d_inner] = act.astype(bc_ref.dtype)


def _ssm_conv(xbc, conv_w_p, conv_b, d_inner):
    B, N, C = xbc.shape
    n_tiles = N // TM
    hb = TM // CONV_HALO
    last_hb = N // CONV_HALO - 1
    return pl.pallas_call(
        functools.partial(_conv_kernel, n_tiles=n_tiles, d_inner=d_inner),
        grid=(B, n_tiles),
        in_specs=[pl.BlockSpec((1, TM, C), lambda b, t: (b, t, 0)),
                  pl.BlockSpec((1, CONV_HALO, C), lambda b, t: (b, jnp.maximum(t * hb - 1, 0), 0)),
                  pl.BlockSpec((1, CONV_HALO, C), lambda b, t: (b, jnp.minimum((t + 1) * hb, last_hb), 0)),
                  pl.BlockSpec(conv_w_p.shape, lambda b, t: (0, 0)),
                  pl.BlockSpec((1, C), lambda b, t: (0, 0))],
        out_specs=[pl.BlockSpec((1, d_inner, TM), lambda b, t: (b, 0, t)),
                   pl.BlockSpec((1, TM, C - d_inner), lambda b, t: (b, t, 0))],
        out_shape=[jax.ShapeDtypeStruct((B, d_inner, N), BF16), jax.ShapeDtypeStruct((B, N, C - d_inner), BF16)],
        scratch_shapes=[pltpu.VMEM((TM + 2 * CONV_HALO, C), F32)],
        compiler_params=_cparams("parallel", "parallel"),
        name="ssm_conv",
    )(xbc, xbc, xbc, conv_w_p, conv_b.reshape(1, C))


def _ssd_kernel(xt_ref, bc_ref, dt_ref, dtt_ref, alr_ref, alc_ref, dsk_ref, y_ref, state_ref, xw_ref, *,
                direction, n_heads):
    L = SSM_CHUNK
    P = SSM_HEAD_DIM
    NS = SSM_STATE
    hpg = n_heads // SSM_GROUPS
    gw = hpg * P
    reverse = direction == 1
    l0 = direction * n_heads

    @pl.when(pl.program_id(1) == 0)
    def _():
        state_ref[...] = jnp.zeros_like(state_ref)

    ri = lax.broadcasted_iota(jnp.int32, (L, L), 0)
    ci = lax.broadcasted_iota(jnp.int32, (L, L), 1)
    before = (ri >= ci) if reverse else (ri <= ci)
    after = (ci >= ri) if reverse else (ci <= ri)
    a = jnp.dot(after.astype(F32), dt_ref[0] * -jnp.exp(alr_ref[...]), precision=HIGHEST, preferred_element_type=F32)
    at = jnp.dot(dtt_ref[0] * -jnp.exp(alc_ref[...]), before.astype(F32), precision=HIGHEST,
                 preferred_element_type=F32)[l0:l0 + n_heads]
    last = 0 if reverse else L - 1
    tot = jnp.broadcast_to(at[:, last:last + 1], (n_heads, L))
    ea = jnp.exp(at)
    wgt = jnp.exp(tot - at)
    etot = jnp.exp(tot)
    dtt = dtt_ref[0][l0:l0 + n_heads]
    dsk = dsk_ref[direction]

    for g in range(SSM_GROUPS):
        bg = bc_ref[0][:, g * NS:(g + 1) * NS]
        cg = bc_ref[0][:, (SSM_GROUPS + g) * NS:(SSM_GROUPS + g + 1) * NS]
        cbt = lax.dot_general(bg, cg, (((1,), (1,)), ((), ())), preferred_element_type=F32)
        yoff = lax.dot_general(state_ref[g * gw:(g + 1) * gw, :].astype(BF16), cg, (((1,), (1,)), ((), ())),
                               preferred_element_type=F32)
        for hh in range(hpg):
            h = g * hpg + hh
            rows = slice(h * P, (h + 1) * P)
            decay = jnp.where(before, jnp.exp(at[h:h + 1, :] - a[:, l0 + h:l0 + h + 1]), 0.0)
            mt = (cbt * decay).astype(BF16)
            xh = xt_ref[0, rows, :].astype(F32)
            xdt = xh * dtt[h:h + 1, :]
            y = (jnp.dot(xdt.astype(BF16), mt, preferred_element_type=F32)
                 + yoff[hh * P:(hh + 1) * P, :] * ea[h:h + 1, :] + xh * dsk[h:h + 1, :])
            y_ref[0, rows, :] = y.astype(y_ref.dtype)
            xw_ref[hh * P:(hh + 1) * P, :] = (xdt * wgt[h:h + 1, :]).astype(xw_ref.dtype)
        upd = jnp.dot(xw_ref[...], bg, preferred_element_type=F32)
        for hh in range(hpg):
            h = g * hpg + hh
            rows = slice(h * P, (h + 1) * P)
            state_ref[rows, :] = state_ref[rows, :] * etot[h:h + 1, :] + upd[hh * P:(hh + 1) * P, :]


def _ssd_scan(xt, bc, dt, dtt, a_log, d_skip, direction, n_ctx):
    B, d_inner, N = xt.shape
    n_heads = a_log.shape[1]
    L = SSM_CHUNK
    nc = N // L
    ncc = n_ctx // L
    if direction == 0:
        chunk = lambda c: c
    else:
        chunk = lambda c: jnp.where(c < ncc, ncc - 1 - c, nc - 1 - (c - ncc))
    kern = functools.partial(_ssd_kernel, direction=direction, n_heads=n_heads)
    a_log_lanes = jnp.pad(a_log.reshape(-1), (0, LANES - a_log.size))
    d_skip_lanes = jnp.broadcast_to(d_skip.astype(F32)[:, :, None], d_skip.shape + (L,))
    return pl.pallas_call(
        kern,
        grid=(B, nc),
        in_specs=[pl.BlockSpec((1, d_inner, L), lambda b, c: (b, 0, chunk(c))),
                  pl.BlockSpec((1, L, bc.shape[2]), lambda b, c: (b, chunk(c), 0)),
                  pl.BlockSpec((1, L, LANES), lambda b, c: (b, chunk(c), 0)),
                  pl.BlockSpec((1, LANES, L), lambda b, c: (b, 0, chunk(c))),
                  pl.BlockSpec((1, LANES), lambda b, c: (0, 0)),
                  pl.BlockSpec((LANES, 1), lambda b, c: (0, 0)),
                  pl.BlockSpec(d_skip_lanes.shape, lambda b, c: (0, 0, 0))],
        out_specs=pl.BlockSpec((1, d_inner, L), lambda b, c: (b, 0, chunk(c))),
        out_shape=jax.ShapeDtypeStruct((B, d_inner, N), BF16),
        scratch_shapes=[pltpu.VMEM((d_inner, SSM_STATE), F32),
                        pltpu.VMEM((d_inner // SSM_GROUPS, L), BF16)],
        compiler_params=_cparams("parallel", "arbitrary"),
        name=f"ssd_scan_dir{direction}",
    )(xt, bc, dt, dtt, a_log_lanes.reshape(1, LANES), a_log_lanes.reshape(LANES, 1), d_skip_lanes)


def kernel(x, c, ctx, c_ctx, w_mod, b_mod, ln1_g, ln1_b, ln2_g, ln2_b, attn_w_qkv, attn_w_o, attn_lq1, attn_lk1, attn_lq2, attn_lk2, attn_subln_g, ssm_w_in, ssm_conv_w, ssm_conv_b, ssm_dt_bias, ssm_a_log, ssm_d, ssm_norm_g, ssm_w_out, moe_w_group, moe_b_group, moe_w_expert, moe_b_expert, moe_w_gate, moe_w_up, moe_w_down):
    B, S, D = x.shape
    C = ctx.shape[1]
    depth = w_mod.shape[0]
    E = moe_w_expert.shape[-1]
    assert C == TM and S % TM == 0 and B < SUBLANES
    alpha = (2 * depth) ** 0.25
    ctx_row = B

    cond = jnp.zeros((SUBLANES, D), F32).at[:B].set(c).at[B].set(c_ctx)
    mods = _modulation(cond, w_mod, b_mod)
    x_all = jnp.concatenate([ctx, x], axis=1)
    cos, sa, sb = _rope_tables(C, S)

    for i in range(depth):
        last = i == depth - 1
        j = i // N_MIXERS
        mod = mods[i].reshape(SUBLANES, 1, 6 * D)
        w_router = jnp.zeros((D, LANES), F32).at[:, :MOE_GROUPS].set(moe_w_group[i]) \
            .at[:, ROUTER_EXPERT_LANE:ROUTER_EXPERT_LANE + E].set(moe_w_expert[i])
        b_router = jnp.zeros((1, LANES), F32).at[0, :MOE_GROUPS].set(moe_b_group[i]) \
            .at[0, ROUTER_EXPERT_LANE:ROUTER_EXPERT_LANE + E].set(moe_b_expert[i])
        if i % N_MIXERS == 0:
            lambda_init = 0.8 - 0.6 * math.exp(-0.3 * i)
            qt, k, vt = _qkv_proj(x_all, mod, attn_w_qkv[j].astype(BF16), cos, sa, sb, ctx_row)
            lam_params = jnp.stack([attn_lq1[j], attn_lk1[j], attn_lq2[j], attn_lk2[j]])
            o = _diff_attention(qt, k, vt, lam_params, attn_subln_g[j], lambda_init, C)
            lead_args = [o]
            lead_specs = [pl.BlockSpec((1, TM, o.shape[-1]), lambda b, t: (b, t, 0))]
            x1, h2, route, counts = _proj_ln(_attn_out_kernel, lead_args, lead_specs, x_all, mod,
                                             attn_w_o[j].astype(BF16), ln1_g[i], ln1_b[i], w_router, b_router,
                                             ctx_row, alpha, E, "attn_out_ln")
        else:
            n_heads = ssm_a_log.shape[-1]
            d_inner = n_heads * SSM_HEAD_DIM
            d_conv = d_inner + 2 * SSM_GROUPS * SSM_STATE
            w_in = ssm_w_in[j]
            n_dt = w_in.shape[1] - d_inner - d_conv
            w_in_p = jnp.pad(w_in, ((0, 0), (0, LANES - n_dt))).astype(BF16)
            dt_bias_p = jnp.pad(ssm_dt_bias[j].reshape(1, n_dt), ((0, 0), (0, LANES - n_dt)))
            z, xbc, dt, dtt = _ssm_in_proj(x_all, mod, w_in_p, dt_bias_p, ctx_row, d_inner, d_conv)
            conv_w_p = jnp.pad(ssm_conv_w[j], ((0, SUBLANES - SSM_CONV), (0, 0)))
            xt, bc = _ssm_conv(xbc, conv_w_p, ssm_conv_b[j], d_inner)
            y_f = _ssd_scan(xt, bc, dt, dtt, ssm_a_log[j], ssm_d[j], 0, C)
            y_b = _ssd_scan(xt, bc, dt, dtt, ssm_a_log[j], ssm_d[j], 1, C)
            row = lambda b, t: (b, t, 0)
            col = lambda b, t: (b, 0, t)
            lead_args = [y_f, y_b, z, ssm_norm_g[j].reshape(1, d_inner)]
            lead_specs = [pl.BlockSpec((1, d_inner, TM), col), pl.BlockSpec((1, d_inner, TM), col),
                          pl.BlockSpec((1, TM, d_inner), row), pl.BlockSpec((1, d_inner), lambda b, t: (0, 0))]
            x1, h2, route, counts = _proj_ln(_ssm_out_kernel, lead_args, lead_specs, x_all, mod,
                                             ssm_w_out[j].astype(BF16), ln1_g[i], ln1_b[i], w_router, b_router,
                                             ctx_row, alpha, E, "ssm_out_ln")
        x_all = _moe_layer(x1, h2, route, counts, mod, moe_w_gate[i], moe_w_up[i], moe_w_down[i],
                           ln2_g[i], ln2_b[i], ctx_row, alpha, 1 if last else 0)
    return x_all
```

```python
import functools
import math

import jax
import jax.numpy as jnp
from jax import lax
from jax.experimental import pallas as pl
from jax.experimental.pallas import tpu as pltpu

F32 = jnp.float32
BF16 = jnp.bfloat16
HIGHEST = lax.Precision.HIGHEST

GRID_W = 64
DA_HEADS = 8
DA_HEAD_DIM = 64
DA_V_DIM = 2 * DA_HEAD_DIM
ROPE_THETA = 10000.0
SSM_HEAD_DIM = 64
SSM_GROUPS = 4
SSM_STATE = 128
SSM_CONV = 5
SSM_CHUNK = 128
MOE_GROUPS = 4
MOE_PER_GROUP = 8
MOE_TOP_K = 2
LN_EPS = 1e-5
RMS_EPS = 1e-5
N_MIXERS = 2

LANES = 128
SUBLANES = 8
TM = 256
MOE_BM = 256
ATTN_TK = 2816
VT_ROWS = DA_V_DIM + 16
VMEM_LIMIT = 56 * 1024 * 1024


def _cparams(*sem):
    return pltpu.CompilerParams(dimension_semantics=sem, vmem_limit_bytes=VMEM_LIMIT)


def _silu(v):
    return v / (1.0 + jnp.exp(-v))


def _layer_norm(r, g, b):
    mu = jnp.mean(r, axis=-1, keepdims=True)
    d = r - mu
    var = jnp.mean(d * d, axis=-1, keepdims=True)
    return d * lax.rsqrt(var + LN_EPS) * g + b


def _mod_row(ctx_row):
    return lambda b, t: (jnp.where(t == 0, ctx_row, b), 0, 0)


def _mod_kernel(c_ref, w_ref, b_ref, o_ref):
    s = _silu(c_ref[...])
    o_ref[0] = jnp.dot(s, w_ref[0], precision=HIGHEST, preferred_element_type=F32) + b_ref[0]


def _modulation(cond, w_mod, b_mod):
    L, D, D6 = w_mod.shape
    R = cond.shape[0]
    tn = 1536
    return pl.pallas_call(
        _mod_kernel,
        grid=(L, D6 // tn),
        in_specs=[pl.BlockSpec((R, D), lambda l, j: (0, 0)),
                  pl.BlockSpec((1, D, tn), lambda l, j: (l, 0, j)),
                  pl.BlockSpec((1, 1, tn), lambda l, j: (l, 0, j))],
        out_specs=pl.BlockSpec((1, R, tn), lambda l, j: (l, 0, j)),
        out_shape=jax.ShapeDtypeStruct((L, R, D6), F32),
        compiler_params=_cparams("parallel", "parallel"),
        name="modulation",
    )(cond, w_mod, b_mod.reshape(L, 1, D6))


def _qkv_kernel(x_ref, mod_ref, w_ref, cos_ref, sa_ref, sb_ref, qt_ref, k_ref, vt_ref, *, d_model, q_scale):
    D = d_model
    H = k_ref.shape[0]
    x = x_ref[0]
    h = (x * (1.0 + mod_ref[0, :, D:2 * D]) + mod_ref[0, :, 0:D]).astype(BF16)
    cos = cos_ref[...]
    sa = sa_ref[...]
    sb = sb_ref[...]
    first_map = lax.broadcasted_iota(jnp.int32, (LANES, TM), 0) < DA_HEAD_DIM
    ones_rows = (lax.broadcasted_iota(jnp.int32, (VT_ROWS - DA_V_DIM, TM), 0) == 0).astype(BF16)

    def rope(a):
        return a * cos + pltpu.roll(a, LANES - 16, 1) * sa + pltpu.roll(a, 16, 1) * sb

    for jj in range(3 * H // 2):
        acc2 = jnp.dot(h, w_ref[:, jj * 2 * LANES:(jj + 1) * 2 * LANES], preferred_element_type=F32)
        for half in range(2):
            j = 2 * jj + half
            acc = acc2[:, half * LANES:(half + 1) * LANES]
            hd = j % H
            if j < H:
                qt = (rope(acc) * q_scale).T
                qt_ref[hd, 0, 0] = jnp.where(first_map, qt, 0.0).astype(BF16)
                qt_ref[hd, 0, 1] = jnp.where(first_map, 0.0, qt).astype(BF16)
            elif j < 2 * H:
                k_ref[hd, 0] = rope(acc).astype(BF16)
            else:
                vt_ref[hd, 0, 0:DA_V_DIM, :] = acc.T.astype(BF16)
                vt_ref[hd, 0, DA_V_DIM:VT_ROWS, :] = ones_rows


def _qkv_proj(x_all, mod, w_qkv, cos, sa, sb, ctx_row):
    B, N, D = x_all.shape
    H = DA_HEADS
    kern = functools.partial(_qkv_kernel, d_model=D, q_scale=DA_HEAD_DIM ** -0.5 * math.log2(math.e))
    return pl.pallas_call(
        kern,
        grid=(B, N // TM),
        in_specs=[pl.BlockSpec((1, TM, D), lambda b, t: (b, t, 0)),
                  pl.BlockSpec((1, 1, 6 * D), _mod_row(ctx_row)),
                  pl.BlockSpec((D, w_qkv.shape[1]), lambda b, t: (0, 0)),
                  pl.BlockSpec((TM, LANES), lambda b, t: (t, 0)),
                  pl.BlockSpec((TM, LANES), lambda b, t: (t, 0)),
                  pl.BlockSpec((TM, LANES), lambda b, t: (t, 0))],
        out_specs=[pl.BlockSpec((H, 1, 2, LANES, TM), lambda b, t: (0, b, 0, 0, t)),
                   pl.BlockSpec((H, 1, TM, LANES), lambda b, t: (0, b, t, 0)),
                   pl.BlockSpec((H, 1, VT_ROWS, TM), lambda b, t: (0, b, 0, t))],
        out_shape=[jax.ShapeDtypeStruct((H, B, 2, LANES, N), BF16),
                   jax.ShapeDtypeStruct((H, B, N, LANES), BF16),
                   jax.ShapeDtypeStruct((H, B, VT_ROWS, N), BF16)],
        compiler_params=_cparams("parallel", "parallel"),
        name="qkv_proj",
    )(x_all, mod, w_qkv, cos, sa, sb)


def _rope_tables(n_ctx, n_lat):
    t = jnp.arange(n_lat)
    pos = jnp.stack([t // GRID_W, t % GRID_W], axis=1).astype(F32)
    axis_dims = DA_HEAD_DIM // 2
    inv = ROPE_THETA ** (-jnp.arange(0, axis_dims, 2, dtype=F32) / axis_dims)
    lane = jnp.arange(LANES)
    d = lane % DA_HEAD_DIM
    axis = d // axis_dims
    second = (d % axis_dims) // (axis_dims // 2)
    ang = pos[:, axis] * inv[d % (axis_dims // 2)][None, :]
    cos = jnp.cos(ang)
    sin = jnp.sin(ang)
    sa = jnp.where(second[None, :] == 0, -sin, 0.0)
    sb = jnp.where(second[None, :] == 1, sin, 0.0)
    pad = lambda a, v: jnp.concatenate([jnp.full((n_ctx, LANES), v, F32), a], axis=0)
    return pad(cos, 1.0), pad(sa, 0.0), pad(sb, 0.0)


def _attn_tile(qt_ref, k_ref, vt_ref, bufs, n_chunks, tk):
    tq = qt_ref.shape[-1]

    def scores(off, slot):
        k = k_ref[0, 0, pl.ds(off, tk), :]
        for j in range(2):
            bufs[slot][j, 0:tk, :] = jnp.dot(k, qt_ref[0, 0, j], preferred_element_type=F32)

    def softmax_pv(off, slot, carry):
        vt = vt_ref[0, 0, :, pl.ds(off, tk)]
        new = []
        for j in range(2):
            m, acc = carry[j]
            s = bufs[slot][j, 0:tk, :]
            m_new = jnp.maximum(m, jnp.max(s, axis=0, keepdims=True))
            p = jnp.exp2(s - m_new).astype(BF16)
            acc = jnp.exp2(m - m_new) * acc + jnp.dot(vt, p, preferred_element_type=F32)
            new.append((m_new, acc))
        return tuple(new)

    carry = tuple((jnp.full((1, tq), -jnp.inf, F32), jnp.zeros((VT_ROWS, tq), F32)) for _ in range(2))
    scores(0, 0)
    n_pairs = (n_chunks - 1) // 2
    if n_pairs > 0:
        def body(i, carry):
            base = pl.multiple_of(2 * i * tk, tk)
            scores(base + tk, 1)
            carry = softmax_pv(base, 0, carry)
            scores(base + 2 * tk, 0)
            return softmax_pv(base + tk, 1, carry)
        carry = lax.fori_loop(0, n_pairs, body, carry)
    done = 2 * n_pairs
    if n_chunks - done == 2:
        scores((done + 1) * tk, 1)
        carry = softmax_pv(done * tk, 0, carry)
        carry = softmax_pv((done + 1) * tk, 1, carry)
    else:
        carry = softmax_pv(done * tk, 0, carry)
    return tuple(acc[0:DA_V_DIM] / acc[DA_V_DIM:DA_V_DIM + 1] for _, acc in carry)


def _attn_kernel(lam_ref, g_ref, qt_ref, k_ref, vt_ref, o_ref, s0_ref, s1_ref, *, n_ctx, n_all, tk, lambda_init):
    t = pl.program_id(2)
    lp = lam_ref[...]
    lam = (jnp.exp(jnp.sum(lp[0:1] * lp[1:2], axis=-1, keepdims=True))
           - jnp.exp(jnp.sum(lp[2:3] * lp[3:4], axis=-1, keepdims=True)) + lambda_init)

    def finish(o0, o1):
        o = o0 - lam * o1
        ms = jnp.mean(o * o, axis=0, keepdims=True)
        o = o * lax.rsqrt(ms + RMS_EPS) * (g_ref[...] * (1.0 - lambda_init))
        o_ref[0] = o.T.astype(o_ref.dtype)

    @pl.when(t == 0)
    def _():
        finish(*_attn_tile(qt_ref, k_ref, vt_ref, (s0_ref, s1_ref), 1, n_ctx))

    @pl.when(t > 0)
    def _():
        finish(*_attn_tile(qt_ref, k_ref, vt_ref, (s0_ref, s1_ref), n_all // tk, tk))


def _diff_attention(qt, k, vt, lam_params, subln_g, lambda_init, n_ctx):
    H, B, N, _ = k.shape
    tk = ATTN_TK
    assert n_ctx == TM and N % tk == 0 and n_ctx <= tk
    kern = functools.partial(_attn_kernel, n_ctx=n_ctx, n_all=N, tk=tk, lambda_init=lambda_init)
    return pl.pallas_call(
        kern,
        grid=(B, H, N // TM),
        in_specs=[pl.BlockSpec(lam_params.shape, lambda b, h, t: (0, 0)),
                  pl.BlockSpec((DA_V_DIM, 1), lambda b, h, t: (0, 0)),
                  pl.BlockSpec((1, 1, 2, LANES, TM), lambda b, h, t: (h, b, 0, 0, t)),
                  pl.BlockSpec((1, 1, N, LANES), lambda b, h, t: (h, b, 0, 0)),
                  pl.BlockSpec((1, 1, VT_ROWS, N), lambda b, h, t: (h, b, 0, 0))],
        out_specs=pl.BlockSpec((1, TM, DA_V_DIM), lambda b, h, t: (b, t, h)),
        out_shape=jax.ShapeDtypeStruct((B, N, H * DA_V_DIM), BF16),
        scratch_shapes=[pltpu.VMEM((2, tk, TM), F32), pltpu.VMEM((2, tk, TM), F32)],
        compiler_params=_cparams("parallel", "parallel", "parallel"),
        name="diff_attention",
    )(lam_params, subln_g.reshape(DA_V_DIM, 1), qt, k, vt)


ROUTE_ID, ROUTE_GATE, ROUTE_RANK = 0, MOE_TOP_K, 2 * MOE_TOP_K
ROUTE_ROWS = SUBLANES
ROUTER_EXPERT_LANE = SUBLANES


def _route_tile(lg, cnt_ref, n_experts):
    G, PER = MOE_GROUPS, MOE_PER_GROUP
    tm = lg.shape[0]
    lt = lg.T
    row = lax.broadcasted_iota(jnp.int32, (SUBLANES, tm), 0)
    ninf = -jnp.inf
    first = lambda hit: jnp.min(jnp.where(hit, row, SUBLANES), axis=0, keepdims=True)
    gl = jnp.where(row < G, lt[0:SUBLANES], ninf)
    gmax = jnp.max(gl, axis=0, keepdims=True)
    g_top = 1.0 / jnp.sum(jnp.exp(gl - gmax), axis=0, keepdims=True)
    g_idx = first(gl == gmax)
    el = lt[ROUTER_EXPERT_LANE:ROUTER_EXPERT_LANE + PER]
    for g in range(1, G):
        el = jnp.where(g_idx == g, lt[ROUTER_EXPERT_LANE + g * PER:ROUTER_EXPERT_LANE + (g + 1) * PER], el)
    e1 = jnp.max(el, axis=0, keepdims=True)
    i1 = first(el == e1)
    el2 = jnp.where(row == i1, ninf, el)
    e2 = jnp.max(el2, axis=0, keepdims=True)
    i2 = first(el2 == e2)
    id1 = g_idx * PER + i1
    id2 = g_idx * PER + i2
    w2 = jnp.exp(e2 - e1)
    den = 1.0 + w2
    erow = lax.broadcasted_iota(jnp.int32, (n_experts, tm), 0)
    hit1 = erow == id1
    hit2 = erow == id2
    onehot = jnp.logical_or(hit1, hit2)
    ui = lax.broadcasted_iota(jnp.int32, (tm, tm), 0)
    ti = lax.broadcasted_iota(jnp.int32, (tm, tm), 1)
    earlier = jnp.dot(onehot.astype(BF16), (ui < ti).astype(BF16), preferred_element_type=F32) + cnt_ref[...]
    r1 = jnp.sum(jnp.where(hit1, earlier, 0.0), axis=0, keepdims=True)
    r2 = jnp.sum(jnp.where(hit2, earlier, 0.0), axis=0, keepdims=True)
    cnt_ref[...] = cnt_ref[...] + jnp.sum(onehot.astype(F32), axis=1, keepdims=True)
    rec = jnp.zeros((ROUTE_ROWS, tm), F32)
    for r, val in ((ROUTE_ID, id1.astype(F32)), (ROUTE_ID + 1, id2.astype(F32)),
                   (ROUTE_GATE, g_top / den), (ROUTE_GATE + 1, g_top * w2 / den),
                   (ROUTE_RANK, r1), (ROUTE_RANK + 1, r2)):
        rec = jnp.where(row == r, val, rec)
    return rec


def _proj_ln_tail(a, x_ref, mod_ref, w_ref, lng_ref, lnb_ref, wr_ref, br_ref, x_out, h_out, rt_out, cnt_out, *,
                  d_model, alpha, n_experts):
    D = d_model

    @pl.when(jnp.logical_and(pl.program_id(0) == 0, pl.program_id(1) == 0))
    def _():
        cnt_out[...] = jnp.zeros_like(cnt_out)

    hb = a.shape[0] // 2
    os_ = [jnp.dot(a[r * hb:(r + 1) * hb], w_ref[...], preferred_element_type=F32) for r in range(2)]
    lgs = []
    for r in range(2):
        rows = slice(r * hb, (r + 1) * hb)
        res = alpha * x_ref[0, rows, :] + mod_ref[0, :, 2 * D:3 * D] * os_[r]
        xn = _layer_norm(res, lng_ref[...], lnb_ref[...])
        x_out[0, rows, :] = xn
        h2 = xn * (1.0 + mod_ref[0, :, 4 * D:5 * D]) + mod_ref[0, :, 3 * D:4 * D]
        h_out[0, rows, :] = h2.astype(h_out.dtype)
        lgs.append(jnp.dot(h2, wr_ref[...], precision=HIGHEST, preferred_element_type=F32) + br_ref[...])
    lg = jnp.concatenate(lgs, axis=0)
    rt_out[0] = _route_tile(lg, cnt_out, n_experts)


def _attn_out_kernel(a_ref, *refs, **kw):
    _proj_ln_tail(a_ref[0], *refs, **kw)


def _ssm_out_kernel(yf_ref, yb_ref, z_ref, ng_ref, *refs, **kw):
    y = (yf_ref[0].astype(F32) + yb_ref[0].astype(F32)).T
    gz = y * _silu(z_ref[0].astype(F32))
    ms = jnp.mean(gz * gz, axis=-1, keepdims=True)
    a = (gz * lax.rsqrt(ms + RMS_EPS) * ng_ref[...]).astype(BF16)
    _proj_ln_tail(a, *refs, **kw)


def _proj_ln(kernel_fn, lead_args, lead_specs, x_all, mod, w, ln_g, ln_b, w_router, b_router, ctx_row, alpha,
             n_experts, name):
    B, N, D = x_all.shape
    row = lambda b, t: (b, t, 0)
    const2 = lambda b, t: (0, 0)
    kern = functools.partial(kernel_fn, d_model=D, alpha=alpha, n_experts=n_experts)
    return pl.pallas_call(
        kern,
        grid=(B, N // TM),
        in_specs=lead_specs + [
            pl.BlockSpec((1, TM, D), row),
            pl.BlockSpec((1, 1, 6 * D), _mod_row(ctx_row)),
            pl.BlockSpec(w.shape, const2),
            pl.BlockSpec((1, D), const2),
            pl.BlockSpec((1, D), const2),
            pl.BlockSpec(w_router.shape, const2),
            pl.BlockSpec((1, LANES), const2)],
        out_specs=[pl.BlockSpec((1, TM, D), row), pl.BlockSpec((1, TM, D), row),
                   pl.BlockSpec((1, ROUTE_ROWS, TM), lambda b, t: (b, 0, t)), pl.BlockSpec((n_experts, 1), const2)],
        out_shape=[jax.ShapeDtypeStruct((B, N, D), F32), jax.ShapeDtypeStruct((B, N, D), BF16),
                   jax.ShapeDtypeStruct((B, ROUTE_ROWS, N), F32), jax.ShapeDtypeStruct((n_experts, 1), F32)],
        compiler_params=_cparams("arbitrary", "arbitrary"),
        name=name,
    )(*lead_args, x_all, mod, w, ln_g.reshape(1, D), ln_b.reshape(1, D), w_router, b_router)


def _moe_kernel(be_ref, nu_ref, x_ref, wg_ref, wu_ref, wd_ref, o_ref, wgu_bf, wd_bf, *, d_ff):
    i = pl.program_id(0)
    active = i < nu_ref[0]
    new_expert = jnp.logical_or(i == 0, be_ref[i] != be_ref[jnp.maximum(i - 1, 0)])

    @pl.when(jnp.logical_and(active, new_expert))
    def _():
        wgu_bf[:, 0:d_ff] = wg_ref[0, 0].astype(BF16)
        wgu_bf[:, d_ff:] = wu_ref[0, 0].astype(BF16)
        wd_bf[...] = wd_ref[0, 0].astype(BF16)

    @pl.when(active)
    def _():
        hb = x_ref.shape[0] // 2
        gus = [jnp.dot(x_ref[r * hb:(r + 1) * hb, :], wgu_bf[...], preferred_element_type=F32) for r in range(2)]
        for r in range(2):
            mid = (_silu(gus[r][:, :d_ff]) * gus[r][:, d_ff:]).astype(BF16)
            o_ref[r * hb:(r + 1) * hb, :] = jnp.dot(mid, wd_bf[...], preferred_element_type=F32).astype(o_ref.dtype)

    @pl.when(jnp.logical_not(active))
    def _():
        o_ref[...] = jnp.zeros_like(o_ref)


def _moe_experts(buf, blk_expert, n_used, w_gate, w_up, w_down, layer):
    n_rows, D = buf.shape
    d_ff = w_down.shape[2]
    grid_spec = pltpu.PrefetchScalarGridSpec(
        num_scalar_prefetch=2,
        grid=(n_rows // MOE_BM,),
        in_specs=[pl.BlockSpec((MOE_BM, D), lambda i, be, nu: (i, 0)),
                  pl.BlockSpec((1, 1, D, d_ff), lambda i, be, nu: (layer, be[i], 0, 0)),
                  pl.BlockSpec((1, 1, D, d_ff), lambda i, be, nu: (layer, be[i], 0, 0)),
                  pl.BlockSpec((1, 1, d_ff, D), lambda i, be, nu: (layer, be[i], 0, 0))],
        out_specs=pl.BlockSpec((MOE_BM, D), lambda i, be, nu: (i, 0)),
        scratch_shapes=[pltpu.VMEM((D, 2 * d_ff), BF16), pltpu.VMEM((d_ff, D), BF16)])
    return pl.pallas_call(
        functools.partial(_moe_kernel, d_ff=d_ff),
        grid_spec=grid_spec,
        out_shape=jax.ShapeDtypeStruct((n_rows, D), BF16),
        compiler_params=_cparams("arbitrary"),
        name="moe_experts",
    )(blk_expert, n_used, buf, w_gate, w_up, w_down)


def _dispatch(eid, rank, sizes):
    K, T = eid.shape
    A = T * K
    E = sizes.shape[0]
    padded = (sizes + MOE_BM - 1) // MOE_BM * MOE_BM
    pad_end = jnp.cumsum(padded)
    pad_start = pad_end - padded
    starts = jnp.cumsum(sizes) - sizes
    hit = eid[:, :, None] == jnp.arange(E, dtype=jnp.int32)
    dest = jnp.sum(jnp.where(hit, pad_start, 0), axis=-1) + rank
    n_blocks = -(-A // MOE_BM) + E
    blk_start = jnp.arange(n_blocks, dtype=jnp.int32) * MOE_BM
    blk_expert = jnp.minimum(jnp.sum(blk_start[:, None] >= pad_end[None, :], axis=1), E - 1).astype(jnp.int32)
    n_used = (pad_end[-1] // MOE_BM).astype(jnp.int32).reshape(1)
    order = jnp.argsort(eid.T.reshape(A))
    within = jnp.arange(MOE_BM, dtype=jnp.int32)[None, :] + (blk_start - pad_start[blk_expert])[:, None]
    src = jnp.clip(starts[blk_expert][:, None] + within, 0, A - 1)
    slot_tok = jnp.where(within < sizes[blk_expert][:, None], jnp.take(order, src, mode='clip') // K, 0)
    return slot_tok.reshape(-1).astype(jnp.int32), blk_expert, n_used, dest.astype(jnp.int32)


def _combine_kernel(x_ref, y0_ref, y1_ref, rt_ref, mod_ref, lng_ref, lnb_ref, o_ref, *, d_model, alpha):
    D = d_model
    rt = rt_ref[0].T
    f = (rt[:, ROUTE_GATE:ROUTE_GATE + 1] * y0_ref[0, 0].astype(F32)
         + rt[:, ROUTE_GATE + 1:ROUTE_GATE + 2] * y1_ref[0, 0].astype(F32))
    r = alpha * x_ref[0] + mod_ref[0, :, 5 * D:6 * D] * f
    o_ref[0] = _layer_norm(r, lng_ref[...], lnb_ref[...])


def _combine_ln(x_all, ys, route, mod, ln_g, ln_b, ctx_row, alpha, t0):
    B, N, D = x_all.shape
    nt = N // TM - t0
    tiles_per_batch = N // TM
    row = lambda b, t: (b, t + t0, 0)
    const2 = lambda b, t: (0, 0)
    mrow = _mod_row(ctx_row)
    return pl.pallas_call(
        functools.partial(_combine_kernel, d_model=D, alpha=alpha),
        grid=(B, nt),
        in_specs=[pl.BlockSpec((1, TM, D), row),
                  pl.BlockSpec((1, 1, TM, D), lambda b, t: (0, b * tiles_per_batch + t + t0, 0, 0)),
                  pl.BlockSpec((1, 1, TM, D), lambda b, t: (1, b * tiles_per_batch + t + t0, 0, 0)),
                  pl.BlockSpec((1, ROUTE_ROWS, TM), lambda b, t: (b, 0, t + t0)),
                  pl.BlockSpec((1, 1, 6 * D), lambda b, t: mrow(b, t + t0)),
                  pl.BlockSpec((1, D), const2),
                  pl.BlockSpec((1, D), const2)],
        out_specs=pl.BlockSpec((1, TM, D), lambda b, t: (b, t, 0)),
        out_shape=jax.ShapeDtypeStruct((B, nt * TM, D), F32),
        compiler_params=_cparams("parallel", "parallel"),
        name="moe_combine_ln",
    )(x_all, ys, ys, route, mod, ln_g.reshape(1, D), ln_b.reshape(1, D))


def _moe_layer(x1, h2, route, counts, mod, w_gate, w_up, w_down, layer, ln_g, ln_b, ctx_row, alpha, t0):
    B, N, D = x1.shape
    T = B * N
    by_row = lambda r: jnp.swapaxes(route[:, r:r + MOE_TOP_K, :], 0, 1).reshape(MOE_TOP_K, T).astype(jnp.int32)
    eid = by_row(ROUTE_ID)
    rank = by_row(ROUTE_RANK)
    sizes = counts[:, 0].astype(jnp.int32)
    slot_tok, blk_expert, n_used, dest = _dispatch(eid, rank, sizes)
    buf = jnp.take(h2.reshape(T, D), slot_tok, axis=0, mode='clip')
    yb = _moe_experts(buf, blk_expert, n_used, w_gate, w_up, w_down, layer)
    ys = jnp.take(yb, dest, axis=0, mode='clip').reshape(MOE_TOP_K, T // TM, TM, D)
    return _combine_ln(x1, ys, route, mod, ln_g, ln_b, ctx_row, alpha, t0)


def _softplus(v):
    return jnp.maximum(v, 0.0) + jnp.log1p(jnp.exp(-jnp.abs(v)))


def _ssm_in_kernel(x_ref, mod_ref, w_ref, dtb_ref, z_out, xbc_out, dt_out, dtt_out, *, d_model, d_inner, d_conv):
    D = d_model
    h = (x_ref[0] * (1.0 + mod_ref[0, :, D:2 * D]) + mod_ref[0, :, 0:D]).astype(BF16)
    z_out[0] = jnp.dot(h, w_ref[:, 0:d_inner], preferred_element_type=F32).astype(z_out.dtype)
    cw = 512
    for j in range(d_conv // cw):
        xbc_out[0, :, j * cw:(j + 1) * cw] = jnp.dot(
            h, w_ref[:, d_inner + j * cw:d_inner + (j + 1) * cw], preferred_element_type=F32).astype(xbc_out.dtype)
    dt = jnp.dot(h, w_ref[:, d_inner + d_conv:], preferred_element_type=F32) + dtb_ref[...]
    dt = _softplus(dt)
    dt_out[0] = dt
    dtt_out[0] = dt.T


def _ssm_in_proj(x_all, mod, w_in_p, dt_bias_p, ctx_row, d_inner, d_conv):
    B, N, D = x_all.shape
    row = lambda b, t: (b, t, 0)
    kern = functools.partial(_ssm_in_kernel, d_model=D, d_inner=d_inner, d_conv=d_conv)
    return pl.pallas_call(
        kern,
        grid=(B, N // TM),
        in_specs=[pl.BlockSpec((1, TM, D), row),
                  pl.BlockSpec((1, 1, 6 * D), _mod_row(ctx_row)),
                  pl.BlockSpec(w_in_p.shape, lambda b, t: (0, 0)),
                  pl.BlockSpec((1, LANES), lambda b, t: (0, 0))],
        out_specs=[pl.BlockSpec((1, TM, d_inner), row), pl.BlockSpec((1, TM, d_conv), row),
                   pl.BlockSpec((1, TM, LANES), row), pl.BlockSpec((1, LANES, TM), lambda b, t: (b, 0, t))],
        out_shape=[jax.ShapeDtypeStruct((B, N, d_inner), BF16), jax.ShapeDtypeStruct((B, N, d_conv), BF16),
                   jax.ShapeDtypeStruct((B, N, LANES), F32), jax.ShapeDtypeStruct((B, LANES, N), F32)],
        compiler_params=_cparams("parallel", "parallel"),
        name="ssm_in_proj",
    )(x_all, mod, w_in_p, dt_bias_p)


CONV_HALO = 16


def _conv_kernel(x_ref, prev_ref, next_ref, w_ref, b_ref, xt_ref, bc_ref, ext_ref, *, n_tiles, d_inner):
    t = pl.program_id(1)
    halo = CONV_HALO
    pad = SSM_CONV // 2
    has_prev = t >= 2
    has_next = jnp.logical_and(t >= 1, t < n_tiles - 1)
    ext_ref[0:halo, :] = jnp.where(has_prev, prev_ref[0].astype(F32), 0.0)
    ext_ref[halo:halo + TM, :] = x_ref[0].astype(F32)
    ext_ref[halo + TM:, :] = jnp.where(has_next, next_ref[0].astype(F32), 0.0)
    cw = 512
    for j in range(ext_ref.shape[1] // cw):
        cols = slice(j * cw, (j + 1) * cw)
        acc = b_ref[:, cols] + w_ref[0:1, cols] * ext_ref[halo - pad:halo - pad + TM, cols]
        for k in range(1, SSM_CONV):
            acc = acc + w_ref[k:k + 1, cols] * ext_ref[halo - pad + k:halo - pad + k + TM, cols]
        act = _silu(acc)
        if j * cw < d_inner:
            xt_ref[0, j * cw:(j + 1) * cw, :] = act.T.astype(xt_ref.dtype)
        else:
            bc_ref[0, :, j * cw - d_inner:(j + 1) * cw - d_inner] = act.astype(bc_ref.dtype)


def _ssm_conv(xbc, conv_w_p, conv_b, d_inner):
    B, N, C = xbc.shape
    n_tiles = N // TM
    hb = TM // CONV_HALO
    last_hb = N // CONV_HALO - 1
    return pl.pallas_call(
        functools.partial(_conv_kernel, n_tiles=n_tiles, d_inner=d_inner),
        grid=(B, n_tiles),
        in_specs=[pl.BlockSpec((1, TM, C), lambda b, t: (b, t, 0)),
                  pl.BlockSpec((1, CONV_HALO, C), lambda b, t: (b, jnp.maximum(t * hb - 1, 0), 0)),
                  pl.BlockSpec((1, CONV_HALO, C), lambda b, t: (b, jnp.minimum((t + 1) * hb, last_hb), 0)),
                  pl.BlockSpec(conv_w_p.shape, lambda b, t: (0, 0)),
                  pl.BlockSpec((1, C), lambda b, t: (0, 0))],
        out_specs=[pl.BlockSpec((1, d_inner, TM), lambda b, t: (b, 0, t)),
                   pl.BlockSpec((1, TM, C - d_inner), lambda b, t: (b, t, 0))],
        out_shape=[jax.ShapeDtypeStruct((B, d_inner, N), BF16), jax.ShapeDtypeStruct((B, N, C - d_inner), BF16)],
        scratch_shapes=[pltpu.VMEM((TM + 2 * CONV_HALO, C), F32)],
        compiler_params=_cparams("parallel", "parallel"),
        name="ssm_conv",
    )(xbc, xbc, xbc, conv_w_p, conv_b.reshape(1, C))


def _ssd_kernel(xt_ref, bc_ref, dt_ref, dtt_ref, alr_ref, alc_ref, dsk_ref, y_ref, state_ref, xw_ref, *,
                direction, n_heads):
    L = SSM_CHUNK
    P = SSM_HEAD_DIM
    NS = SSM_STATE
    hpg = n_heads // SSM_GROUPS
    gw = hpg * P
    reverse = direction == 1
    l0 = direction * n_heads

    @pl.when(pl.program_id(1) == 0)
    def _():
        state_ref[...] = jnp.zeros_like(state_ref)

    ri = lax.broadcasted_iota(jnp.int32, (L, L), 0)
    ci = lax.broadcasted_iota(jnp.int32, (L, L), 1)
    before = (ri >= ci) if reverse else (ri <= ci)
    after = (ci >= ri) if reverse else (ci <= ri)
    a = jnp.dot(after.astype(F32), dt_ref[0] * -jnp.exp(alr_ref[...]), precision=HIGHEST, preferred_element_type=F32)
    at = jnp.dot(dtt_ref[0] * -jnp.exp(alc_ref[...]), before.astype(F32), precision=HIGHEST,
                 preferred_element_type=F32)[l0:l0 + n_heads]
    last = 0 if reverse else L - 1
    tot = jnp.broadcast_to(at[:, last:last + 1], (n_heads, L))
    ea = jnp.exp(at)
    wgt = jnp.exp(tot - at)
    etot = jnp.exp(tot)
    dtt = dtt_ref[0][l0:l0 + n_heads]
    dsk = dsk_ref[direction]

    for g in range(SSM_GROUPS):
        bg = bc_ref[0][:, g * NS:(g + 1) * NS]
        cg = bc_ref[0][:, (SSM_GROUPS + g) * NS:(SSM_GROUPS + g + 1) * NS]
        cbt = lax.dot_general(bg, cg, (((1,), (1,)), ((), ())), preferred_element_type=F32)
        yoff = lax.dot_general(state_ref[g * gw:(g + 1) * gw, :].astype(BF16), cg, (((1,), (1,)), ((), ())),
                               preferred_element_type=F32)
        for hh in range(hpg):
            h = g * hpg + hh
            rows = slice(h * P, (h + 1) * P)
            decay = jnp.where(before, jnp.exp(at[h:h + 1, :] - a[:, l0 + h:l0 + h + 1]), 0.0)
            mt = (cbt * decay).astype(BF16)
            xh = xt_ref[0, rows, :].astype(F32)
            xdt = xh * dtt[h:h + 1, :]
            y = (jnp.dot(xdt.astype(BF16), mt, preferred_element_type=F32)
                 + yoff[hh * P:(hh + 1) * P, :] * ea[h:h + 1, :] + xh * dsk[h:h + 1, :])
            y_ref[0, rows, :] = y.astype(y_ref.dtype)
            xw_ref[hh * P:(hh + 1) * P, :] = (xdt * wgt[h:h + 1, :]).astype(xw_ref.dtype)
        upd = jnp.dot(xw_ref[...], bg, preferred_element_type=F32)
        for hh in range(hpg):
            h = g * hpg + hh
            rows = slice(h * P, (h + 1) * P)
            state_ref[rows, :] = state_ref[rows, :] * etot[h:h + 1, :] + upd[hh * P:(hh + 1) * P, :]


def _ssd_scan(xt, bc, dt, dtt, a_log, d_skip, direction, n_ctx):
    B, d_inner, N = xt.shape
    n_heads = a_log.shape[1]
    L = SSM_CHUNK
    nc = N // L
    ncc = n_ctx // L
    if direction == 0:
        chunk = lambda c: c
    else:
        chunk = lambda c: jnp.where(c < ncc, ncc - 1 - c, nc - 1 - (c - ncc))
    kern = functools.partial(_ssd_kernel, direction=direction, n_heads=n_heads)
    a_log_lanes = jnp.pad(a_log.reshape(-1), (0, LANES - a_log.size))
    d_skip_lanes = jnp.broadcast_to(d_skip.astype(F32)[:, :, None], d_skip.shape + (L,))
    return pl.pallas_call(
        kern,
        grid=(B, nc),
        in_specs=[pl.BlockSpec((1, d_inner, L), lambda b, c: (b, 0, chunk(c))),
                  pl.BlockSpec((1, L, bc.shape[2]), lambda b, c: (b, chunk(c), 0)),
                  pl.BlockSpec((1, L, LANES), lambda b, c: (b, chunk(c), 0)),
                  pl.BlockSpec((1, LANES, L), lambda b, c: (b, 0, chunk(c))),
                  pl.BlockSpec((1, LANES), lambda b, c: (0, 0)),
                  pl.BlockSpec((LANES, 1), lambda b, c: (0, 0)),
                  pl.BlockSpec(d_skip_lanes.shape, lambda b, c: (0, 0, 0))],
        out_specs=pl.BlockSpec((1, d_inner, L), lambda b, c: (b, 0, chunk(c))),
        out_shape=jax.ShapeDtypeStruct((B, d_inner, N), BF16),
        scratch_shapes=[pltpu.VMEM((d_inner, SSM_STATE), F32),
                        pltpu.VMEM((d_inner // SSM_GROUPS, L), BF16)],
        compiler_params=_cparams("parallel", "arbitrary"),
        name=f"ssd_scan_dir{direction}",
    )(xt, bc, dt, dtt, a_log_lanes.reshape(1, LANES), a_log_lanes.reshape(LANES, 1), d_skip_lanes)


def kernel(x, c, ctx, c_ctx, w_mod, b_mod, ln1_g, ln1_b, ln2_g, ln2_b, attn_w_qkv, attn_w_o, attn_lq1, attn_lk1, attn_lq2, attn_lk2, attn_subln_g, ssm_w_in, ssm_conv_w, ssm_conv_b, ssm_dt_bias, ssm_a_log, ssm_d, ssm_norm_g, ssm_w_out, moe_w_group, moe_b_group, moe_w_expert, moe_b_expert, moe_w_gate, moe_w_up, moe_w_down):
    B, S, D = x.shape
    C = ctx.shape[1]
    depth = w_mod.shape[0]
    E = moe_w_expert.shape[-1]
    assert C == TM and S % TM == 0 and B < SUBLANES
    alpha = (2 * depth) ** 0.25
    ctx_row = B

    cond = jnp.zeros((SUBLANES, D), F32).at[:B].set(c).at[B].set(c_ctx)
    mods = _modulation(cond, w_mod, b_mod)
    x_all = jnp.concatenate([ctx, x], axis=1)
    cos, sa, sb = _rope_tables(C, S)

    for i in range(depth):
        last = i == depth - 1
        j = i // N_MIXERS
        mod = mods[i].reshape(SUBLANES, 1, 6 * D)
        w_router = jnp.zeros((D, LANES), F32).at[:, :MOE_GROUPS].set(moe_w_group[i]) \
            .at[:, ROUTER_EXPERT_LANE:ROUTER_EXPERT_LANE + E].set(moe_w_expert[i])
        b_router = jnp.zeros((1, LANES), F32).at[0, :MOE_GROUPS].set(moe_b_group[i]) \
            .at[0, ROUTER_EXPERT_LANE:ROUTER_EXPERT_LANE + E].set(moe_b_expert[i])
        if i % N_MIXERS == 0:
            lambda_init = 0.8 - 0.6 * math.exp(-0.3 * i)
            qt, k, vt = _qkv_proj(x_all, mod, attn_w_qkv[j].astype(BF16), cos, sa, sb, ctx_row)
            lam_params = jnp.stack([attn_lq1[j], attn_lk1[j], attn_lq2[j], attn_lk2[j]])
            o = _diff_attention(qt, k, vt, lam_params, attn_subln_g[j], lambda_init, C)
            lead_args = [o]
            lead_specs = [pl.BlockSpec((1, TM, o.shape[-1]), lambda b, t: (b, t, 0))]
            x1, h2, route, counts = _proj_ln(_attn_out_kernel, lead_args, lead_specs, x_all, mod,
                                             attn_w_o[j].astype(BF16), ln1_g[i], ln1_b[i], w_router, b_router,
                                             ctx_row, alpha, E, "attn_out_ln")
        else:
            n_heads = ssm_a_log.shape[-1]
            d_inner = n_heads * SSM_HEAD_DIM
            d_conv = d_inner + 2 * SSM_GROUPS * SSM_STATE
            w_in = ssm_w_in[j]
            n_dt = w_in.shape[1] - d_inner - d_conv
            w_in_p = jnp.pad(w_in, ((0, 0), (0, LANES - n_dt))).astype(BF16)
            dt_bias_p = jnp.pad(ssm_dt_bias[j].reshape(1, n_dt), ((0, 0), (0, LANES - n_dt)))
            z, xbc, dt, dtt = _ssm_in_proj(x_all, mod, w_in_p, dt_bias_p, ctx_row, d_inner, d_conv)
            conv_w_p = jnp.pad(ssm_conv_w[j], ((0, SUBLANES - SSM_CONV), (0, 0)))
            xt, bc = _ssm_conv(xbc, conv_w_p, ssm_conv_b[j], d_inner)
            y_f = _ssd_scan(xt, bc, dt, dtt, ssm_a_log[j], ssm_d[j], 0, C)
            y_b = _ssd_scan(xt, bc, dt, dtt, ssm_a_log[j], ssm_d[j], 1, C)
            row = lambda b, t: (b, t, 0)
            col = lambda b, t: (b, 0, t)
            lead_args = [y_f, y_b, z, ssm_norm_g[j].reshape(1, d_inner)]
            lead_specs = [pl.BlockSpec((1, d_inner, TM), col), pl.BlockSpec((1, d_inner, TM), col),
                          pl.BlockSpec((1, TM, d_inner), row), pl.BlockSpec((1, d_inner), lambda b, t: (0, 0))]
            x1, h2, route, counts = _proj_ln(_ssm_out_kernel, lead_args, lead_specs, x_all, mod,
                                             ssm_w_out[j].astype(BF16), ln1_g[i], ln1_b[i], w_router, b_router,
                                             ctx_row, alpha, E, "ssm_out_ln")
        x_all = _moe_layer(x1, h2, route, counts, mod, moe_w_gate, moe_w_up, moe_w_down, i,
                           ln2_g[i], ln2_b[i], ctx_row, alpha, 1 if last else 0)
    return x_all
```

```python
import functools
import math

import jax
import jax.numpy as jnp
from jax import lax
from jax.experimental import pallas as pl
from jax.experimental.pallas import tpu as pltpu

F32 = jnp.float32
BF16 = jnp.bfloat16
HIGHEST = lax.Precision.HIGHEST

GRID_W = 64
DA_HEADS = 8
DA_HEAD_DIM = 64
DA_V_DIM = 2 * DA_HEAD_DIM
ROPE_THETA = 10000.0
SSM_HEAD_DIM = 64
SSM_GROUPS = 4
SSM_STATE = 128
SSM_CONV = 5
SSM_CHUNK = 128
MOE_GROUPS = 4
MOE_PER_GROUP = 8
MOE_TOP_K = 2
LN_EPS = 1e-5
RMS_EPS = 1e-5
N_MIXERS = 2

LANES = 128
SUBLANES = 8
TM = 256
MOE_BM = 256
ATTN_TK = 2816
ATTN_HEADS = 2
VT_ROWS = DA_V_DIM + 16
VMEM_LIMIT = 56 * 1024 * 1024


def _cparams(*sem):
    return pltpu.CompilerParams(dimension_semantics=sem, vmem_limit_bytes=VMEM_LIMIT)


def _silu(v):
    return v / (1.0 + jnp.exp(-v))


def _layer_norm(r, g, b):
    mu = jnp.mean(r, axis=-1, keepdims=True)
    d = r - mu
    var = jnp.mean(d * d, axis=-1, keepdims=True)
    return d * lax.rsqrt(var + LN_EPS) * g + b


def _mod_row(ctx_row):
    return lambda b, t: (jnp.where(t == 0, ctx_row, b), 0, 0)


def _mod_kernel(c_ref, w_ref, b_ref, o_ref):
    s = _silu(c_ref[...])
    o_ref[0] = jnp.dot(s, w_ref[0], precision=HIGHEST, preferred_element_type=F32) + b_ref[0]


def _modulation(cond, w_mod, b_mod):
    L, D, D6 = w_mod.shape
    R = cond.shape[0]
    tn = 1536
    return pl.pallas_call(
        _mod_kernel,
        grid=(L, D6 // tn),
        in_specs=[pl.BlockSpec((R, D), lambda l, j: (0, 0)),
                  pl.BlockSpec((1, D, tn), lambda l, j: (l, 0, j)),
                  pl.BlockSpec((1, 1, tn), lambda l, j: (l, 0, j))],
        out_specs=pl.BlockSpec((1, R, tn), lambda l, j: (l, 0, j)),
        out_shape=jax.ShapeDtypeStruct((L, R, D6), F32),
        compiler_params=_cparams("parallel", "parallel"),
        name="modulation",
    )(cond, w_mod, b_mod.reshape(L, 1, D6))


def _qkv_kernel(x_ref, mod_ref, w_ref, cos_ref, sa_ref, sb_ref, qt_ref, k_ref, vt_ref, *, d_model, q_scale):
    D = d_model
    H = k_ref.shape[0]
    x = x_ref[0]
    h = (x * (1.0 + mod_ref[0, :, D:2 * D]) + mod_ref[0, :, 0:D]).astype(BF16)
    cos = cos_ref[...]
    sa = sa_ref[...]
    sb = sb_ref[...]
    first_map = lax.broadcasted_iota(jnp.int32, (LANES, TM), 0) < DA_HEAD_DIM
    ones_rows = (lax.broadcasted_iota(jnp.int32, (VT_ROWS - DA_V_DIM, TM), 0) == 0).astype(BF16)

    def rope(a):
        return a * cos + pltpu.roll(a, LANES - 16, 1) * sa + pltpu.roll(a, 16, 1) * sb

    for jj in range(3 * H // 2):
        acc2 = jnp.dot(h, w_ref[:, jj * 2 * LANES:(jj + 1) * 2 * LANES], preferred_element_type=F32)
        for half in range(2):
            j = 2 * jj + half
            acc = acc2[:, half * LANES:(half + 1) * LANES]
            hd = j % H
            if j < H:
                qt = (rope(acc) * q_scale).T
                qt_ref[hd, 0, 0] = jnp.where(first_map, qt, 0.0).astype(BF16)
                qt_ref[hd, 0, 1] = jnp.where(first_map, 0.0, qt).astype(BF16)
            elif j < 2 * H:
                k_ref[hd, 0] = rope(acc).astype(BF16)
            else:
                vt_ref[hd, 0, 0:DA_V_DIM, :] = acc.T.astype(BF16)
                vt_ref[hd, 0, DA_V_DIM:VT_ROWS, :] = ones_rows


def _qkv_proj(x_all, mod, w_qkv, cos, sa, sb, ctx_row):
    B, N, D = x_all.shape
    H = DA_HEADS
    kern = functools.partial(_qkv_kernel, d_model=D, q_scale=DA_HEAD_DIM ** -0.5 * math.log2(math.e))
    return pl.pallas_call(
        kern,
        grid=(B, N // TM),
        in_specs=[pl.BlockSpec((1, TM, D), lambda b, t: (b, t, 0)),
                  pl.BlockSpec((1, 1, 6 * D), _mod_row(ctx_row)),
                  pl.BlockSpec((D, w_qkv.shape[1]), lambda b, t: (0, 0)),
                  pl.BlockSpec((TM, LANES), lambda b, t: (t, 0)),
                  pl.BlockSpec((TM, LANES), lambda b, t: (t, 0)),
                  pl.BlockSpec((TM, LANES), lambda b, t: (t, 0))],
        out_specs=[pl.BlockSpec((H, 1, 2, LANES, TM), lambda b, t: (0, b, 0, 0, t)),
                   pl.BlockSpec((H, 1, TM, LANES), lambda b, t: (0, b, t, 0)),
                   pl.BlockSpec((H, 1, VT_ROWS, TM), lambda b, t: (0, b, 0, t))],
        out_shape=[jax.ShapeDtypeStruct((H, B, 2, LANES, N), BF16),
                   jax.ShapeDtypeStruct((H, B, N, LANES), BF16),
                   jax.ShapeDtypeStruct((H, B, VT_ROWS, N), BF16)],
        compiler_params=_cparams("parallel", "parallel"),
        name="qkv_proj",
    )(x_all, mod, w_qkv, cos, sa, sb)


def _rope_tables(n_ctx, n_lat):
    t = jnp.arange(n_lat)
    pos = jnp.stack([t // GRID_W, t % GRID_W], axis=1).astype(F32)
    axis_dims = DA_HEAD_DIM // 2
    inv = ROPE_THETA ** (-jnp.arange(0, axis_dims, 2, dtype=F32) / axis_dims)
    lane = jnp.arange(LANES)
    d = lane % DA_HEAD_DIM
    axis = d // axis_dims
    second = (d % axis_dims) // (axis_dims // 2)
    ang = pos[:, axis] * inv[d % (axis_dims // 2)][None, :]
    cos = jnp.cos(ang)
    sin = jnp.sin(ang)
    sa = jnp.where(second[None, :] == 0, -sin, 0.0)
    sb = jnp.where(second[None, :] == 1, sin, 0.0)
    pad = lambda a, v: jnp.concatenate([jnp.full((n_ctx, LANES), v, F32), a], axis=0)
    return pad(cos, 1.0), pad(sa, 0.0), pad(sb, 0.0)


def _attn_tiles(qt_ref, k_ref, vt_ref, bufs, n_chunks, tk):
    tq = qt_ref.shape[-1]
    heads = range(qt_ref.shape[0])

    def scores(h, off, slot):
        k = k_ref[h, 0, pl.ds(off, tk), :]
        for j in range(2):
            bufs[h][slot][j, 0:tk, :] = jnp.dot(k, qt_ref[h, 0, j], preferred_element_type=F32)

    def softmax_pv(h, off, slot, carry):
        vt = vt_ref[h, 0, :, pl.ds(off, tk)]
        new = []
        for j in range(2):
            m, acc = carry[j]
            s = bufs[h][slot][j, 0:tk, :]
            m_new = jnp.maximum(m, jnp.max(s, axis=0, keepdims=True))
            p = jnp.exp2(s - m_new).astype(BF16)
            acc = jnp.exp2(m - m_new) * acc + jnp.dot(vt, p, preferred_element_type=F32)
            new.append((m_new, acc))
        return tuple(new)

    carries = [tuple((jnp.full((1, tq), -jnp.inf, F32), jnp.zeros((VT_ROWS, tq), F32)) for _ in range(2))
               for _ in heads]
    for h in heads:
        scores(h, 0, 0)
    for c in range(n_chunks):
        for h in heads:
            if c + 1 < n_chunks:
                scores(h, (c + 1) * tk, (c + 1) % 2)
            carries[h] = softmax_pv(h, c * tk, c % 2, carries[h])
    return [tuple(acc[0:DA_V_DIM] / acc[DA_V_DIM:DA_V_DIM + 1] for _, acc in carries[h]) for h in heads]


def _attn_kernel(lam_ref, g_ref, qt_ref, k_ref, vt_ref, o_ref, *s_refs, n_ctx, n_all, tk, lambda_init):
    t = pl.program_id(2)
    bufs = [(s_refs[2 * h], s_refs[2 * h + 1]) for h in range(qt_ref.shape[0])]
    lp = lam_ref[...]
    lam = (jnp.exp(jnp.sum(lp[0:1] * lp[1:2], axis=-1, keepdims=True))
           - jnp.exp(jnp.sum(lp[2:3] * lp[3:4], axis=-1, keepdims=True)) + lambda_init)

    def finish(outs):
        for h, (o0, o1) in enumerate(outs):
            o = o0 - lam * o1
            ms = jnp.mean(o * o, axis=0, keepdims=True)
            o = o * lax.rsqrt(ms + RMS_EPS) * (g_ref[...] * (1.0 - lambda_init))
            o_ref[0, :, h * DA_V_DIM:(h + 1) * DA_V_DIM] = o.T.astype(o_ref.dtype)

    @pl.when(t == 0)
    def _():
        finish(_attn_tiles(qt_ref, k_ref, vt_ref, bufs, 1, n_ctx))

    @pl.when(t > 0)
    def _():
        finish(_attn_tiles(qt_ref, k_ref, vt_ref, bufs, n_all // tk, tk))


def _diff_attention(qt, k, vt, lam_params, subln_g, lambda_init, n_ctx):
    H, B, N, _ = k.shape
    tk = ATTN_TK
    nh = ATTN_HEADS
    assert n_ctx == TM and N % tk == 0 and n_ctx <= tk and H % nh == 0
    kern = functools.partial(_attn_kernel, n_ctx=n_ctx, n_all=N, tk=tk, lambda_init=lambda_init)
    return pl.pallas_call(
        kern,
        grid=(B, H // nh, N // TM),
        in_specs=[pl.BlockSpec(lam_params.shape, lambda b, h, t: (0, 0)),
                  pl.BlockSpec((DA_V_DIM, 1), lambda b, h, t: (0, 0)),
                  pl.BlockSpec((nh, 1, 2, LANES, TM), lambda b, h, t: (h, b, 0, 0, t)),
                  pl.BlockSpec((nh, 1, N, LANES), lambda b, h, t: (h, b, 0, 0)),
                  pl.BlockSpec((nh, 1, VT_ROWS, N), lambda b, h, t: (h, b, 0, 0))],
        out_specs=pl.BlockSpec((1, TM, nh * DA_V_DIM), lambda b, h, t: (b, t, h)),
        out_shape=jax.ShapeDtypeStruct((B, N, H * DA_V_DIM), BF16),
        scratch_shapes=[pltpu.VMEM((2, tk, TM), F32) for _ in range(2 * nh)],
        compiler_params=_cparams("parallel", "parallel", "parallel"),
        name="diff_attention",
    )(lam_params, subln_g.reshape(DA_V_DIM, 1), qt, k, vt)


ROUTE_ID, ROUTE_GATE, ROUTE_RANK = 0, MOE_TOP_K, 2 * MOE_TOP_K
ROUTE_ROWS = SUBLANES
ROUTER_EXPERT_LANE = SUBLANES


def _route_tile(lg, cnt_ref, n_experts):
    G, PER = MOE_GROUPS, MOE_PER_GROUP
    tm = lg.shape[0]
    lt = lg.T
    row = lax.broadcasted_iota(jnp.int32, (SUBLANES, tm), 0)
    ninf = -jnp.inf
    first = lambda hit: jnp.min(jnp.where(hit, row, SUBLANES), axis=0, keepdims=True)
    gl = jnp.where(row < G, lt[0:SUBLANES], ninf)
    gmax = jnp.max(gl, axis=0, keepdims=True)
    g_top = 1.0 / jnp.sum(jnp.exp(gl - gmax), axis=0, keepdims=True)
    g_idx = first(gl == gmax)
    el = lt[ROUTER_EXPERT_LANE:ROUTER_EXPERT_LANE + PER]
    for g in range(1, G):
        el = jnp.where(g_idx == g, lt[ROUTER_EXPERT_LANE + g * PER:ROUTER_EXPERT_LANE + (g + 1) * PER], el)
    e1 = jnp.max(el, axis=0, keepdims=True)
    i1 = first(el == e1)
    el2 = jnp.where(row == i1, ninf, el)
    e2 = jnp.max(el2, axis=0, keepdims=True)
    i2 = first(el2 == e2)
    id1 = g_idx * PER + i1
    id2 = g_idx * PER + i2
    w2 = jnp.exp(e2 - e1)
    den = 1.0 + w2
    erow = lax.broadcasted_iota(jnp.int32, (n_experts, tm), 0)
    hit1 = erow == id1
    hit2 = erow == id2
    onehot = jnp.logical_or(hit1, hit2)
    ui = lax.broadcasted_iota(jnp.int32, (tm, tm), 0)
    ti = lax.broadcasted_iota(jnp.int32, (tm, tm), 1)
    earlier = jnp.dot(onehot.astype(BF16), (ui < ti).astype(BF16), preferred_element_type=F32) + cnt_ref[...]
    r1 = jnp.sum(jnp.where(hit1, earlier, 0.0), axis=0, keepdims=True)
    r2 = jnp.sum(jnp.where(hit2, earlier, 0.0), axis=0, keepdims=True)
    cnt_ref[...] = cnt_ref[...] + jnp.sum(onehot.astype(F32), axis=1, keepdims=True)
    rec = jnp.zeros((ROUTE_ROWS, tm), F32)
    for r, val in ((ROUTE_ID, id1.astype(F32)), (ROUTE_ID + 1, id2.astype(F32)),
                   (ROUTE_GATE, g_top / den), (ROUTE_GATE + 1, g_top * w2 / den),
                   (ROUTE_RANK, r1), (ROUTE_RANK + 1, r2)):
        rec = jnp.where(row == r, val, rec)
    return rec


def _proj_ln_tail(a, x_ref, mod_ref, w_ref, lng_ref, lnb_ref, wr_ref, br_ref, x_out, h_out, rt_out, cnt_out, *,
                  d_model, alpha, n_experts):
    D = d_model

    @pl.when(jnp.logical_and(pl.program_id(0) == 0, pl.program_id(1) == 0))
    def _():
        cnt_out[...] = jnp.zeros_like(cnt_out)

    hb = a.shape[0] // 2
    os_ = [jnp.dot(a[r * hb:(r + 1) * hb], w_ref[...], preferred_element_type=F32) for r in range(2)]
    lgs = []
    for r in range(2):
        rows = slice(r * hb, (r + 1) * hb)
        res = alpha * x_ref[0, rows, :] + mod_ref[0, :, 2 * D:3 * D] * os_[r]
        xn = _layer_norm(res, lng_ref[...], lnb_ref[...])
        x_out[0, rows, :] = xn
        h2 = xn * (1.0 + mod_ref[0, :, 4 * D:5 * D]) + mod_ref[0, :, 3 * D:4 * D]
        h_out[0, rows, :] = h2.astype(h_out.dtype)
        lgs.append(jnp.dot(h2, wr_ref[...], precision=HIGHEST, preferred_element_type=F32) + br_ref[...])
    lg = jnp.concatenate(lgs, axis=0)
    rt_out[0] = _route_tile(lg, cnt_out, n_experts)


def _attn_out_kernel(a_ref, *refs, **kw):
    _proj_ln_tail(a_ref[0], *refs, **kw)


def _ssm_out_kernel(yf_ref, yb_ref, z_ref, ng_ref, *refs, **kw):
    y = (yf_ref[0].astype(F32) + yb_ref[0].astype(F32)).T
    gz = y * _silu(z_ref[0].astype(F32))
    ms = jnp.mean(gz * gz, axis=-1, keepdims=True)
    a = (gz * lax.rsqrt(ms + RMS_EPS) * ng_ref[...]).astype(BF16)
    _proj_ln_tail(a, *refs, **kw)


def _proj_ln(kernel_fn, lead_args, lead_specs, x_all, mod, w, ln_g, ln_b, w_router, b_router, ctx_row, alpha,
             n_experts, name):
    B, N, D = x_all.shape
    row = lambda b, t: (b, t, 0)
    const2 = lambda b, t: (0, 0)
    kern = functools.partial(kernel_fn, d_model=D, alpha=alpha, n_experts=n_experts)
    return pl.pallas_call(
        kern,
        grid=(B, N // TM),
        in_specs=lead_specs + [
            pl.BlockSpec((1, TM, D), row),
            pl.BlockSpec((1, 1, 6 * D), _mod_row(ctx_row)),
            pl.BlockSpec(w.shape, const2),
            pl.BlockSpec((1, D), const2),
            pl.BlockSpec((1, D), const2),
            pl.BlockSpec(w_router.shape, const2),
            pl.BlockSpec((1, LANES), const2)],
        out_specs=[pl.BlockSpec((1, TM, D), row), pl.BlockSpec((1, TM, D), row),
                   pl.BlockSpec((1, ROUTE_ROWS, TM), lambda b, t: (b, 0, t)), pl.BlockSpec((n_experts, 1), const2)],
        out_shape=[jax.ShapeDtypeStruct((B, N, D), F32), jax.ShapeDtypeStruct((B, N, D), BF16),
                   jax.ShapeDtypeStruct((B, ROUTE_ROWS, N), F32), jax.ShapeDtypeStruct((n_experts, 1), F32)],
        compiler_params=_cparams("arbitrary", "arbitrary"),
        name=name,
    )(*lead_args, x_all, mod, w, ln_g.reshape(1, D), ln_b.reshape(1, D), w_router, b_router)


def _moe_kernel(be_ref, nu_ref, x_ref, wg_ref, wu_ref, wd_ref, o_ref, wgu_bf, wd_bf, *, d_ff):
    i = pl.program_id(0)
    active = i < nu_ref[0]
    new_expert = jnp.logical_or(i == 0, be_ref[i] != be_ref[jnp.maximum(i - 1, 0)])

    @pl.when(jnp.logical_and(active, new_expert))
    def _():
        wgu_bf[:, 0:d_ff] = wg_ref[0, 0].astype(BF16)
        wgu_bf[:, d_ff:] = wu_ref[0, 0].astype(BF16)
        wd_bf[...] = wd_ref[0, 0].astype(BF16)

    @pl.when(active)
    def _():
        hb = x_ref.shape[0] // 2
        gus = [jnp.dot(x_ref[r * hb:(r + 1) * hb, :], wgu_bf[...], preferred_element_type=F32) for r in range(2)]
        for r in range(2):
            mid = (_silu(gus[r][:, :d_ff]) * gus[r][:, d_ff:]).astype(BF16)
            o_ref[r * hb:(r + 1) * hb, :] = jnp.dot(mid, wd_bf[...], preferred_element_type=F32).astype(o_ref.dtype)

    @pl.when(jnp.logical_not(active))
    def _():
        o_ref[...] = jnp.zeros_like(o_ref)


def _moe_experts(buf, blk_expert, n_used, w_gate, w_up, w_down, layer):
    n_rows, D = buf.shape
    d_ff = w_down.shape[2]
    grid_spec = pltpu.PrefetchScalarGridSpec(
        num_scalar_prefetch=2,
        grid=(n_rows // MOE_BM,),
        in_specs=[pl.BlockSpec((MOE_BM, D), lambda i, be, nu: (i, 0)),
                  pl.BlockSpec((1, 1, D, d_ff), lambda i, be, nu: (layer, be[i], 0, 0)),
                  pl.BlockSpec((1, 1, D, d_ff), lambda i, be, nu: (layer, be[i], 0, 0)),
                  pl.BlockSpec((1, 1, d_ff, D), lambda i, be, nu: (layer, be[i], 0, 0))],
        out_specs=pl.BlockSpec((MOE_BM, D), lambda i, be, nu: (i, 0)),
        scratch_shapes=[pltpu.VMEM((D, 2 * d_ff), BF16), pltpu.VMEM((d_ff, D), BF16)])
    return pl.pallas_call(
        functools.partial(_moe_kernel, d_ff=d_ff),
        grid_spec=grid_spec,
        out_shape=jax.ShapeDtypeStruct((n_rows, D), BF16),
        compiler_params=_cparams("arbitrary"),
        name="moe_experts",
    )(blk_expert, n_used, buf, w_gate, w_up, w_down)


def _dispatch(eid, rank, sizes):
    K, T = eid.shape
    A = T * K
    E = sizes.shape[0]
    padded = (sizes + MOE_BM - 1) // MOE_BM * MOE_BM
    pad_end = jnp.cumsum(padded)
    pad_start = pad_end - padded
    starts = jnp.cumsum(sizes) - sizes
    hit = eid[:, :, None] == jnp.arange(E, dtype=jnp.int32)
    dest = jnp.sum(jnp.where(hit, pad_start, 0), axis=-1) + rank
    n_blocks = -(-A // MOE_BM) + E
    blk_start = jnp.arange(n_blocks, dtype=jnp.int32) * MOE_BM
    blk_expert = jnp.minimum(jnp.sum(blk_start[:, None] >= pad_end[None, :], axis=1), E - 1).astype(jnp.int32)
    n_used = (pad_end[-1] // MOE_BM).astype(jnp.int32).reshape(1)
    order = jnp.argsort(eid.T.reshape(A))
    within = jnp.arange(MOE_BM, dtype=jnp.int32)[None, :] + (blk_start - pad_start[blk_expert])[:, None]
    src = jnp.clip(starts[blk_expert][:, None] + within, 0, A - 1)
    filler = (blk_start[:, None] + jnp.arange(MOE_BM, dtype=jnp.int32)[None, :]) % T
    slot_tok = jnp.where(within < sizes[blk_expert][:, None], jnp.take(order, src, mode='clip') // K, filler)
    return slot_tok.reshape(-1).astype(jnp.int32), blk_expert, n_used, dest.astype(jnp.int32)


def _combine_kernel(x_ref, y0_ref, y1_ref, rt_ref, mod_ref, lng_ref, lnb_ref, o_ref, *, d_model, alpha):
    D = d_model
    rt = rt_ref[0].T
    f = (rt[:, ROUTE_GATE:ROUTE_GATE + 1] * y0_ref[0, 0].astype(F32)
         + rt[:, ROUTE_GATE + 1:ROUTE_GATE + 2] * y1_ref[0, 0].astype(F32))
    r = alpha * x_ref[0] + mod_ref[0, :, 5 * D:6 * D] * f
    o_ref[0] = _layer_norm(r, lng_ref[...], lnb_ref[...])


def _combine_ln(x_all, ys, route, mod, ln_g, ln_b, ctx_row, alpha, t0):
    B, N, D = x_all.shape
    nt = N // TM - t0
    tiles_per_batch = N // TM
    row = lambda b, t: (b, t + t0, 0)
    const2 = lambda b, t: (0, 0)
    mrow = _mod_row(ctx_row)
    return pl.pallas_call(
        functools.partial(_combine_kernel, d_model=D, alpha=alpha),
        grid=(B, nt),
        in_specs=[pl.BlockSpec((1, TM, D), row),
                  pl.BlockSpec((1, 1, TM, D), lambda b, t: (0, b * tiles_per_batch + t + t0, 0, 0)),
                  pl.BlockSpec((1, 1, TM, D), lambda b, t: (1, b * tiles_per_batch + t + t0, 0, 0)),
                  pl.BlockSpec((1, ROUTE_ROWS, TM), lambda b, t: (b, 0, t + t0)),
                  pl.BlockSpec((1, 1, 6 * D), lambda b, t: mrow(b, t + t0)),
                  pl.BlockSpec((1, D), const2),
                  pl.BlockSpec((1, D), const2)],
        out_specs=pl.BlockSpec((1, TM, D), lambda b, t: (b, t, 0)),
        out_shape=jax.ShapeDtypeStruct((B, nt * TM, D), F32),
        compiler_params=_cparams("parallel", "parallel"),
        name="moe_combine_ln",
    )(x_all, ys, ys, route, mod, ln_g.reshape(1, D), ln_b.reshape(1, D))


def _moe_layer(x1, h2, route, counts, mod, w_gate, w_up, w_down, layer, ln_g, ln_b, ctx_row, alpha, t0):
    B, N, D = x1.shape
    T = B * N
    by_row = lambda r: jnp.swapaxes(route[:, r:r + MOE_TOP_K, :], 0, 1).reshape(MOE_TOP_K, T).astype(jnp.int32)
    eid = by_row(ROUTE_ID)
    rank = by_row(ROUTE_RANK)
    sizes = counts[:, 0].astype(jnp.int32)
    slot_tok, blk_expert, n_used, dest = _dispatch(eid, rank, sizes)
    buf = jnp.take(h2.reshape(T, D), slot_tok, axis=0, mode='clip')
    yb = _moe_experts(buf, blk_expert, n_used, w_gate, w_up, w_down, layer)
    ys = jnp.take(yb, dest, axis=0, mode='clip').reshape(MOE_TOP_K, T // TM, TM, D)
    return _combine_ln(x1, ys, route, mod, ln_g, ln_b, ctx_row, alpha, t0)


def _softplus(v):
    return jnp.maximum(v, 0.0) + jnp.log1p(jnp.exp(-jnp.abs(v)))


def _ssm_in_kernel(x_ref, mod_ref, w_ref, dtb_ref, z_out, xbc_out, dt_out, dtt_out, *, d_model, d_inner, d_conv):
    D = d_model
    h = (x_ref[0] * (1.0 + mod_ref[0, :, D:2 * D]) + mod_ref[0, :, 0:D]).astype(BF16)
    z_out[0] = jnp.dot(h, w_ref[:, 0:d_inner], preferred_element_type=F32).astype(z_out.dtype)
    cw = 512
    for j in range(d_conv // cw):
        xbc_out[0, :, j * cw:(j + 1) * cw] = jnp.dot(
            h, w_ref[:, d_inner + j * cw:d_inner + (j + 1) * cw], preferred_element_type=F32).astype(xbc_out.dtype)
    dt = jnp.dot(h, w_ref[:, d_inner + d_conv:], preferred_element_type=F32) + dtb_ref[...]
    dt = _softplus(dt)
    dt_out[0] = dt
    dtt_out[0] = dt.T


def _ssm_in_proj(x_all, mod, w_in_p, dt_bias_p, ctx_row, d_inner, d_conv):
    B, N, D = x_all.shape
    row = lambda b, t: (b, t, 0)
    kern = functools.partial(_ssm_in_kernel, d_model=D, d_inner=d_inner, d_conv=d_conv)
    return pl.pallas_call(
        kern,
        grid=(B, N // TM),
        in_specs=[pl.BlockSpec((1, TM, D), row),
                  pl.BlockSpec((1, 1, 6 * D), _mod_row(ctx_row)),
                  pl.BlockSpec(w_in_p.shape, lambda b, t: (0, 0)),
                  pl.BlockSpec((1, LANES), lambda b, t: (0, 0))],
        out_specs=[pl.BlockSpec((1, TM, d_inner), row), pl.BlockSpec((1, TM, d_conv), row),
                   pl.BlockSpec((1, TM, LANES), row), pl.BlockSpec((1, LANES, TM), lambda b, t: (b, 0, t))],
        out_shape=[jax.ShapeDtypeStruct((B, N, d_inner), BF16), jax.ShapeDtypeStruct((B, N, d_conv), BF16),
                   jax.ShapeDtypeStruct((B, N, LANES), F32), jax.ShapeDtypeStruct((B, LANES, N), F32)],
        compiler_params=_cparams("parallel", "parallel"),
        name="ssm_in_proj",
    )(x_all, mod, w_in_p, dt_bias_p)


CONV_HALO = 16


def _conv_kernel(x_ref, prev_ref, next_ref, w_ref, b_ref, xt_ref, bc_ref, ext_ref, *, n_tiles, d_inner):
    t = pl.program_id(1)
    halo = CONV_HALO
    pad = SSM_CONV // 2
    has_prev = t >= 2
    has_next = jnp.logical_and(t >= 1, t < n_tiles - 1)
    ext_ref[0:halo, :] = jnp.where(has_prev, prev_ref[0].astype(F32), 0.0)
    ext_ref[halo:halo + TM, :] = x_ref[0].astype(F32)
    ext_ref[halo + TM:, :] = jnp.where(has_next, next_ref[0].astype(F32), 0.0)
    cw = 512
    for j in range(ext_ref.shape[1] // cw):
        cols = slice(j * cw, (j + 1) * cw)
        acc = b_ref[:, cols] + w_ref[0:1, cols] * ext_ref[halo - pad:halo - pad + TM, cols]
        for k in range(1, SSM_CONV):
            acc = acc + w_ref[k:k + 1, cols] * ext_ref[halo - pad + k:halo - pad + k + TM, cols]
        act = _silu(acc)
        if j * cw < d_inner:
            xt_ref[0, j * cw:(j + 1) * cw, :] = act.T.astype(xt_ref.dtype)
        else:
            bc_ref[0, :, j * cw - d_inner:(j + 1) * cw - d_inner] = act.astype(bc_ref.dtype)


def _ssm_conv(xbc, conv_w_p, conv_b, d_inner):
    B, N, C = xbc.shape
    n_tiles = N // TM
    hb = TM // CONV_HALO
    last_hb = N // CONV_HALO - 1
    return pl.pallas_call(
        functools.partial(_conv_kernel, n_tiles=n_tiles, d_inner=d_inner),
        grid=(B, n_tiles),
        in_specs=[pl.BlockSpec((1, TM, C), lambda b, t: (b, t, 0)),
                  pl.BlockSpec((1, CONV_HALO, C), lambda b, t: (b, jnp.maximum(t * hb - 1, 0), 0)),
                  pl.BlockSpec((1, CONV_HALO, C), lambda b, t: (b, jnp.minimum((t + 1) * hb, last_hb), 0)),
                  pl.BlockSpec(conv_w_p.shape, lambda b, t: (0, 0)),
                  pl.BlockSpec((1, C), lambda b, t: (0, 0))],
        out_specs=[pl.BlockSpec((1, d_inner, TM), lambda b, t: (b, 0, t)),
                   pl.BlockSpec((1, TM, C - d_inner), lambda b, t: (b, t, 0))],
        out_shape=[jax.ShapeDtypeStruct((B, d_inner, N), BF16), jax.ShapeDtypeStruct((B, N, C - d_inner), BF16)],
        scratch_shapes=[pltpu.VMEM((TM + 2 * CONV_HALO, C), F32)],
        compiler_params=_cparams("parallel", "parallel"),
        name="ssm_conv",
    )(xbc, xbc, xbc, conv_w_p, conv_b.reshape(1, C))


def _ssd_kernel(xt_ref, bc_ref, dt_ref, dtt_ref, alr_ref, alc_ref, dsk_ref, y_ref, state_ref, xw_ref, *,
                direction, n_heads):
    L = SSM_CHUNK
    P = SSM_HEAD_DIM
    NS = SSM_STATE
    hpg = n_heads // SSM_GROUPS
    gw = hpg * P
    reverse = direction == 1
    l0 = direction * n_heads

    @pl.when(pl.program_id(1) == 0)
    def _():
        state_ref[...] = jnp.zeros_like(state_ref)

    ri = lax.broadcasted_iota(jnp.int32, (L, L), 0)
    ci = lax.broadcasted_iota(jnp.int32, (L, L), 1)
    before = (ri >= ci) if reverse else (ri <= ci)
    after = (ci >= ri) if reverse else (ci <= ri)
    a = jnp.dot(after.astype(F32), dt_ref[0] * -jnp.exp(alr_ref[...]), precision=HIGHEST, preferred_element_type=F32)
    at = jnp.dot(dtt_ref[0] * -jnp.exp(alc_ref[...]), before.astype(F32), precision=HIGHEST,
                 preferred_element_type=F32)[l0:l0 + n_heads]
    last = 0 if reverse else L - 1
    tot = jnp.broadcast_to(at[:, last:last + 1], (n_heads, L))
    ea = jnp.exp(at)
    wgt = jnp.exp(tot - at)
    etot = jnp.exp(tot)
    dtt = dtt_ref[0][l0:l0 + n_heads]
    dsk = dsk_ref[direction]

    for g in range(SSM_GROUPS):
        bg = bc_ref[0][:, g * NS:(g + 1) * NS]
        cg = bc_ref[0][:, (SSM_GROUPS + g) * NS:(SSM_GROUPS + g + 1) * NS]
        cbt = lax.dot_general(bg, cg, (((1,), (1,)), ((), ())), preferred_element_type=F32)
        yoff = lax.dot_general(state_ref[g * gw:(g + 1) * gw, :].astype(BF16), cg, (((1,), (1,)), ((), ())),
                               preferred_element_type=F32)
        for hh in range(hpg):
            h = g * hpg + hh
            rows = slice(h * P, (h + 1) * P)
            decay = jnp.where(before, jnp.exp(at[h:h + 1, :] - a[:, l0 + h:l0 + h + 1]), 0.0)
            mt = (cbt * decay).astype(BF16)
            xh = xt_ref[0, rows, :].astype(F32)
            xdt = xh * dtt[h:h + 1, :]
            y = (jnp.dot(xdt.astype(BF16), mt, preferred_element_type=F32)
                 + yoff[hh * P:(hh + 1) * P, :] * ea[h:h + 1, :] + xh * dsk[h:h + 1, :])
            y_ref[0, rows, :] = y.astype(y_ref.dtype)
            xw_ref[hh * P:(hh + 1) * P, :] = (xdt * wgt[h:h + 1, :]).astype(xw_ref.dtype)
        upd = jnp.dot(xw_ref[...], bg, preferred_element_type=F32)
        for hh in range(hpg):
            h = g * hpg + hh
            rows = slice(h * P, (h + 1) * P)
            state_ref[rows, :] = state_ref[rows, :] * etot[h:h + 1, :] + upd[hh * P:(hh + 1) * P, :]


def _ssd_scan(xt, bc, dt, dtt, a_log, d_skip, direction, n_ctx):
    B, d_inner, N = xt.shape
    n_heads = a_log.shape[1]
    L = SSM_CHUNK
    nc = N // L
    ncc = n_ctx // L
    if direction == 0:
        chunk = lambda c: c
    else:
        chunk = lambda c: jnp.where(c < ncc, ncc - 1 - c, nc - 1 - (c - ncc))
    kern = functools.partial(_ssd_kernel, direction=direction, n_heads=n_heads)
    a_log_lanes = jnp.pad(a_log.reshape(-1), (0, LANES - a_log.size))
    d_skip_lanes = jnp.broadcast_to(d_skip.astype(F32)[:, :, None], d_skip.shape + (L,))
    return pl.pallas_call(
        kern,
        grid=(B, nc),
        in_specs=[pl.BlockSpec((1, d_inner, L), lambda b, c: (b, 0, chunk(c))),
                  pl.BlockSpec((1, L, bc.shape[2]), lambda b, c: (b, chunk(c), 0)),
                  pl.BlockSpec((1, L, LANES), lambda b, c: (b, chunk(c), 0)),
                  pl.BlockSpec((1, LANES, L), lambda b, c: (b, 0, chunk(c))),
                  pl.BlockSpec((1, LANES), lambda b, c: (0, 0)),
                  pl.BlockSpec((LANES, 1), lambda b, c: (0, 0)),
                  pl.BlockSpec(d_skip_lanes.shape, lambda b, c: (0, 0, 0))],
        out_specs=pl.BlockSpec((1, d_inner, L), lambda b, c: (b, 0, chunk(c))),
        out_shape=jax.ShapeDtypeStruct((B, d_inner, N), BF16),
        scratch_shapes=[pltpu.VMEM((d_inner, SSM_STATE), F32),
                        pltpu.VMEM((d_inner // SSM_GROUPS, L), BF16)],
        compiler_params=_cparams("parallel", "arbitrary"),
        name=f"ssd_scan_dir{direction}",
    )(xt, bc, dt, dtt, a_log_lanes.reshape(1, LANES), a_log_lanes.reshape(LANES, 1), d_skip_lanes)


def kernel(x, c, ctx, c_ctx, w_mod, b_mod, ln1_g, ln1_b, ln2_g, ln2_b, attn_w_qkv, attn_w_o, attn_lq1, attn_lk1, attn_lq2, attn_lk2, attn_subln_g, ssm_w_in, ssm_conv_w, ssm_conv_b, ssm_dt_bias, ssm_a_log, ssm_d, ssm_norm_g, ssm_w_out, moe_w_group, moe_b_group, moe_w_expert, moe_b_expert, moe_w_gate, moe_w_up, moe_w_down):
    B, S, D = x.shape
    C = ctx.shape[1]
    depth = w_mod.shape[0]
    E = moe_w_expert.shape[-1]
    assert C == TM and S % TM == 0 and B < SUBLANES
    alpha = (2 * depth) ** 0.25
    ctx_row = B

    cond = jnp.zeros((SUBLANES, D), F32).at[:B].set(c).at[B].set(c_ctx)
    mods = _modulation(cond, w_mod, b_mod)
    x_all = jnp.concatenate([ctx, x], axis=1)
    cos, sa, sb = _rope_tables(C, S)

    for i in range(depth):
        last = i == depth - 1
        j = i // N_MIXERS
        mod = mods[i].reshape(SUBLANES, 1, 6 * D)
        w_router = jnp.zeros((D, LANES), F32).at[:, :MOE_GROUPS].set(moe_w_group[i]) \
            .at[:, ROUTER_EXPERT_LANE:ROUTER_EXPERT_LANE + E].set(moe_w_expert[i])
        b_router = jnp.zeros((1, LANES), F32).at[0, :MOE_GROUPS].set(moe_b_group[i]) \
            .at[0, ROUTER_EXPERT_LANE:ROUTER_EXPERT_LANE + E].set(moe_b_expert[i])
        if i % N_MIXERS == 0:
            lambda_init = 0.8 - 0.6 * math.exp(-0.3 * i)
            qt, k, vt = _qkv_proj(x_all, mod, attn_w_qkv[j].astype(BF16), cos, sa, sb, ctx_row)
            lam_params = jnp.stack([attn_lq1[j], attn_lk1[j], attn_lq2[j], attn_lk2[j]])
            o = _diff_attention(qt, k, vt, lam_params, attn_subln_g[j], lambda_init, C)
            lead_args = [o]
            lead_specs = [pl.BlockSpec((1, TM, o.shape[-1]), lambda b, t: (b, t, 0))]
            x1, h2, route, counts = _proj_ln(_attn_out_kernel, lead_args, lead_specs, x_all, mod,
                                             attn_w_o[j].astype(BF16), ln1_g[i], ln1_b[i], w_router, b_router,
                                             ctx_row, alpha, E, "attn_out_ln")
        else:
            n_heads = ssm_a_log.shape[-1]
            d_inner = n_heads * SSM_HEAD_DIM
            d_conv = d_inner + 2 * SSM_GROUPS * SSM_STATE
            w_in = ssm_w_in[j]
            n_dt = w_in.shape[1] - d_inner - d_conv
            w_in_p = jnp.pad(w_in, ((0, 0), (0, LANES - n_dt))).astype(BF16)
            dt_bias_p = jnp.pad(ssm_dt_bias[j].reshape(1, n_dt), ((0, 0), (0, LANES - n_dt)))
            z, xbc, dt, dtt = _ssm_in_proj(x_all, mod, w_in_p, dt_bias_p, ctx_row, d_inner, d_conv)
            conv_w_p = jnp.pad(ssm_conv_w[j], ((0, SUBLANES - SSM_CONV), (0, 0)))
            xt, bc = _ssm_conv(xbc, conv_w_p, ssm_conv_b[j], d_inner)
            y_f = _ssd_scan(xt, bc, dt, dtt, ssm_a_log[j], ssm_d[j], 0, C)
            y_b = _ssd_scan(xt, bc, dt, dtt, ssm_a_log[j], ssm_d[j], 1, C)
            row = lambda b, t: (b, t, 0)
            col = lambda b, t: (b, 0, t)
            lead_args = [y_f, y_b, z, ssm_norm_g[j].reshape(1, d_inner)]
            lead_specs = [pl.BlockSpec((1, d_inner, TM), col), pl.BlockSpec((1, d_inner, TM), col),
                          pl.BlockSpec((1, TM, d_inner), row), pl.BlockSpec((1, d_inner), lambda b, t: (0, 0))]
            x1, h2, route, counts = _proj_ln(_ssm_out_kernel, lead_args, lead_specs, x_all, mod,
                                             ssm_w_out[j].astype(BF16), ln1_g[i], ln1_b[i], w_router, b_router,
                                             ctx_row, alpha, E, "ssm_out_ln")
        x_all = _moe_layer(x1, h2, route, counts, mod, moe_w_gate, moe_w_up, moe_w_down, i,
                           ln2_g[i], ln2_b[i], ctx_row, alpha, 1 if last else 0)
    return x_all
```

```python
import functools
import math

import jax
import jax.numpy as jnp
from jax import lax
from jax.experimental import pallas as pl
from jax.experimental.pallas import tpu as pltpu

F32 = jnp.float32
BF16 = jnp.bfloat16
HIGHEST = lax.Precision.HIGHEST

GRID_W = 64
DA_HEADS = 8
DA_HEAD_DIM = 64
DA_V_DIM = 2 * DA_HEAD_DIM
ROPE_THETA = 10000.0
SSM_HEAD_DIM = 64
SSM_GROUPS = 4
SSM_STATE = 128
SSM_CONV = 5
SSM_CHUNK = 128
MOE_GROUPS = 4
MOE_PER_GROUP = 8
MOE_TOP_K = 2
LN_EPS = 1e-5
RMS_EPS = 1e-5
N_MIXERS = 2

LANES = 128
SUBLANES = 8
TM = 256
MOE_BM = 256
ATTN_TK = 2816
ATTN_HEADS = 2
VT_ROWS = DA_V_DIM + 16
VMEM_LIMIT = 56 * 1024 * 1024


def _cparams(*sem):
    return pltpu.CompilerParams(dimension_semantics=sem, vmem_limit_bytes=VMEM_LIMIT)


def _silu(v):
    return v / (1.0 + jnp.exp(-v))


def _layer_norm(r, g, b):
    mu = jnp.mean(r, axis=-1, keepdims=True)
    d = r - mu
    var = jnp.mean(d * d, axis=-1, keepdims=True)
    return d * lax.rsqrt(var + LN_EPS) * g + b


def _mod_row(ctx_row):
    return lambda b, t: (jnp.where(t == 0, ctx_row, b), 0, 0)


def _mod_kernel(c_ref, w_ref, b_ref, o_ref):
    s = _silu(c_ref[...])
    o_ref[0] = jnp.dot(s, w_ref[0], precision=HIGHEST, preferred_element_type=F32) + b_ref[0]


def _modulation(cond, w_mod, b_mod):
    L, D, D6 = w_mod.shape
    R = cond.shape[0]
    tn = 1536
    return pl.pallas_call(
        _mod_kernel,
        grid=(L, D6 // tn),
        in_specs=[pl.BlockSpec((R, D), lambda l, j: (0, 0)),
                  pl.BlockSpec((1, D, tn), lambda l, j: (l, 0, j)),
                  pl.BlockSpec((1, 1, tn), lambda l, j: (l, 0, j))],
        out_specs=pl.BlockSpec((1, R, tn), lambda l, j: (l, 0, j)),
        out_shape=jax.ShapeDtypeStruct((L, R, D6), F32),
        compiler_params=_cparams("parallel", "parallel"),
        name="modulation",
    )(cond, w_mod, b_mod.reshape(L, 1, D6))


def _qkv_kernel(x_ref, mod_ref, w_ref, cos_ref, sa_ref, sb_ref, qt_ref, k_ref, vt_ref, *, d_model, q_scale):
    D = d_model
    H = k_ref.shape[0]
    x = x_ref[0]
    h = (x * (1.0 + mod_ref[0, :, D:2 * D]) + mod_ref[0, :, 0:D]).astype(BF16)
    cos = cos_ref[...]
    sa = sa_ref[...]
    sb = sb_ref[...]
    first_map = lax.broadcasted_iota(jnp.int32, (LANES, TM), 0) < DA_HEAD_DIM
    ones_rows = (lax.broadcasted_iota(jnp.int32, (VT_ROWS - DA_V_DIM, TM), 0) == 0).astype(BF16)

    def rope(a):
        return a * cos + pltpu.roll(a, LANES - 16, 1) * sa + pltpu.roll(a, 16, 1) * sb

    for jj in range(3 * H // 2):
        acc2 = jnp.dot(h, w_ref[:, jj * 2 * LANES:(jj + 1) * 2 * LANES], preferred_element_type=F32)
        for half in range(2):
            j = 2 * jj + half
            acc = acc2[:, half * LANES:(half + 1) * LANES]
            hd = j % H
            if j < H:
                qt = (rope(acc) * q_scale).T
                qt_ref[hd, 0, 0] = jnp.where(first_map, qt, 0.0).astype(BF16)
                qt_ref[hd, 0, 1] = jnp.where(first_map, 0.0, qt).astype(BF16)
            elif j < 2 * H:
                k_ref[hd, 0] = rope(acc).astype(BF16)
            else:
                vt_ref[hd, 0, 0:DA_V_DIM, :] = acc.T.astype(BF16)
                vt_ref[hd, 0, DA_V_DIM:VT_ROWS, :] = ones_rows


def _qkv_proj(x_all, mod, w_qkv, cos, sa, sb, ctx_row):
    B, N, D = x_all.shape
    H = DA_HEADS
    kern = functools.partial(_qkv_kernel, d_model=D, q_scale=DA_HEAD_DIM ** -0.5 * math.log2(math.e))
    return pl.pallas_call(
        kern,
        grid=(B, N // TM),
        in_specs=[pl.BlockSpec((1, TM, D), lambda b, t: (b, t, 0)),
                  pl.BlockSpec((1, 1, 6 * D), _mod_row(ctx_row)),
                  pl.BlockSpec((D, w_qkv.shape[1]), lambda b, t: (0, 0)),
                  pl.BlockSpec((TM, LANES), lambda b, t: (t, 0)),
                  pl.BlockSpec((TM, LANES), lambda b, t: (t, 0)),
                  pl.BlockSpec((TM, LANES), lambda b, t: (t, 0))],
        out_specs=[pl.BlockSpec((H, 1, 2, LANES, TM), lambda b, t: (0, b, 0, 0, t)),
                   pl.BlockSpec((H, 1, TM, LANES), lambda b, t: (0, b, t, 0)),
                   pl.BlockSpec((H, 1, VT_ROWS, TM), lambda b, t: (0, b, 0, t))],
        out_shape=[jax.ShapeDtypeStruct((H, B, 2, LANES, N), BF16),
                   jax.ShapeDtypeStruct((H, B, N, LANES), BF16),
                   jax.ShapeDtypeStruct((H, B, VT_ROWS, N), BF16)],
        compiler_params=_cparams("parallel", "parallel"),
        name="qkv_proj",
    )(x_all, mod, w_qkv, cos, sa, sb)


def _rope_tables(n_ctx, n_lat):
    t = jnp.arange(n_lat)
    pos = jnp.stack([t // GRID_W, t % GRID_W], axis=1).astype(F32)
    axis_dims = DA_HEAD_DIM // 2
    inv = ROPE_THETA ** (-jnp.arange(0, axis_dims, 2, dtype=F32) / axis_dims)
    lane = jnp.arange(LANES)
    d = lane % DA_HEAD_DIM
    axis = d // axis_dims
    second = (d % axis_dims) // (axis_dims // 2)
    ang = pos[:, axis] * inv[d % (axis_dims // 2)][None, :]
    cos = jnp.cos(ang)
    sin = jnp.sin(ang)
    sa = jnp.where(second[None, :] == 0, -sin, 0.0)
    sb = jnp.where(second[None, :] == 1, sin, 0.0)
    pad = lambda a, v: jnp.concatenate([jnp.full((n_ctx, LANES), v, F32), a], axis=0)
    return pad(cos, 1.0), pad(sa, 0.0), pad(sb, 0.0)


def _attn_tiles(qt_ref, k_ref, vt_ref, bufs, n_chunks, tk):
    tq = qt_ref.shape[-1]
    heads = range(qt_ref.shape[0])

    def scores(h, off, slot):
        k = k_ref[h, 0, pl.ds(off, tk), :]
        for j in range(2):
            bufs[h][slot][j, 0:tk, :] = jnp.dot(k, qt_ref[h, 0, j], preferred_element_type=F32)

    def softmax_pv(h, off, slot, carry):
        vt = vt_ref[h, 0, :, pl.ds(off, tk)]
        new = []
        for j in range(2):
            m, acc = carry[j]
            s = bufs[h][slot][j, 0:tk, :]
            m_new = jnp.maximum(m, jnp.max(s, axis=0, keepdims=True))
            p = jnp.exp2(s - m_new).astype(BF16)
            acc = jnp.exp2(m - m_new) * acc + jnp.dot(vt, p, preferred_element_type=F32)
            new.append((m_new, acc))
        return tuple(new)

    carries = [tuple((jnp.full((1, tq), -jnp.inf, F32), jnp.zeros((VT_ROWS, tq), F32)) for _ in range(2))
               for _ in heads]
    for h in heads:
        scores(h, 0, 0)
    for c in range(n_chunks):
        for h in heads:
            if c + 1 < n_chunks:
                scores(h, (c + 1) * tk, (c + 1) % 2)
            carries[h] = softmax_pv(h, c * tk, c % 2, carries[h])
    return [tuple(acc[0:DA_V_DIM] / acc[DA_V_DIM:DA_V_DIM + 1] for _, acc in carries[h]) for h in heads]


def _attn_kernel(lam_ref, g_ref, qt_ref, k_ref, vt_ref, o_ref, *s_refs, n_ctx, n_all, tk, lambda_init):
    t = pl.program_id(2)
    bufs = [(s_refs[2 * h], s_refs[2 * h + 1]) for h in range(qt_ref.shape[0])]
    lp = lam_ref[...]
    lam = (jnp.exp(jnp.sum(lp[0:1] * lp[1:2], axis=-1, keepdims=True))
           - jnp.exp(jnp.sum(lp[2:3] * lp[3:4], axis=-1, keepdims=True)) + lambda_init)

    def finish(outs):
        for h, (o0, o1) in enumerate(outs):
            o = o0 - lam * o1
            ms = jnp.mean(o * o, axis=0, keepdims=True)
            o = o * lax.rsqrt(ms + RMS_EPS) * (g_ref[...] * (1.0 - lambda_init))
            o_ref[0, :, h * DA_V_DIM:(h + 1) * DA_V_DIM] = o.T.astype(o_ref.dtype)

    @pl.when(t == 0)
    def _():
        finish(_attn_tiles(qt_ref, k_ref, vt_ref, bufs, 1, n_ctx))

    @pl.when(t > 0)
    def _():
        finish(_attn_tiles(qt_ref, k_ref, vt_ref, bufs, n_all // tk, tk))


def _diff_attention(qt, k, vt, lam_params, subln_g, lambda_init, n_ctx):
    H, B, N, _ = k.shape
    tk = ATTN_TK
    nh = ATTN_HEADS
    assert n_ctx == TM and N % tk == 0 and n_ctx <= tk and H % nh == 0
    kern = functools.partial(_attn_kernel, n_ctx=n_ctx, n_all=N, tk=tk, lambda_init=lambda_init)
    return pl.pallas_call(
        kern,
        grid=(B, H // nh, N // TM),
        in_specs=[pl.BlockSpec(lam_params.shape, lambda b, h, t: (0, 0)),
                  pl.BlockSpec((DA_V_DIM, 1), lambda b, h, t: (0, 0)),
                  pl.BlockSpec((nh, 1, 2, LANES, TM), lambda b, h, t: (h, b, 0, 0, t)),
                  pl.BlockSpec((nh, 1, N, LANES), lambda b, h, t: (h, b, 0, 0)),
                  pl.BlockSpec((nh, 1, VT_ROWS, N), lambda b, h, t: (h, b, 0, 0))],
        out_specs=pl.BlockSpec((1, TM, nh * DA_V_DIM), lambda b, h, t: (b, t, h)),
        out_shape=jax.ShapeDtypeStruct((B, N, H * DA_V_DIM), BF16),
        scratch_shapes=[pltpu.VMEM((2, tk, TM), F32) for _ in range(2 * nh)],
        compiler_params=_cparams("parallel", "parallel", "parallel"),
        name="diff_attention",
    )(lam_params, subln_g.reshape(DA_V_DIM, 1), qt, k, vt)


ROUTE_ID, ROUTE_GATE, ROUTE_RANK = 0, MOE_TOP_K, 2 * MOE_TOP_K
ROUTE_ROWS = SUBLANES
ROUTER_EXPERT_LANE = SUBLANES


def _route_tile(lg, cnt_ref, n_experts):
    G, PER = MOE_GROUPS, MOE_PER_GROUP
    tm = lg.shape[0]
    lt = lg.T
    row = lax.broadcasted_iota(jnp.int32, (SUBLANES, tm), 0)
    ninf = -jnp.inf
    first = lambda hit: jnp.min(jnp.where(hit, row, SUBLANES), axis=0, keepdims=True)
    gl = jnp.where(row < G, lt[0:SUBLANES], ninf)
    gmax = jnp.max(gl, axis=0, keepdims=True)
    g_top = 1.0 / jnp.sum(jnp.exp(gl - gmax), axis=0, keepdims=True)
    g_idx = first(gl == gmax)
    el = lt[ROUTER_EXPERT_LANE:ROUTER_EXPERT_LANE + PER]
    for g in range(1, G):
        el = jnp.where(g_idx == g, lt[ROUTER_EXPERT_LANE + g * PER:ROUTER_EXPERT_LANE + (g + 1) * PER], el)
    e1 = jnp.max(el, axis=0, keepdims=True)
    i1 = first(el == e1)
    el2 = jnp.where(row == i1, ninf, el)
    e2 = jnp.max(el2, axis=0, keepdims=True)
    i2 = first(el2 == e2)
    id1 = g_idx * PER + i1
    id2 = g_idx * PER + i2
    w2 = jnp.exp(e2 - e1)
    den = 1.0 + w2
    erow = lax.broadcasted_iota(jnp.int32, (n_experts, tm), 0)
    hit1 = erow == id1
    hit2 = erow == id2
    onehot = jnp.logical_or(hit1, hit2)
    ui = lax.broadcasted_iota(jnp.int32, (tm, tm), 0)
    ti = lax.broadcasted_iota(jnp.int32, (tm, tm), 1)
    earlier = jnp.dot(onehot.astype(BF16), (ui < ti).astype(BF16), preferred_element_type=F32) + cnt_ref[...]
    r1 = jnp.sum(jnp.where(hit1, earlier, 0.0), axis=0, keepdims=True)
    r2 = jnp.sum(jnp.where(hit2, earlier, 0.0), axis=0, keepdims=True)
    cnt_ref[...] = cnt_ref[...] + jnp.sum(onehot.astype(F32), axis=1, keepdims=True)
    rec = jnp.zeros((ROUTE_ROWS, tm), F32)
    for r, val in ((ROUTE_ID, id1.astype(F32)), (ROUTE_ID + 1, id2.astype(F32)),
                   (ROUTE_GATE, g_top / den), (ROUTE_GATE + 1, g_top * w2 / den),
                   (ROUTE_RANK, r1), (ROUTE_RANK + 1, r2)):
        rec = jnp.where(row == r, val, rec)
    return rec


def _proj_ln_tail(a, x_ref, mod_ref, w_ref, lng_ref, lnb_ref, wr_ref, br_ref, x_out, h_out, rt_out, cnt_out, *,
                  d_model, alpha, n_experts):
    D = d_model

    @pl.when(jnp.logical_and(pl.program_id(0) == 0, pl.program_id(1) == 0))
    def _():
        cnt_out[...] = jnp.zeros_like(cnt_out)

    hb = a.shape[0] // 2
    os_ = [jnp.dot(a[r * hb:(r + 1) * hb], w_ref[...], preferred_element_type=F32) for r in range(2)]
    lgs = []
    for r in range(2):
        rows = slice(r * hb, (r + 1) * hb)
        res = alpha * x_ref[0, rows, :] + mod_ref[0, :, 2 * D:3 * D] * os_[r]
        xn = _layer_norm(res, lng_ref[...], lnb_ref[...])
        x_out[0, rows, :] = xn
        h2 = xn * (1.0 + mod_ref[0, :, 4 * D:5 * D]) + mod_ref[0, :, 3 * D:4 * D]
        h_out[0, rows, :] = h2.astype(h_out.dtype)
        lgs.append(jnp.dot(h2, wr_ref[...], precision=HIGHEST, preferred_element_type=F32) + br_ref[...])
    lg = jnp.concatenate(lgs, axis=0)
    rt_out[0] = _route_tile(lg, cnt_out, n_experts)


def _attn_out_kernel(a_ref, *refs, **kw):
    _proj_ln_tail(a_ref[0], *refs, **kw)


def _ssm_out_kernel(yf_ref, yb_ref, z_ref, ng_ref, *refs, **kw):
    y = (yf_ref[0].astype(F32) + yb_ref[0].astype(F32)).T
    gz = y * _silu(z_ref[0].astype(F32))
    ms = jnp.mean(gz * gz, axis=-1, keepdims=True)
    a = (gz * lax.rsqrt(ms + RMS_EPS) * ng_ref[...]).astype(BF16)
    _proj_ln_tail(a, *refs, **kw)


def _proj_ln(kernel_fn, lead_args, lead_specs, x_all, mod, w, ln_g, ln_b, w_router, b_router, ctx_row, alpha,
             n_experts, name):
    B, N, D = x_all.shape
    row = lambda b, t: (b, t, 0)
    const2 = lambda b, t: (0, 0)
    kern = functools.partial(kernel_fn, d_model=D, alpha=alpha, n_experts=n_experts)
    return pl.pallas_call(
        kern,
        grid=(B, N // TM),
        in_specs=lead_specs + [
            pl.BlockSpec((1, TM, D), row),
            pl.BlockSpec((1, 1, 6 * D), _mod_row(ctx_row)),
            pl.BlockSpec(w.shape, const2),
            pl.BlockSpec((1, D), const2),
            pl.BlockSpec((1, D), const2),
            pl.BlockSpec(w_router.shape, const2),
            pl.BlockSpec((1, LANES), const2)],
        out_specs=[pl.BlockSpec((1, TM, D), row), pl.BlockSpec((1, TM, D), row),
                   pl.BlockSpec((1, ROUTE_ROWS, TM), lambda b, t: (b, 0, t)), pl.BlockSpec((n_experts, 1), const2)],
        out_shape=[jax.ShapeDtypeStruct((B, N, D), F32), jax.ShapeDtypeStruct((B, N, D), BF16),
                   jax.ShapeDtypeStruct((B, ROUTE_ROWS, N), F32), jax.ShapeDtypeStruct((n_experts, 1), F32)],
        compiler_params=_cparams("arbitrary", "arbitrary"),
        name=name,
    )(*lead_args, x_all, mod, w, ln_g.reshape(1, D), ln_b.reshape(1, D), w_router, b_router)


def _moe_kernel(be_ref, nu_ref, x_ref, wg_ref, wu_ref, wd_ref, o_ref, wgu_bf, wd_bf, *, d_ff):
    i = pl.program_id(0)
    active = i < nu_ref[0]
    new_expert = jnp.logical_or(i == 0, be_ref[i] != be_ref[jnp.maximum(i - 1, 0)])

    @pl.when(jnp.logical_and(active, new_expert))
    def _():
        wgu_bf[:, 0:d_ff] = wg_ref[0, 0].astype(BF16)
        wgu_bf[:, d_ff:] = wu_ref[0, 0].astype(BF16)
        wd_bf[...] = wd_ref[0, 0].astype(BF16)

    @pl.when(active)
    def _():
        hb = x_ref.shape[0] // 2
        gus = [jnp.dot(x_ref[r * hb:(r + 1) * hb, :], wgu_bf[...], preferred_element_type=F32) for r in range(2)]
        for r in range(2):
            mid = (_silu(gus[r][:, :d_ff]) * gus[r][:, d_ff:]).astype(BF16)
            o_ref[r * hb:(r + 1) * hb, :] = jnp.dot(mid, wd_bf[...], preferred_element_type=F32).astype(o_ref.dtype)

    @pl.when(jnp.logical_not(active))
    def _():
        o_ref[...] = jnp.zeros_like(o_ref)


def _moe_experts(buf, blk_expert, n_used, w_gate, w_up, w_down, layer):
    n_rows, D = buf.shape
    d_ff = w_down.shape[2]
    grid_spec = pltpu.PrefetchScalarGridSpec(
        num_scalar_prefetch=2,
        grid=(n_rows // MOE_BM,),
        in_specs=[pl.BlockSpec((MOE_BM, D), lambda i, be, nu: (i, 0)),
                  pl.BlockSpec((1, 1, D, d_ff), lambda i, be, nu: (layer, be[i], 0, 0)),
                  pl.BlockSpec((1, 1, D, d_ff), lambda i, be, nu: (layer, be[i], 0, 0)),
                  pl.BlockSpec((1, 1, d_ff, D), lambda i, be, nu: (layer, be[i], 0, 0))],
        out_specs=pl.BlockSpec((MOE_BM, D), lambda i, be, nu: (i, 0)),
        scratch_shapes=[pltpu.VMEM((D, 2 * d_ff), BF16), pltpu.VMEM((d_ff, D), BF16)])
    return pl.pallas_call(
        functools.partial(_moe_kernel, d_ff=d_ff),
        grid_spec=grid_spec,
        out_shape=jax.ShapeDtypeStruct((n_rows, D), BF16),
        compiler_params=_cparams("arbitrary"),
        name="moe_experts",
    )(blk_expert, n_used, buf, w_gate, w_up, w_down)


def _dispatch(eid, rank, sizes):
    K, T = eid.shape
    A = T * K
    E = sizes.shape[0]
    padded = (sizes + MOE_BM - 1) // MOE_BM * MOE_BM
    pad_end = jnp.cumsum(padded)
    pad_start = pad_end - padded
    starts = jnp.cumsum(sizes) - sizes
    hit = eid[:, :, None] == jnp.arange(E, dtype=jnp.int32)
    dest = jnp.sum(jnp.where(hit, pad_start, 0), axis=-1) + rank
    n_blocks = -(-A // MOE_BM) + E
    blk_start = jnp.arange(n_blocks, dtype=jnp.int32) * MOE_BM
    blk_expert = jnp.minimum(jnp.sum(blk_start[:, None] >= pad_end[None, :], axis=1), E - 1).astype(jnp.int32)
    n_used = (pad_end[-1] // MOE_BM).astype(jnp.int32).reshape(1)
    order = jnp.argsort(eid.T.reshape(A))
    within = jnp.arange(MOE_BM, dtype=jnp.int32)[None, :] + (blk_start - pad_start[blk_expert])[:, None]
    src = jnp.clip(starts[blk_expert][:, None] + within, 0, A - 1)
    filler = (blk_start[:, None] + jnp.arange(MOE_BM, dtype=jnp.int32)[None, :]) % T
    slot_tok = jnp.where(within < sizes[blk_expert][:, None], jnp.take(order, src, mode='clip') // K, filler)
    return slot_tok.reshape(-1).astype(jnp.int32), blk_expert, n_used, dest.astype(jnp.int32)


def _combine_kernel(x_ref, y0_ref, y1_ref, rt_ref, mod_ref, lng_ref, lnb_ref, o_ref, *, d_model, alpha):
    D = d_model
    rt = rt_ref[0].T
    f = (rt[:, ROUTE_GATE:ROUTE_GATE + 1] * y0_ref[0, 0].astype(F32)
         + rt[:, ROUTE_GATE + 1:ROUTE_GATE + 2] * y1_ref[0, 0].astype(F32))
    r = alpha * x_ref[0] + mod_ref[0, :, 5 * D:6 * D] * f
    o_ref[0] = _layer_norm(r, lng_ref[...], lnb_ref[...])


def _combine_ln(x_all, ys, route, mod, ln_g, ln_b, ctx_row, alpha, t0):
    B, N, D = x_all.shape
    nt = N // TM - t0
    tiles_per_batch = N // TM
    row = lambda b, t: (b, t + t0, 0)
    const2 = lambda b, t: (0, 0)
    mrow = _mod_row(ctx_row)
    return pl.pallas_call(
        functools.partial(_combine_kernel, d_model=D, alpha=alpha),
        grid=(B, nt),
        in_specs=[pl.BlockSpec((1, TM, D), row),
                  pl.BlockSpec((1, 1, TM, D), lambda b, t: (0, b * tiles_per_batch + t + t0, 0, 0)),
                  pl.BlockSpec((1, 1, TM, D), lambda b, t: (1, b * tiles_per_batch + t + t0, 0, 0)),
                  pl.BlockSpec((1, ROUTE_ROWS, TM), lambda b, t: (b, 0, t + t0)),
                  pl.BlockSpec((1, 1, 6 * D), lambda b, t: mrow(b, t + t0)),
                  pl.BlockSpec((1, D), const2),
                  pl.BlockSpec((1, D), const2)],
        out_specs=pl.BlockSpec((1, TM, D), lambda b, t: (b, t, 0)),
        out_shape=jax.ShapeDtypeStruct((B, nt * TM, D), F32),
        compiler_params=_cparams("parallel", "parallel"),
        name="moe_combine_ln",
    )(x_all, ys, ys, route, mod, ln_g.reshape(1, D), ln_b.reshape(1, D))


def _moe_layer(x1, h2, route, counts, mod, w_gate, w_up, w_down, layer, ln_g, ln_b, ctx_row, alpha, t0):
    B, N, D = x1.shape
    T = B * N
    by_row = lambda r: jnp.swapaxes(route[:, r:r + MOE_TOP_K, :], 0, 1).reshape(MOE_TOP_K, T).astype(jnp.int32)
    eid = by_row(ROUTE_ID)
    rank = by_row(ROUTE_RANK)
    sizes = counts[:, 0].astype(jnp.int32)
    slot_tok, blk_expert, n_used, dest = _dispatch(eid, rank, sizes)
    buf = jnp.take(h2.reshape(T, D), slot_tok, axis=0, mode='clip')
    yb = _moe_experts(buf, blk_expert, n_used, w_gate, w_up, w_down, layer)
    ys = jnp.take(yb, dest, axis=0, mode='clip').reshape(MOE_TOP_K, T // TM, TM, D)
    return _combine_ln(x1, ys, route, mod, ln_g, ln_b, ctx_row, alpha, t0)


def _softplus(v):
    return jnp.maximum(v, 0.0) + jnp.log1p(jnp.exp(-jnp.abs(v)))


def _ssm_in_kernel(x_ref, mod_ref, w_ref, dtb_ref, z_out, xbc_out, dt_out, dtt_out, *, d_model, d_inner, d_conv):
    D = d_model
    h = (x_ref[0] * (1.0 + mod_ref[0, :, D:2 * D]) + mod_ref[0, :, 0:D]).astype(BF16)
    z_out[0] = jnp.dot(h, w_ref[:, 0:d_inner], preferred_element_type=F32).astype(z_out.dtype)
    cw = 512
    for j in range(d_conv // cw):
        xbc_out[0, :, j * cw:(j + 1) * cw] = jnp.dot(
            h, w_ref[:, d_inner + j * cw:d_inner + (j + 1) * cw], preferred_element_type=F32).astype(xbc_out.dtype)
    dt = jnp.dot(h, w_ref[:, d_inner + d_conv:], preferred_element_type=F32) + dtb_ref[...]
    dt = _softplus(dt)
    dt_out[0] = dt
    dtt_out[0] = dt.T


def _ssm_in_proj(x_all, mod, w_in_p, dt_bias_p, ctx_row, d_inner, d_conv):
    B, N, D = x_all.shape
    row = lambda b, t: (b, t, 0)
    kern = functools.partial(_ssm_in_kernel, d_model=D, d_inner=d_inner, d_conv=d_conv)
    return pl.pallas_call(
        kern,
        grid=(B, N // TM),
        in_specs=[pl.BlockSpec((1, TM, D), row),
                  pl.BlockSpec((1, 1, 6 * D), _mod_row(ctx_row)),
                  pl.BlockSpec(w_in_p.shape, lambda b, t: (0, 0)),
                  pl.BlockSpec((1, LANES), lambda b, t: (0, 0))],
        out_specs=[pl.BlockSpec((1, TM, d_inner), row), pl.BlockSpec((1, TM, d_conv), row),
                   pl.BlockSpec((1, TM, LANES), row), pl.BlockSpec((1, LANES, TM), lambda b, t: (b, 0, t))],
        out_shape=[jax.ShapeDtypeStruct((B, N, d_inner), BF16), jax.ShapeDtypeStruct((B, N, d_conv), BF16),
                   jax.ShapeDtypeStruct((B, N, LANES), F32), jax.ShapeDtypeStruct((B, LANES, N), F32)],
        compiler_params=_cparams("parallel", "parallel"),
        name="ssm_in_proj",
    )(x_all, mod, w_in_p, dt_bias_p)


CONV_HALO = 16


def _conv_kernel(x_ref, prev_ref, next_ref, w_ref, b_ref, xt_ref, bc_ref, ext_ref, *, n_tiles, d_inner):
    t = pl.program_id(1)
    halo = CONV_HALO
    pad = SSM_CONV // 2
    has_prev = t >= 2
    has_next = jnp.logical_and(t >= 1, t < n_tiles - 1)
    ext_ref[0:halo, :] = jnp.where(has_prev, prev_ref[0].astype(F32), 0.0)
    ext_ref[halo:halo + TM, :] = x_ref[0].astype(F32)
    ext_ref[halo + TM:, :] = jnp.where(has_next, next_ref[0].astype(F32), 0.0)
    rows = TM + 2 * halo
    cw = LANES
    for j in range(ext_ref.shape[1] // cw):
        cols = slice(j * cw, (j + 1) * cw)
        ext = ext_ref[:, cols]
        acc = b_ref[:, cols] + w_ref[pad:pad + 1, cols] * ext[halo:halo + TM]
        for k in range(SSM_CONV):
            if k != pad:
                acc = acc + w_ref[k:k + 1, cols] * pltpu.roll(ext, (pad - k) % rows, 0)[halo:halo + TM]
        act = _silu(acc)
        if j * cw < d_inner:
            xt_ref[0, j * cw:(j + 1) * cw, :] = act.T.astype(xt_ref.dtype)
        else:
            bc_ref[0, :, j * cw - d_inner:(j + 1) * cw - d_inner] = act.astype(bc_ref.dtype)


def _ssm_conv(xbc, conv_w_p, conv_b, d_inner):
    B, N, C = xbc.shape
    n_tiles = N // TM
    hb = TM // CONV_HALO
    last_hb = N // CONV_HALO - 1
    return pl.pallas_call(
        functools.partial(_conv_kernel, n_tiles=n_tiles, d_inner=d_inner),
        grid=(B, n_tiles),
        in_specs=[pl.BlockSpec((1, TM, C), lambda b, t: (b, t, 0)),
                  pl.BlockSpec((1, CONV_HALO, C), lambda b, t: (b, jnp.maximum(t * hb - 1, 0), 0)),
                  pl.BlockSpec((1, CONV_HALO, C), lambda b, t: (b, jnp.minimum((t + 1) * hb, last_hb), 0)),
                  pl.BlockSpec(conv_w_p.shape, lambda b, t: (0, 0)),
                  pl.BlockSpec((1, C), lambda b, t: (0, 0))],
        out_specs=[pl.BlockSpec((1, d_inner, TM), lambda b, t: (b, 0, t)),
                   pl.BlockSpec((1, TM, C - d_inner), lambda b, t: (b, t, 0))],
        out_shape=[jax.ShapeDtypeStruct((B, d_inner, N), BF16), jax.ShapeDtypeStruct((B, N, C - d_inner), BF16)],
        scratch_shapes=[pltpu.VMEM((TM + 2 * CONV_HALO, C), F32)],
        compiler_params=_cparams("parallel", "parallel"),
        name="ssm_conv",
    )(xbc, xbc, xbc, conv_w_p, conv_b.reshape(1, C))


def _ssd_chunk(sub, xt_ref, bc_ref, dt_ref, dtt_ref, alr_ref, alc_ref, dsk_ref, y_ref, state_ref, xw_ref, *,
               direction, n_heads):
    L = SSM_CHUNK
    P = SSM_HEAD_DIM
    NS = SSM_STATE
    hpg = n_heads // SSM_GROUPS
    gw = hpg * P
    reverse = direction == 1
    l0 = direction * n_heads

    tsl = slice(sub * L, (sub + 1) * L)
    ri = lax.broadcasted_iota(jnp.int32, (L, L), 0)
    ci = lax.broadcasted_iota(jnp.int32, (L, L), 1)
    before = (ri >= ci) if reverse else (ri <= ci)
    after = (ci >= ri) if reverse else (ci <= ri)
    a = jnp.dot(after.astype(F32), dt_ref[0, tsl, :] * -jnp.exp(alr_ref[...]), precision=HIGHEST,
                preferred_element_type=F32)
    at = jnp.dot(dtt_ref[0, :, tsl] * -jnp.exp(alc_ref[...]), before.astype(F32), precision=HIGHEST,
                 preferred_element_type=F32)[l0:l0 + n_heads]
    last = 0 if reverse else L - 1
    tot = jnp.broadcast_to(at[:, last:last + 1], (n_heads, L))
    ea = jnp.exp(at)
    wgt = jnp.exp(tot - at)
    etot = jnp.exp(tot)
    dtt = dtt_ref[0, l0:l0 + n_heads, tsl]
    dsk = dsk_ref[direction]

    for g in range(SSM_GROUPS):
        bg = bc_ref[0, tsl, g * NS:(g + 1) * NS]
        cg = bc_ref[0, tsl, (SSM_GROUPS + g) * NS:(SSM_GROUPS + g + 1) * NS]
        cbt = lax.dot_general(bg, cg, (((1,), (1,)), ((), ())), preferred_element_type=F32)
        yoff = lax.dot_general(state_ref[g * gw:(g + 1) * gw, :].astype(BF16), cg, (((1,), (1,)), ((), ())),
                               preferred_element_type=F32)
        for hh in range(hpg):
            h = g * hpg + hh
            rows = slice(h * P, (h + 1) * P)
            decay = jnp.where(before, jnp.exp(at[h:h + 1, :] - a[:, l0 + h:l0 + h + 1]), 0.0)
            mt = (cbt * decay).astype(BF16)
            xh = xt_ref[0, rows, tsl].astype(F32)
            xdt = xh * dtt[h:h + 1, :]
            y = (jnp.dot(xdt.astype(BF16), mt, preferred_element_type=F32)
                 + yoff[hh * P:(hh + 1) * P, :] * ea[h:h + 1, :] + xh * dsk[h:h + 1, :])
            y_ref[0, rows, tsl] = y.astype(y_ref.dtype)
            xw_ref[hh * P:(hh + 1) * P, :] = (xdt * wgt[h:h + 1, :]).astype(xw_ref.dtype)
        upd = jnp.dot(xw_ref[...], bg, preferred_element_type=F32)
        for hh in range(hpg):
            h = g * hpg + hh
            rows = slice(h * P, (h + 1) * P)
            state_ref[rows, :] = state_ref[rows, :] * etot[h:h + 1, :] + upd[hh * P:(hh + 1) * P, :]


def _ssd_kernel(*refs, direction, n_heads):
    state_ref = refs[-2]

    @pl.when(pl.program_id(1) == 0)
    def _():
        state_ref[...] = jnp.zeros_like(state_ref)

    for sub in ((1, 0) if direction == 1 else (0, 1)):
        _ssd_chunk(sub, *refs, direction=direction, n_heads=n_heads)


def _ssd_scan(xt, bc, dt, dtt, a_log, d_skip, direction, n_ctx):
    B, d_inner, N = xt.shape
    n_heads = a_log.shape[1]
    L = SSM_CHUNK
    LB = 2 * L
    assert n_ctx % LB == 0 and N % LB == 0
    nc = N // LB
    ncc = n_ctx // LB
    if direction == 0:
        chunk = lambda c: c
    else:
        chunk = lambda c: jnp.where(c < ncc, ncc - 1 - c, nc - 1 - (c - ncc))
    kern = functools.partial(_ssd_kernel, direction=direction, n_heads=n_heads)
    a_log_lanes = jnp.pad(a_log.reshape(-1), (0, LANES - a_log.size))
    d_skip_lanes = jnp.broadcast_to(d_skip.astype(F32)[:, :, None], d_skip.shape + (L,))
    return pl.pallas_call(
        kern,
        grid=(B, nc),
        in_specs=[pl.BlockSpec((1, d_inner, LB), lambda b, c: (b, 0, chunk(c))),
                  pl.BlockSpec((1, LB, bc.shape[2]), lambda b, c: (b, chunk(c), 0)),
                  pl.BlockSpec((1, LB, LANES), lambda b, c: (b, chunk(c), 0)),
                  pl.BlockSpec((1, LANES, LB), lambda b, c: (b, 0, chunk(c))),
                  pl.BlockSpec((1, LANES), lambda b, c: (0, 0)),
                  pl.BlockSpec((LANES, 1), lambda b, c: (0, 0)),
                  pl.BlockSpec(d_skip_lanes.shape, lambda b, c: (0, 0, 0))],
        out_specs=pl.BlockSpec((1, d_inner, LB), lambda b, c: (b, 0, chunk(c))),
        out_shape=jax.ShapeDtypeStruct((B, d_inner, N), BF16),
        scratch_shapes=[pltpu.VMEM((d_inner, SSM_STATE), F32),
                        pltpu.VMEM((d_inner // SSM_GROUPS, L), BF16)],
        compiler_params=_cparams("parallel", "arbitrary"),
        name=f"ssd_scan_dir{direction}",
    )(xt, bc, dt, dtt, a_log_lanes.reshape(1, LANES), a_log_lanes.reshape(LANES, 1), d_skip_lanes)


def kernel(x, c, ctx, c_ctx, w_mod, b_mod, ln1_g, ln1_b, ln2_g, ln2_b, attn_w_qkv, attn_w_o, attn_lq1, attn_lk1, attn_lq2, attn_lk2, attn_subln_g, ssm_w_in, ssm_conv_w, ssm_conv_b, ssm_dt_bias, ssm_a_log, ssm_d, ssm_norm_g, ssm_w_out, moe_w_group, moe_b_group, moe_w_expert, moe_b_expert, moe_w_gate, moe_w_up, moe_w_down):
    B, S, D = x.shape
    C = ctx.shape[1]
    depth = w_mod.shape[0]
    E = moe_w_expert.shape[-1]
    assert C == TM and S % TM == 0 and B < SUBLANES
    alpha = (2 * depth) ** 0.25
    ctx_row = B

    cond = jnp.zeros((SUBLANES, D), F32).at[:B].set(c).at[B].set(c_ctx)
    mods = _modulation(cond, w_mod, b_mod)
    x_all = jnp.concatenate([ctx, x], axis=1)
    cos, sa, sb = _rope_tables(C, S)

    for i in range(depth):
        last = i == depth - 1
        j = i // N_MIXERS
        mod = mods[i].reshape(SUBLANES, 1, 6 * D)
        lane_pad = lambda a, n: jnp.pad(a, [(0, 0)] * (a.ndim - 1) + [(0, n - a.shape[-1])])
        w_router = jnp.concatenate([lane_pad(moe_w_group[i], ROUTER_EXPERT_LANE),
                                    lane_pad(moe_w_expert[i], LANES - ROUTER_EXPERT_LANE)], axis=-1)
        b_router = jnp.concatenate([lane_pad(moe_b_group[i], ROUTER_EXPERT_LANE),
                                    lane_pad(moe_b_expert[i], LANES - ROUTER_EXPERT_LANE)]).reshape(1, LANES)
        if i % N_MIXERS == 0:
            lambda_init = 0.8 - 0.6 * math.exp(-0.3 * i)
            qt, k, vt = _qkv_proj(x_all, mod, attn_w_qkv[j].astype(BF16), cos, sa, sb, ctx_row)
            lam_params = jnp.stack([attn_lq1[j], attn_lk1[j], attn_lq2[j], attn_lk2[j]])
            o = _diff_attention(qt, k, vt, lam_params, attn_subln_g[j], lambda_init, C)
            lead_args = [o]
            lead_specs = [pl.BlockSpec((1, TM, o.shape[-1]), lambda b, t: (b, t, 0))]
            x1, h2, route, counts = _proj_ln(_attn_out_kernel, lead_args, lead_specs, x_all, mod,
                                             attn_w_o[j].astype(BF16), ln1_g[i], ln1_b[i], w_router, b_router,
                                             ctx_row, alpha, E, "attn_out_ln")
        else:
            n_heads = ssm_a_log.shape[-1]
            d_inner = n_heads * SSM_HEAD_DIM
            d_conv = d_inner + 2 * SSM_GROUPS * SSM_STATE
            w_in = ssm_w_in[j]
            n_dt = w_in.shape[1] - d_inner - d_conv
            w_in_p = jnp.pad(w_in, ((0, 0), (0, LANES - n_dt))).astype(BF16)
            dt_bias_p = jnp.pad(ssm_dt_bias[j].reshape(1, n_dt), ((0, 0), (0, LANES - n_dt)))
            z, xbc, dt, dtt = _ssm_in_proj(x_all, mod, w_in_p, dt_bias_p, ctx_row, d_inner, d_conv)
            conv_w_p = jnp.pad(ssm_conv_w[j], ((0, SUBLANES - SSM_CONV), (0, 0)))
            xt, bc = _ssm_conv(xbc, conv_w_p, ssm_conv_b[j], d_inner)
            y_f = _ssd_scan(xt, bc, dt, dtt, ssm_a_log[j], ssm_d[j], 0, C)
            y_b = _ssd_scan(xt, bc, dt, dtt, ssm_a_log[j], ssm_d[j], 1, C)
            row = lambda b, t: (b, t, 0)
            col = lambda b, t: (b, 0, t)
            lead_args = [y_f, y_b, z, ssm_norm_g[j].reshape(1, d_inner)]
            lead_specs = [pl.BlockSpec((1, d_inner, TM), col), pl.BlockSpec((1, d_inner, TM), col),
                          pl.BlockSpec((1, TM, d_inner), row), pl.BlockSpec((1, d_inner), lambda b, t: (0, 0))]
            x1, h2, route, counts = _proj_ln(_ssm_out_kernel, lead_args, lead_specs, x_all, mod,
                                             ssm_w_out[j].astype(BF16), ln1_g[i], ln1_b[i], w_router, b_router,
                                             ctx_row, alpha, E, "ssm_out_ln")
        x_all = _moe_layer(x1, h2, route, counts, mod, moe_w_gate, moe_w_up, moe_w_down, i,
                           ln2_g[i], ln2_b[i], ctx_row, alpha, 1 if last else 0)
    return x_all
```

```python
import functools
import math

import jax
import jax.numpy as jnp
from jax import lax
from jax.experimental import pallas as pl
from jax.experimental.pallas import tpu as pltpu

F32 = jnp.float32
BF16 = jnp.bfloat16
HIGHEST = lax.Precision.HIGHEST

GRID_W = 64
DA_HEADS = 8
DA_HEAD_DIM = 64
DA_V_DIM = 2 * DA_HEAD_DIM
ROPE_THETA = 10000.0
SSM_HEAD_DIM = 64
SSM_GROUPS = 4
SSM_STATE = 128
SSM_CONV = 5
SSM_CHUNK = 128
MOE_GROUPS = 4
MOE_PER_GROUP = 8
MOE_TOP_K = 2
LN_EPS = 1e-5
RMS_EPS = 1e-5
N_MIXERS = 2

LANES = 128
SUBLANES = 8
TM = 256
MOE_BM = 256
ATTN_TK = 2816
ATTN_HEADS = 2
VT_ROWS = DA_V_DIM + 16
VMEM_LIMIT = 56 * 1024 * 1024


def _cparams(*sem):
    return pltpu.CompilerParams(dimension_semantics=sem, vmem_limit_bytes=VMEM_LIMIT)


def _silu(v):
    return v / (1.0 + jnp.exp(-v))


def _layer_norm(r, g, b):
    mu = jnp.mean(r, axis=-1, keepdims=True)
    d = r - mu
    var = jnp.mean(d * d, axis=-1, keepdims=True)
    return d * lax.rsqrt(var + LN_EPS) * g + b


def _mod_row(ctx_row):
    return lambda b, t: (jnp.where(t == 0, ctx_row, b), 0, 0)


def _mod_kernel(c_ref, w_ref, b_ref, o_ref):
    s = _silu(c_ref[...])
    o_ref[0] = jnp.dot(s, w_ref[0], precision=HIGHEST, preferred_element_type=F32) + b_ref[0]


def _modulation(cond, w_mod, b_mod):
    L, D, D6 = w_mod.shape
    R = cond.shape[0]
    tn = 1536
    return pl.pallas_call(
        _mod_kernel,
        grid=(L, D6 // tn),
        in_specs=[pl.BlockSpec((R, D), lambda l, j: (0, 0)),
                  pl.BlockSpec((1, D, tn), lambda l, j: (l, 0, j)),
                  pl.BlockSpec((1, 1, tn), lambda l, j: (l, 0, j))],
        out_specs=pl.BlockSpec((1, R, tn), lambda l, j: (l, 0, j)),
        out_shape=jax.ShapeDtypeStruct((L, R, D6), F32),
        compiler_params=_cparams("parallel", "parallel"),
        name="modulation",
    )(cond, w_mod, b_mod.reshape(L, 1, D6))


def _qkv_kernel(x_ref, mod_ref, w_ref, cos_ref, sa_ref, sb_ref, qt_ref, k_ref, vt_ref, *, d_model, q_scale):
    D = d_model
    H = k_ref.shape[0]
    x = x_ref[0]
    h = (x * (1.0 + mod_ref[0, :, D:2 * D]) + mod_ref[0, :, 0:D]).astype(BF16)
    cos = cos_ref[...]
    sa = sa_ref[...]
    sb = sb_ref[...]
    first_map = lax.broadcasted_iota(jnp.int32, (LANES, TM), 0) < DA_HEAD_DIM
    ones_rows = (lax.broadcasted_iota(jnp.int32, (VT_ROWS - DA_V_DIM, TM), 0) == 0).astype(BF16)

    def rope(a):
        return a * cos + pltpu.roll(a, LANES - 16, 1) * sa + pltpu.roll(a, 16, 1) * sb

    for jj in range(3 * H // 2):
        acc2 = jnp.dot(h, w_ref[:, jj * 2 * LANES:(jj + 1) * 2 * LANES], preferred_element_type=F32)
        for half in range(2):
            j = 2 * jj + half
            acc = acc2[:, half * LANES:(half + 1) * LANES]
            hd = j % H
            if j < H:
                qt = (rope(acc) * q_scale).T
                qt_ref[hd, 0, 0] = jnp.where(first_map, qt, 0.0).astype(BF16)
                qt_ref[hd, 0, 1] = jnp.where(first_map, 0.0, qt).astype(BF16)
            elif j < 2 * H:
                k_ref[hd, 0] = rope(acc).astype(BF16)
            else:
                vt_ref[hd, 0, 0:DA_V_DIM, :] = acc.T.astype(BF16)
                vt_ref[hd, 0, DA_V_DIM:VT_ROWS, :] = ones_rows


def _qkv_proj(x_all, mod, w_qkv, cos, sa, sb, ctx_row):
    B, N, D = x_all.shape
    H = DA_HEADS
    kern = functools.partial(_qkv_kernel, d_model=D, q_scale=DA_HEAD_DIM ** -0.5 * math.log2(math.e))
    return pl.pallas_call(
        kern,
        grid=(B, N // TM),
        in_specs=[pl.BlockSpec((1, TM, D), lambda b, t: (b, t, 0)),
                  pl.BlockSpec((1, 1, 6 * D), _mod_row(ctx_row)),
                  pl.BlockSpec((D, w_qkv.shape[1]), lambda b, t: (0, 0)),
                  pl.BlockSpec((TM, LANES), lambda b, t: (t, 0)),
                  pl.BlockSpec((TM, LANES), lambda b, t: (t, 0)),
                  pl.BlockSpec((TM, LANES), lambda b, t: (t, 0))],
        out_specs=[pl.BlockSpec((H, 1, 2, LANES, TM), lambda b, t: (0, b, 0, 0, t)),
                   pl.BlockSpec((H, 1, TM, LANES), lambda b, t: (0, b, t, 0)),
                   pl.BlockSpec((H, 1, VT_ROWS, TM), lambda b, t: (0, b, 0, t))],
        out_shape=[jax.ShapeDtypeStruct((H, B, 2, LANES, N), BF16),
                   jax.ShapeDtypeStruct((H, B, N, LANES), BF16),
                   jax.ShapeDtypeStruct((H, B, VT_ROWS, N), BF16)],
        compiler_params=_cparams("parallel", "parallel"),
        name="qkv_proj",
    )(x_all, mod, w_qkv, cos, sa, sb)


def _rope_tables(n_ctx, n_lat):
    t = jnp.arange(n_lat)
    pos = jnp.stack([t // GRID_W, t % GRID_W], axis=1).astype(F32)
    axis_dims = DA_HEAD_DIM // 2
    inv = ROPE_THETA ** (-jnp.arange(0, axis_dims, 2, dtype=F32) / axis_dims)
    lane = jnp.arange(LANES)
    d = lane % DA_HEAD_DIM
    axis = d // axis_dims
    second = (d % axis_dims) // (axis_dims // 2)
    ang = pos[:, axis] * inv[d % (axis_dims // 2)][None, :]
    cos = jnp.cos(ang)
    sin = jnp.sin(ang)
    sa = jnp.where(second[None, :] == 0, -sin, 0.0)
    sb = jnp.where(second[None, :] == 1, sin, 0.0)
    pad = lambda a, v: jnp.concatenate([jnp.full((n_ctx, LANES), v, F32), a], axis=0)
    return pad(cos, 1.0), pad(sa, 0.0), pad(sb, 0.0)


def _attn_tiles(qt_ref, k_ref, vt_ref, bufs, n_chunks, tk):
    tq = qt_ref.shape[-1]
    heads = range(qt_ref.shape[0])

    def scores(h, off, slot):
        k = k_ref[h, 0, pl.ds(off, tk), :]
        for j in range(2):
            bufs[h][slot][j, 0:tk, :] = jnp.dot(k, qt_ref[h, 0, j], preferred_element_type=F32)

    def softmax_pv(h, off, slot, carry):
        vt = vt_ref[h, 0, :, pl.ds(off, tk)]
        new = []
        for j in range(2):
            m, acc = carry[j]
            s = bufs[h][slot][j, 0:tk, :]
            m_new = jnp.maximum(m, jnp.max(s, axis=0, keepdims=True))
            p = jnp.exp2(s - m_new).astype(BF16)
            acc = jnp.exp2(m - m_new) * acc + jnp.dot(vt, p, preferred_element_type=F32)
            new.append((m_new, acc))
        return tuple(new)

    carries = [tuple((jnp.full((1, tq), -jnp.inf, F32), jnp.zeros((VT_ROWS, tq), F32)) for _ in range(2))
               for _ in heads]
    for h in heads:
        scores(h, 0, 0)
    for c in range(n_chunks):
        for h in heads:
            if c + 1 < n_chunks:
                scores(h, (c + 1) * tk, (c + 1) % 2)
            carries[h] = softmax_pv(h, c * tk, c % 2, carries[h])
    return [tuple(acc[0:DA_V_DIM] / acc[DA_V_DIM:DA_V_DIM + 1] for _, acc in carries[h]) for h in heads]


def _attn_kernel(lam_ref, g_ref, qt_ref, k_ref, vt_ref, o_ref, *s_refs, n_ctx, n_all, tk, lambda_init):
    t = pl.program_id(2)
    bufs = [(s_refs[2 * h], s_refs[2 * h + 1]) for h in range(qt_ref.shape[0])]
    lp = lam_ref[...]
    lam = (jnp.exp(jnp.sum(lp[0:1] * lp[1:2], axis=-1, keepdims=True))
           - jnp.exp(jnp.sum(lp[2:3] * lp[3:4], axis=-1, keepdims=True)) + lambda_init)

    def finish(outs):
        for h, (o0, o1) in enumerate(outs):
            o = o0 - lam * o1
            ms = jnp.mean(o * o, axis=0, keepdims=True)
            o = o * lax.rsqrt(ms + RMS_EPS) * (g_ref[...] * (1.0 - lambda_init))
            o_ref[0, :, h * DA_V_DIM:(h + 1) * DA_V_DIM] = o.T.astype(o_ref.dtype)

    @pl.when(t == 0)
    def _():
        finish(_attn_tiles(qt_ref, k_ref, vt_ref, bufs, 1, n_ctx))

    @pl.when(t > 0)
    def _():
        finish(_attn_tiles(qt_ref, k_ref, vt_ref, bufs, n_all // tk, tk))


def _diff_attention(qt, k, vt, lam_params, subln_g, lambda_init, n_ctx):
    H, B, N, _ = k.shape
    tk = ATTN_TK
    nh = ATTN_HEADS
    assert n_ctx == TM and N % tk == 0 and n_ctx <= tk and H % nh == 0
    kern = functools.partial(_attn_kernel, n_ctx=n_ctx, n_all=N, tk=tk, lambda_init=lambda_init)
    return pl.pallas_call(
        kern,
        grid=(B, H // nh, N // TM),
        in_specs=[pl.BlockSpec(lam_params.shape, lambda b, h, t: (0, 0)),
                  pl.BlockSpec((DA_V_DIM, 1), lambda b, h, t: (0, 0)),
                  pl.BlockSpec((nh, 1, 2, LANES, TM), lambda b, h, t: (h, b, 0, 0, t)),
                  pl.BlockSpec((nh, 1, N, LANES), lambda b, h, t: (h, b, 0, 0)),
                  pl.BlockSpec((nh, 1, VT_ROWS, N), lambda b, h, t: (h, b, 0, 0))],
        out_specs=pl.BlockSpec((1, TM, nh * DA_V_DIM), lambda b, h, t: (b, t, h)),
        out_shape=jax.ShapeDtypeStruct((B, N, H * DA_V_DIM), BF16),
        scratch_shapes=[pltpu.VMEM((2, tk, TM), F32) for _ in range(2 * nh)],
        compiler_params=_cparams("parallel", "parallel", "parallel"),
        name="diff_attention",
    )(lam_params, subln_g.reshape(DA_V_DIM, 1), qt, k, vt)


ROUTE_ID, ROUTE_GATE, ROUTE_RANK = 0, MOE_TOP_K, 2 * MOE_TOP_K
ROUTE_ROWS = SUBLANES
ROUTER_EXPERT_LANE = SUBLANES


def _route_tile(lg, cnt_ref, n_experts):
    G, PER = MOE_GROUPS, MOE_PER_GROUP
    tm = lg.shape[0]
    lt = lg.T
    row = lax.broadcasted_iota(jnp.int32, (SUBLANES, tm), 0)
    ninf = -jnp.inf
    first = lambda hit: jnp.min(jnp.where(hit, row, SUBLANES), axis=0, keepdims=True)
    gl = jnp.where(row < G, lt[0:SUBLANES], ninf)
    gmax = jnp.max(gl, axis=0, keepdims=True)
    g_top = 1.0 / jnp.sum(jnp.exp(gl - gmax), axis=0, keepdims=True)
    g_idx = first(gl == gmax)
    el = lt[ROUTER_EXPERT_LANE:ROUTER_EXPERT_LANE + PER]
    for g in range(1, G):
        el = jnp.where(g_idx == g, lt[ROUTER_EXPERT_LANE + g * PER:ROUTER_EXPERT_LANE + (g + 1) * PER], el)
    e1 = jnp.max(el, axis=0, keepdims=True)
    i1 = first(el == e1)
    el2 = jnp.where(row == i1, ninf, el)
    e2 = jnp.max(el2, axis=0, keepdims=True)
    i2 = first(el2 == e2)
    id1 = g_idx * PER + i1
    id2 = g_idx * PER + i2
    w2 = jnp.exp(e2 - e1)
    den = 1.0 + w2
    erow = lax.broadcasted_iota(jnp.int32, (n_experts, tm), 0)
    hit1 = erow == id1
    hit2 = erow == id2
    onehot = jnp.logical_or(hit1, hit2)
    ui = lax.broadcasted_iota(jnp.int32, (tm, tm), 0)
    ti = lax.broadcasted_iota(jnp.int32, (tm, tm), 1)
    earlier = jnp.dot(onehot.astype(BF16), (ui < ti).astype(BF16), preferred_element_type=F32) + cnt_ref[...]
    r1 = jnp.sum(jnp.where(hit1, earlier, 0.0), axis=0, keepdims=True)
    r2 = jnp.sum(jnp.where(hit2, earlier, 0.0), axis=0, keepdims=True)
    cnt_ref[...] = cnt_ref[...] + jnp.sum(onehot.astype(F32), axis=1, keepdims=True)
    rec = jnp.zeros((ROUTE_ROWS, tm), F32)
    for r, val in ((ROUTE_ID, id1.astype(F32)), (ROUTE_ID + 1, id2.astype(F32)),
                   (ROUTE_GATE, g_top / den), (ROUTE_GATE + 1, g_top * w2 / den),
                   (ROUTE_RANK, r1), (ROUTE_RANK + 1, r2)):
        rec = jnp.where(row == r, val, rec)
    return rec


def _proj_ln_tail(a, x_ref, mod_ref, w_ref, lng_ref, lnb_ref, wr_ref, br_ref, x_out, h_out, rt_out, cnt_out, *,
                  d_model, alpha, n_experts):
    D = d_model

    @pl.when(jnp.logical_and(pl.program_id(0) == 0, pl.program_id(1) == 0))
    def _():
        cnt_out[...] = jnp.zeros_like(cnt_out)

    hb = a.shape[0] // 2
    os_ = [jnp.dot(a[r * hb:(r + 1) * hb], w_ref[...], preferred_element_type=F32) for r in range(2)]
    lgs = []
    for r in range(2):
        rows = slice(r * hb, (r + 1) * hb)
        res = alpha * x_ref[0, rows, :] + mod_ref[0, :, 2 * D:3 * D] * os_[r]
        xn = _layer_norm(res, lng_ref[...], lnb_ref[...])
        x_out[0, rows, :] = xn
        h2 = xn * (1.0 + mod_ref[0, :, 4 * D:5 * D]) + mod_ref[0, :, 3 * D:4 * D]
        h_out[0, rows, :] = h2.astype(h_out.dtype)
        lgs.append(jnp.dot(h2, wr_ref[...], precision=HIGHEST, preferred_element_type=F32) + br_ref[...])
    lg = jnp.concatenate(lgs, axis=0)
    rt_out[0] = _route_tile(lg, cnt_out, n_experts)


def _attn_out_kernel(a_ref, *refs, **kw):
    _proj_ln_tail(a_ref[0], *refs, **kw)


def _ssm_out_kernel(yf_ref, yb_ref, z_ref, ng_ref, *refs, **kw):
    y = (yf_ref[0].astype(F32) + yb_ref[0].astype(F32)).T
    gz = y * _silu(z_ref[0].astype(F32))
    ms = jnp.mean(gz * gz, axis=-1, keepdims=True)
    a = (gz * lax.rsqrt(ms + RMS_EPS) * ng_ref[...]).astype(BF16)
    _proj_ln_tail(a, *refs, **kw)


def _proj_ln(kernel_fn, lead_args, lead_specs, x_all, mod, w, ln_g, ln_b, w_router, b_router, ctx_row, alpha,
             n_experts, name):
    B, N, D = x_all.shape
    row = lambda b, t: (b, t, 0)
    const2 = lambda b, t: (0, 0)
    kern = functools.partial(kernel_fn, d_model=D, alpha=alpha, n_experts=n_experts)
    return pl.pallas_call(
        kern,
        grid=(B, N // TM),
        in_specs=lead_specs + [
            pl.BlockSpec((1, TM, D), row),
            pl.BlockSpec((1, 1, 6 * D), _mod_row(ctx_row)),
            pl.BlockSpec(w.shape, const2),
            pl.BlockSpec((1, D), const2),
            pl.BlockSpec((1, D), const2),
            pl.BlockSpec(w_router.shape, const2),
            pl.BlockSpec((1, LANES), const2)],
        out_specs=[pl.BlockSpec((1, TM, D), row), pl.BlockSpec((1, TM, D), row),
                   pl.BlockSpec((1, ROUTE_ROWS, TM), lambda b, t: (b, 0, t)), pl.BlockSpec((n_experts, 1), const2)],
        out_shape=[jax.ShapeDtypeStruct((B, N, D), F32), jax.ShapeDtypeStruct((B, N, D), BF16),
                   jax.ShapeDtypeStruct((B, ROUTE_ROWS, N), F32), jax.ShapeDtypeStruct((n_experts, 1), F32)],
        compiler_params=_cparams("arbitrary", "arbitrary"),
        name=name,
    )(*lead_args, x_all, mod, w, ln_g.reshape(1, D), ln_b.reshape(1, D), w_router, b_router)


def _moe_kernel(be_ref, nu_ref, x_ref, wg_ref, wu_ref, wd_ref, o_ref, wgu_bf, wd_bf, *, d_ff):
    i = pl.program_id(0)
    active = i < nu_ref[0]
    new_expert = jnp.logical_or(i == 0, be_ref[i] != be_ref[jnp.maximum(i - 1, 0)])

    @pl.when(jnp.logical_and(active, new_expert))
    def _():
        wgu_bf[:, 0:d_ff] = wg_ref[0, 0].astype(BF16)
        wgu_bf[:, d_ff:] = wu_ref[0, 0].astype(BF16)
        wd_bf[...] = wd_ref[0, 0].astype(BF16)

    @pl.when(active)
    def _():
        hb = x_ref.shape[0] // 2
        gus = [jnp.dot(x_ref[r * hb:(r + 1) * hb, :], wgu_bf[...], preferred_element_type=F32) for r in range(2)]
        for r in range(2):
            mid = (_silu(gus[r][:, :d_ff]) * gus[r][:, d_ff:]).astype(BF16)
            o_ref[r * hb:(r + 1) * hb, :] = jnp.dot(mid, wd_bf[...], preferred_element_type=F32).astype(o_ref.dtype)

    @pl.when(jnp.logical_not(active))
    def _():
        o_ref[...] = jnp.zeros_like(o_ref)


def _moe_experts(buf, blk_expert, n_used, w_gate, w_up, w_down, layer):
    n_rows, D = buf.shape
    d_ff = w_down.shape[2]
    grid_spec = pltpu.PrefetchScalarGridSpec(
        num_scalar_prefetch=2,
        grid=(n_rows // MOE_BM,),
        in_specs=[pl.BlockSpec((MOE_BM, D), lambda i, be, nu: (i, 0)),
                  pl.BlockSpec((1, 1, D, d_ff), lambda i, be, nu: (layer, be[i], 0, 0)),
                  pl.BlockSpec((1, 1, D, d_ff), lambda i, be, nu: (layer, be[i], 0, 0)),
                  pl.BlockSpec((1, 1, d_ff, D), lambda i, be, nu: (layer, be[i], 0, 0))],
        out_specs=pl.BlockSpec((MOE_BM, D), lambda i, be, nu: (i, 0)),
        scratch_shapes=[pltpu.VMEM((D, 2 * d_ff), BF16), pltpu.VMEM((d_ff, D), BF16)])
    return pl.pallas_call(
        functools.partial(_moe_kernel, d_ff=d_ff),
        grid_spec=grid_spec,
        out_shape=jax.ShapeDtypeStruct((n_rows, D), BF16),
        compiler_params=_cparams("arbitrary"),
        name="moe_experts",
    )(blk_expert, n_used, buf, w_gate, w_up, w_down)


def _dispatch(eid, rank, sizes):
    K, T = eid.shape
    A = T * K
    E = sizes.shape[0]
    padded = (sizes + MOE_BM - 1) // MOE_BM * MOE_BM
    pad_end = jnp.cumsum(padded)
    pad_start = pad_end - padded
    starts = jnp.cumsum(sizes) - sizes
    hit = eid[:, :, None] == jnp.arange(E, dtype=jnp.int32)
    dest = jnp.sum(jnp.where(hit, pad_start, 0), axis=-1) + rank
    n_blocks = -(-A // MOE_BM) + E
    blk_start = jnp.arange(n_blocks, dtype=jnp.int32) * MOE_BM
    blk_expert = jnp.minimum(jnp.sum(blk_start[:, None] >= pad_end[None, :], axis=1), E - 1).astype(jnp.int32)
    n_used = (pad_end[-1] // MOE_BM).astype(jnp.int32).reshape(1)
    order = jnp.argsort(eid.T.reshape(A))
    within = jnp.arange(MOE_BM, dtype=jnp.int32)[None, :] + (blk_start - pad_start[blk_expert])[:, None]
    src = jnp.clip(starts[blk_expert][:, None] + within, 0, A - 1)
    filler = (blk_start[:, None] + jnp.arange(MOE_BM, dtype=jnp.int32)[None, :]) % T
    slot_tok = jnp.where(within < sizes[blk_expert][:, None], jnp.take(order, src, mode='clip') // K, filler)
    return slot_tok.reshape(-1).astype(jnp.int32), blk_expert, n_used, dest.astype(jnp.int32)


def _combine_kernel(x_ref, y0_ref, y1_ref, rt_ref, mod_ref, lng_ref, lnb_ref, o_ref, *, d_model, alpha):
    D = d_model
    rt = rt_ref[0].T
    f = (rt[:, ROUTE_GATE:ROUTE_GATE + 1] * y0_ref[0, 0].astype(F32)
         + rt[:, ROUTE_GATE + 1:ROUTE_GATE + 2] * y1_ref[0, 0].astype(F32))
    r = alpha * x_ref[0] + mod_ref[0, :, 5 * D:6 * D] * f
    o_ref[0] = _layer_norm(r, lng_ref[...], lnb_ref[...])


def _combine_ln(x_all, ys, route, mod, ln_g, ln_b, ctx_row, alpha, t0):
    B, N, D = x_all.shape
    nt = N // TM - t0
    tiles_per_batch = N // TM
    row = lambda b, t: (b, t + t0, 0)
    const2 = lambda b, t: (0, 0)
    mrow = _mod_row(ctx_row)
    return pl.pallas_call(
        functools.partial(_combine_kernel, d_model=D, alpha=alpha),
        grid=(B, nt),
        in_specs=[pl.BlockSpec((1, TM, D), row),
                  pl.BlockSpec((1, 1, TM, D), lambda b, t: (0, b * tiles_per_batch + t + t0, 0, 0)),
                  pl.BlockSpec((1, 1, TM, D), lambda b, t: (1, b * tiles_per_batch + t + t0, 0, 0)),
                  pl.BlockSpec((1, ROUTE_ROWS, TM), lambda b, t: (b, 0, t + t0)),
                  pl.BlockSpec((1, 1, 6 * D), lambda b, t: mrow(b, t + t0)),
                  pl.BlockSpec((1, D), const2),
                  pl.BlockSpec((1, D), const2)],
        out_specs=pl.BlockSpec((1, TM, D), lambda b, t: (b, t, 0)),
        out_shape=jax.ShapeDtypeStruct((B, nt * TM, D), F32),
        compiler_params=_cparams("parallel", "parallel"),
        name="moe_combine_ln",
    )(x_all, ys, ys, route, mod, ln_g.reshape(1, D), ln_b.reshape(1, D))


def _moe_layer(x1, h2, route, counts, mod, w_gate, w_up, w_down, layer, ln_g, ln_b, ctx_row, alpha, t0, defer=False):
    B, N, D = x1.shape
    T = B * N
    by_row = lambda r: jnp.swapaxes(route[:, r:r + MOE_TOP_K, :], 0, 1).reshape(MOE_TOP_K, T).astype(jnp.int32)
    eid = by_row(ROUTE_ID)
    rank = by_row(ROUTE_RANK)
    sizes = counts[:, 0].astype(jnp.int32)
    slot_tok, blk_expert, n_used, dest = _dispatch(eid, rank, sizes)
    buf = jnp.take(h2.reshape(T, D), slot_tok, axis=0, mode='clip')
    yb = _moe_experts(buf, blk_expert, n_used, w_gate, w_up, w_down, layer)
    ys = jnp.take(yb, dest, axis=0, mode='clip').reshape(MOE_TOP_K, T // TM, TM, D)
    if defer:
        return (x1, ys, route, mod, ln_g, ln_b)
    return _combine_ln(x1, ys, route, mod, ln_g, ln_b, ctx_row, alpha, t0)


def _softplus(v):
    return jnp.maximum(v, 0.0) + jnp.log1p(jnp.exp(-jnp.abs(v)))


def _in_proj_rows(h, rows, w_ref, dtb_ref, z_out, xbc_out, dt_out, dtt_out, d_inner, d_conv):
    z_out[0, rows, :] = jnp.dot(h, w_ref[:, 0:d_inner], preferred_element_type=F32).astype(z_out.dtype)
    cw = 512
    for j in range(d_conv // cw):
        xbc_out[0, rows, j * cw:(j + 1) * cw] = jnp.dot(
            h, w_ref[:, d_inner + j * cw:d_inner + (j + 1) * cw], preferred_element_type=F32).astype(xbc_out.dtype)
    dt = _softplus(jnp.dot(h, w_ref[:, d_inner + d_conv:], preferred_element_type=F32) + dtb_ref[...])
    dt_out[0, rows, :] = dt
    dtt_out[0, :, rows] = dt.T


def _ssm_in_kernel(x_ref, mod_ref, w_ref, dtb_ref, z_out, xbc_out, dt_out, dtt_out, *, d_model, d_inner, d_conv):
    D = d_model
    h = (x_ref[0] * (1.0 + mod_ref[0, :, D:2 * D]) + mod_ref[0, :, 0:D]).astype(BF16)
    _in_proj_rows(h, slice(0, TM), w_ref, dtb_ref, z_out, xbc_out, dt_out, dtt_out, d_inner, d_conv)


def _ssm_in_combine_kernel(x1_ref, y0_ref, y1_ref, rt_ref, modp_ref, lng_ref, lnb_ref, mod_ref, w_ref, dtb_ref,
                           x_out, z_out, xbc_out, dt_out, dtt_out, *, d_model, d_inner, d_conv, alpha):
    D = d_model
    rt = rt_ref[0].T
    hb = TM // 2
    hs = []
    for r in range(2):
        rows = slice(r * hb, (r + 1) * hb)
        f = (rt[rows, ROUTE_GATE:ROUTE_GATE + 1] * y0_ref[0, 0, rows, :].astype(F32)
             + rt[rows, ROUTE_GATE + 1:ROUTE_GATE + 2] * y1_ref[0, 0, rows, :].astype(F32))
        res = alpha * x1_ref[0, rows, :] + modp_ref[0, :, 5 * D:6 * D] * f
        xn = _layer_norm(res, lng_ref[...], lnb_ref[...])
        x_out[0, rows, :] = xn
        hs.append((xn * (1.0 + mod_ref[0, :, D:2 * D]) + mod_ref[0, :, 0:D]).astype(BF16))
    for r in range(2):
        _in_proj_rows(hs[r], slice(r * hb, (r + 1) * hb), w_ref, dtb_ref, z_out, xbc_out, dt_out, dtt_out,
                      d_inner, d_conv)


def _ssm_in_proj(x_all, mod, w_in_p, dt_bias_p, ctx_row, d_inner, d_conv, pending=None, alpha=None):
    B, N, D = (pending[0] if pending is not None else x_all).shape
    row = lambda b, t: (b, t, 0)
    const2 = lambda b, t: (0, 0)
    tiles = N // TM
    proj_specs = [pl.BlockSpec((1, 1, 6 * D), _mod_row(ctx_row)),
                  pl.BlockSpec(w_in_p.shape, const2),
                  pl.BlockSpec((1, LANES), const2)]
    out_specs = [pl.BlockSpec((1, TM, d_inner), row), pl.BlockSpec((1, TM, d_conv), row),
                 pl.BlockSpec((1, TM, LANES), row), pl.BlockSpec((1, LANES, TM), lambda b, t: (b, 0, t))]
    out_shape = [jax.ShapeDtypeStruct((B, N, d_inner), BF16), jax.ShapeDtypeStruct((B, N, d_conv), BF16),
                 jax.ShapeDtypeStruct((B, N, LANES), F32), jax.ShapeDtypeStruct((B, LANES, N), F32)]
    if pending is None:
        kern = functools.partial(_ssm_in_kernel, d_model=D, d_inner=d_inner, d_conv=d_conv)
        in_specs = [pl.BlockSpec((1, TM, D), row)] + proj_specs
        args = (x_all, mod, w_in_p, dt_bias_p)
    else:
        x1, ys, route, mod_prev, ln_g, ln_b = pending
        kern = functools.partial(_ssm_in_combine_kernel, d_model=D, d_inner=d_inner, d_conv=d_conv, alpha=alpha)
        in_specs = [pl.BlockSpec((1, TM, D), row),
                    pl.BlockSpec((1, 1, TM, D), lambda b, t: (0, b * tiles + t, 0, 0)),
                    pl.BlockSpec((1, 1, TM, D), lambda b, t: (1, b * tiles + t, 0, 0)),
                    pl.BlockSpec((1, ROUTE_ROWS, TM), lambda b, t: (b, 0, t)),
                    pl.BlockSpec((1, 1, 6 * D), _mod_row(ctx_row)),
                    pl.BlockSpec((1, D), const2),
                    pl.BlockSpec((1, D), const2)] + proj_specs
        args = (x1, ys, ys, route, mod_prev, ln_g.reshape(1, D), ln_b.reshape(1, D), mod, w_in_p, dt_bias_p)
        out_specs = [pl.BlockSpec((1, TM, D), row)] + out_specs
        out_shape = [jax.ShapeDtypeStruct((B, N, D), F32)] + out_shape
    return pl.pallas_call(
        kern,
        grid=(B, tiles),
        in_specs=in_specs,
        out_specs=out_specs,
        out_shape=out_shape,
        compiler_params=_cparams("parallel", "parallel"),
        name="ssm_in_proj",
    )(*args)


CONV_HALO = 16


def _conv_kernel(x_ref, prev_ref, next_ref, w_ref, b_ref, xt_ref, bc_ref, ext_ref, *, n_tiles, d_inner):
    t = pl.program_id(1)
    halo = CONV_HALO
    pad = SSM_CONV // 2
    has_prev = t >= 2
    has_next = jnp.logical_and(t >= 1, t < n_tiles - 1)
    ext_ref[0:halo, :] = jnp.where(has_prev, prev_ref[0].astype(F32), 0.0)
    ext_ref[halo:halo + TM, :] = x_ref[0].astype(F32)
    ext_ref[halo + TM:, :] = jnp.where(has_next, next_ref[0].astype(F32), 0.0)
    rows = TM + 2 * halo
    cw = LANES
    for j in range(ext_ref.shape[1] // cw):
        cols = slice(j * cw, (j + 1) * cw)
        ext = ext_ref[:, cols]
        acc = b_ref[:, cols] + w_ref[pad:pad + 1, cols] * ext[halo:halo + TM]
        for k in range(SSM_CONV):
            if k != pad:
                acc = acc + w_ref[k:k + 1, cols] * pltpu.roll(ext, (pad - k) % rows, 0)[halo:halo + TM]
        act = _silu(acc)
        if j * cw < d_inner:
            xt_ref[0, j * cw:(j + 1) * cw, :] = act.T.astype(xt_ref.dtype)
        else:
            bc_ref[0, :, j * cw - d_inner:(j + 1) * cw - d_inner] = act.astype(bc_ref.dtype)


def _ssm_conv(xbc, conv_w_p, conv_b, d_inner):
    B, N, C = xbc.shape
    n_tiles = N // TM
    hb = TM // CONV_HALO
    last_hb = N // CONV_HALO - 1
    return pl.pallas_call(
        functools.partial(_conv_kernel, n_tiles=n_tiles, d_inner=d_inner),
        grid=(B, n_tiles),
        in_specs=[pl.BlockSpec((1, TM, C), lambda b, t: (b, t, 0)),
                  pl.BlockSpec((1, CONV_HALO, C), lambda b, t: (b, jnp.maximum(t * hb - 1, 0), 0)),
                  pl.BlockSpec((1, CONV_HALO, C), lambda b, t: (b, jnp.minimum((t + 1) * hb, last_hb), 0)),
                  pl.BlockSpec(conv_w_p.shape, lambda b, t: (0, 0)),
                  pl.BlockSpec((1, C), lambda b, t: (0, 0))],
        out_specs=[pl.BlockSpec((1, d_inner, TM), lambda b, t: (b, 0, t)),
                   pl.BlockSpec((1, TM, C - d_inner), lambda b, t: (b, t, 0))],
        out_shape=[jax.ShapeDtypeStruct((B, d_inner, N), BF16), jax.ShapeDtypeStruct((B, N, C - d_inner), BF16)],
        scratch_shapes=[pltpu.VMEM((TM + 2 * CONV_HALO, C), F32)],
        compiler_params=_cparams("parallel", "parallel"),
        name="ssm_conv",
    )(xbc, xbc, xbc, conv_w_p, conv_b.reshape(1, C))


def _ssd_chunk(sub, xt_ref, bc_ref, dt_ref, dtt_ref, alr_ref, alc_ref, dsk_ref, y_ref, state_ref, xw_ref, *,
               direction, n_heads):
    L = SSM_CHUNK
    P = SSM_HEAD_DIM
    NS = SSM_STATE
    hpg = n_heads // SSM_GROUPS
    gw = hpg * P
    reverse = direction == 1
    l0 = direction * n_heads

    tsl = slice(sub * L, (sub + 1) * L)
    ri = lax.broadcasted_iota(jnp.int32, (L, L), 0)
    ci = lax.broadcasted_iota(jnp.int32, (L, L), 1)
    before = (ri >= ci) if reverse else (ri <= ci)
    after = (ci >= ri) if reverse else (ci <= ri)
    a = jnp.dot(after.astype(F32), dt_ref[0, tsl, :] * -jnp.exp(alr_ref[...]), precision=HIGHEST,
                preferred_element_type=F32)
    at = jnp.dot(dtt_ref[0, :, tsl] * -jnp.exp(alc_ref[...]), before.astype(F32), precision=HIGHEST,
                 preferred_element_type=F32)[l0:l0 + n_heads]
    last = 0 if reverse else L - 1
    tot = jnp.broadcast_to(at[:, last:last + 1], (n_heads, L))
    ea = jnp.exp(at)
    wgt = jnp.exp(tot - at)
    etot = jnp.exp(tot)
    dtt = dtt_ref[0, l0:l0 + n_heads, tsl]
    dsk = dsk_ref[direction]

    for g in range(SSM_GROUPS):
        bg = bc_ref[0, tsl, g * NS:(g + 1) * NS]
        cg = bc_ref[0, tsl, (SSM_GROUPS + g) * NS:(SSM_GROUPS + g + 1) * NS]
        cbt = lax.dot_general(bg, cg, (((1,), (1,)), ((), ())), preferred_element_type=F32)
        yoff = lax.dot_general(state_ref[g * gw:(g + 1) * gw, :].astype(BF16), cg, (((1,), (1,)), ((), ())),
                               preferred_element_type=F32)
        for hh in range(hpg):
            h = g * hpg + hh
            rows = slice(h * P, (h + 1) * P)
            decay = jnp.where(before, jnp.exp(at[h:h + 1, :] - a[:, l0 + h:l0 + h + 1]), 0.0)
            mt = (cbt * decay).astype(BF16)
            xh = xt_ref[0, rows, tsl].astype(F32)
            xdt = xh * dtt[h:h + 1, :]
            y = (jnp.dot(xdt.astype(BF16), mt, preferred_element_type=F32)
                 + yoff[hh * P:(hh + 1) * P, :] * ea[h:h + 1, :] + xh * dsk[h:h + 1, :])
            y_ref[0, rows, tsl] = y.astype(y_ref.dtype)
            xw_ref[hh * P:(hh + 1) * P, :] = (xdt * wgt[h:h + 1, :]).astype(xw_ref.dtype)
        upd = jnp.dot(xw_ref[...], bg, preferred_element_type=F32)
        for hh in range(hpg):
            h = g * hpg + hh
            rows = slice(h * P, (h + 1) * P)
            state_ref[rows, :] = state_ref[rows, :] * etot[h:h + 1, :] + upd[hh * P:(hh + 1) * P, :]


def _ssd_kernel(*refs, direction, n_heads):
    state_ref = refs[-2]

    @pl.when(pl.program_id(1) == 0)
    def _():
        state_ref[...] = jnp.zeros_like(state_ref)

    for sub in ((1, 0) if direction == 1 else (0, 1)):
        _ssd_chunk(sub, *refs, direction=direction, n_heads=n_heads)


def _ssd_scan(xt, bc, dt, dtt, a_log, d_skip, direction, n_ctx):
    B, d_inner, N = xt.shape
    n_heads = a_log.shape[1]
    L = SSM_CHUNK
    LB = 2 * L
    assert n_ctx % LB == 0 and N % LB == 0
    nc = N // LB
    ncc = n_ctx // LB
    if direction == 0:
        chunk = lambda c: c
    else:
        chunk = lambda c: jnp.where(c < ncc, ncc - 1 - c, nc - 1 - (c - ncc))
    kern = functools.partial(_ssd_kernel, direction=direction, n_heads=n_heads)
    a_log_lanes = jnp.pad(a_log.reshape(-1), (0, LANES - a_log.size))
    d_skip_lanes = jnp.broadcast_to(d_skip.astype(F32)[:, :, None], d_skip.shape + (L,))
    return pl.pallas_call(
        kern,
        grid=(B, nc),
        in_specs=[pl.BlockSpec((1, d_inner, LB), lambda b, c: (b, 0, chunk(c))),
                  pl.BlockSpec((1, LB, bc.shape[2]), lambda b, c: (b, chunk(c), 0)),
                  pl.BlockSpec((1, LB, LANES), lambda b, c: (b, chunk(c), 0)),
                  pl.BlockSpec((1, LANES, LB), lambda b, c: (b, 0, chunk(c))),
                  pl.BlockSpec((1, LANES), lambda b, c: (0, 0)),
                  pl.BlockSpec((LANES, 1), lambda b, c: (0, 0)),
                  pl.BlockSpec(d_skip_lanes.shape, lambda b, c: (0, 0, 0))],
        out_specs=pl.BlockSpec((1, d_inner, LB), lambda b, c: (b, 0, chunk(c))),
        out_shape=jax.ShapeDtypeStruct((B, d_inner, N), BF16),
        scratch_shapes=[pltpu.VMEM((d_inner, SSM_STATE), F32),
                        pltpu.VMEM((d_inner // SSM_GROUPS, L), BF16)],
        compiler_params=_cparams("parallel", "arbitrary"),
        name=f"ssd_scan_dir{direction}",
    )(xt, bc, dt, dtt, a_log_lanes.reshape(1, LANES), a_log_lanes.reshape(LANES, 1), d_skip_lanes)


def kernel(x, c, ctx, c_ctx, w_mod, b_mod, ln1_g, ln1_b, ln2_g, ln2_b, attn_w_qkv, attn_w_o, attn_lq1, attn_lk1, attn_lq2, attn_lk2, attn_subln_g, ssm_w_in, ssm_conv_w, ssm_conv_b, ssm_dt_bias, ssm_a_log, ssm_d, ssm_norm_g, ssm_w_out, moe_w_group, moe_b_group, moe_w_expert, moe_b_expert, moe_w_gate, moe_w_up, moe_w_down):
    B, S, D = x.shape
    C = ctx.shape[1]
    depth = w_mod.shape[0]
    E = moe_w_expert.shape[-1]
    assert C == TM and S % TM == 0 and B < SUBLANES
    alpha = (2 * depth) ** 0.25
    ctx_row = B

    cond = jnp.zeros((SUBLANES, D), F32).at[:B].set(c).at[B].set(c_ctx)
    mods = _modulation(cond, w_mod, b_mod)
    x_all = jnp.concatenate([ctx, x], axis=1)
    cos, sa, sb = _rope_tables(C, S)

    pending = None
    for i in range(depth):
        last = i == depth - 1
        j = i // N_MIXERS
        mod = mods[i].reshape(SUBLANES, 1, 6 * D)
        lane_pad = lambda a, n: jnp.pad(a, [(0, 0)] * (a.ndim - 1) + [(0, n - a.shape[-1])])
        w_router = jnp.concatenate([lane_pad(moe_w_group[i], ROUTER_EXPERT_LANE),
                                    lane_pad(moe_w_expert[i], LANES - ROUTER_EXPERT_LANE)], axis=-1)
        b_router = jnp.concatenate([lane_pad(moe_b_group[i], ROUTER_EXPERT_LANE),
                                    lane_pad(moe_b_expert[i], LANES - ROUTER_EXPERT_LANE)]).reshape(1, LANES)
        if i % N_MIXERS == 0:
            lambda_init = 0.8 - 0.6 * math.exp(-0.3 * i)
            qt, k, vt = _qkv_proj(x_all, mod, attn_w_qkv[j].astype(BF16), cos, sa, sb, ctx_row)
            lam_params = jnp.stack([attn_lq1[j], attn_lk1[j], attn_lq2[j], attn_lk2[j]])
            o = _diff_attention(qt, k, vt, lam_params, attn_subln_g[j], lambda_init, C)
            lead_args = [o]
            lead_specs = [pl.BlockSpec((1, TM, o.shape[-1]), lambda b, t: (b, t, 0))]
            x1, h2, route, counts = _proj_ln(_attn_out_kernel, lead_args, lead_specs, x_all, mod,
                                             attn_w_o[j].astype(BF16), ln1_g[i], ln1_b[i], w_router, b_router,
                                             ctx_row, alpha, E, "attn_out_ln")
        else:
            n_heads = ssm_a_log.shape[-1]
            d_inner = n_heads * SSM_HEAD_DIM
            d_conv = d_inner + 2 * SSM_GROUPS * SSM_STATE
            w_in = ssm_w_in[j]
            n_dt = w_in.shape[1] - d_inner - d_conv
            w_in_p = jnp.pad(w_in, ((0, 0), (0, LANES - n_dt))).astype(BF16)
            dt_bias_p = jnp.pad(ssm_dt_bias[j].reshape(1, n_dt), ((0, 0), (0, LANES - n_dt)))
            if pending is None:
                z, xbc, dt, dtt = _ssm_in_proj(x_all, mod, w_in_p, dt_bias_p, ctx_row, d_inner, d_conv)
            else:
                x_all, z, xbc, dt, dtt = _ssm_in_proj(None, mod, w_in_p, dt_bias_p, ctx_row, d_inner, d_conv,
                                                      pending=pending, alpha=alpha)
                pending = None
            conv_w_p = jnp.pad(ssm_conv_w[j], ((0, SUBLANES - SSM_CONV), (0, 0)))
            xt, bc = _ssm_conv(xbc, conv_w_p, ssm_conv_b[j], d_inner)
            y_f = _ssd_scan(xt, bc, dt, dtt, ssm_a_log[j], ssm_d[j], 0, C)
            y_b = _ssd_scan(xt, bc, dt, dtt, ssm_a_log[j], ssm_d[j], 1, C)
            row = lambda b, t: (b, t, 0)
            col = lambda b, t: (b, 0, t)
            lead_args = [y_f, y_b, z, ssm_norm_g[j].reshape(1, d_inner)]
            lead_specs = [pl.BlockSpec((1, d_inner, TM), col), pl.BlockSpec((1, d_inner, TM), col),
                          pl.BlockSpec((1, TM, d_inner), row), pl.BlockSpec((1, d_inner), lambda b, t: (0, 0))]
            x1, h2, route, counts = _proj_ln(_ssm_out_kernel, lead_args, lead_specs, x_all, mod,
                                             ssm_w_out[j].astype(BF16), ln1_g[i], ln1_b[i], w_router, b_router,
                                             ctx_row, alpha, E, "ssm_out_ln")
        defer = (not last) and (i + 1) % N_MIXERS == 1
        res = _moe_layer(x1, h2, route, counts, mod, moe_w_gate, moe_w_up, moe_w_down, i,
                         ln2_g[i], ln2_b[i], ctx_row, alpha, 1 if last else 0, defer=defer)
        if defer:
            pending = res
        else:
            x_all = res
    return x_all
```

```python
import functools
import math

import jax
import jax.numpy as jnp
from jax import lax
from jax.experimental import pallas as pl
from jax.experimental.pallas import tpu as pltpu

F32 = jnp.float32
BF16 = jnp.bfloat16
HIGHEST = lax.Precision.HIGHEST

GRID_W = 64
DA_HEADS = 8
DA_HEAD_DIM = 64
DA_V_DIM = 2 * DA_HEAD_DIM
ROPE_THETA = 10000.0
SSM_HEAD_DIM = 64
SSM_GROUPS = 4
SSM_STATE = 128
SSM_CONV = 5
SSM_CHUNK = 128
MOE_GROUPS = 4
MOE_PER_GROUP = 8
MOE_TOP_K = 2
LN_EPS = 1e-5
RMS_EPS = 1e-5
N_MIXERS = 2

LANES = 128
SUBLANES = 8
TM = 256
MOE_BM = 256
ATTN_TK = 2816
ATTN_HEADS = 2
VT_ROWS = DA_V_DIM + 16
VMEM_LIMIT = 56 * 1024 * 1024


def _cparams(*sem):
    return pltpu.CompilerParams(dimension_semantics=sem, vmem_limit_bytes=VMEM_LIMIT)


def _silu(v):
    return v / (1.0 + jnp.exp(-v))


def _layer_norm(r, g, b):
    mu = jnp.mean(r, axis=-1, keepdims=True)
    d = r - mu
    var = jnp.mean(d * d, axis=-1, keepdims=True)
    return d * lax.rsqrt(var + LN_EPS) * g + b


def _token_specs(lat_off, D):
    return [pl.BlockSpec((1, TM, D), lambda b, t: (b, 0, 0)),
            pl.BlockSpec((1, TM, D), lambda b, t: (b, jnp.maximum(t - lat_off, 0), 0))]


def _token_tile(c_ref, x_ref):
    return jnp.where(pl.program_id(1) == 0, c_ref[0], x_ref[0])


def _mod_row(ctx_row):
    return lambda b, t: (jnp.where(t == 0, ctx_row, b), 0, 0)


def _mod_kernel(c_ref, w_ref, b_ref, o_ref):
    s = _silu(c_ref[...])
    o_ref[0] = jnp.dot(s, w_ref[0], precision=HIGHEST, preferred_element_type=F32) + b_ref[0]


def _modulation(cond, w_mod, b_mod):
    L, D, D6 = w_mod.shape
    R = cond.shape[0]
    tn = 1536
    return pl.pallas_call(
        _mod_kernel,
        grid=(L, D6 // tn),
        in_specs=[pl.BlockSpec((R, D), lambda l, j: (0, 0)),
                  pl.BlockSpec((1, D, tn), lambda l, j: (l, 0, j)),
                  pl.BlockSpec((1, 1, tn), lambda l, j: (l, 0, j))],
        out_specs=pl.BlockSpec((1, R, tn), lambda l, j: (l, 0, j)),
        out_shape=jax.ShapeDtypeStruct((L, R, D6), F32),
        compiler_params=_cparams("parallel", "parallel"),
        name="modulation",
    )(cond, w_mod, b_mod.reshape(L, 1, D6))


def _qkv_kernel(c_ref, x_ref, mod_ref, w_ref, cos_ref, sa_ref, sb_ref, qt_ref, k_ref, vt_ref, *, d_model, q_scale):
    D = d_model
    H = k_ref.shape[0]
    x = _token_tile(c_ref, x_ref)
    h = (x * (1.0 + mod_ref[0, :, D:2 * D]) + mod_ref[0, :, 0:D]).astype(BF16)
    cos = cos_ref[...]
    sa = sa_ref[...]
    sb = sb_ref[...]
    first_map = lax.broadcasted_iota(jnp.int32, (LANES, TM), 0) < DA_HEAD_DIM
    ones_rows = (lax.broadcasted_iota(jnp.int32, (VT_ROWS - DA_V_DIM, TM), 0) == 0).astype(BF16)

    def rope(a):
        return a * cos + pltpu.roll(a, LANES - 16, 1) * sa + pltpu.roll(a, 16, 1) * sb

    for jj in range(3 * H // 2):
        acc2 = jnp.dot(h, w_ref[:, jj * 2 * LANES:(jj + 1) * 2 * LANES], preferred_element_type=F32)
        for half in range(2):
            j = 2 * jj + half
            acc = acc2[:, half * LANES:(half + 1) * LANES]
            hd = j % H
            if j < H:
                qt = (rope(acc) * q_scale).T
                qt_ref[hd, 0, 0] = jnp.where(first_map, qt, 0.0).astype(BF16)
                qt_ref[hd, 0, 1] = jnp.where(first_map, 0.0, qt).astype(BF16)
            elif j < 2 * H:
                k_ref[hd, 0] = rope(acc).astype(BF16)
            else:
                vt_ref[hd, 0, 0:DA_V_DIM, :] = acc.T.astype(BF16)
                vt_ref[hd, 0, DA_V_DIM:VT_ROWS, :] = ones_rows


def _qkv_proj(tokens, mod, w_qkv, cos, sa, sb, ctx_row):
    ctx_part, lat_part, lat_off = tokens
    B, _, D = lat_part.shape
    N = lat_part.shape[1] + lat_off * TM
    H = DA_HEADS
    kern = functools.partial(_qkv_kernel, d_model=D, q_scale=DA_HEAD_DIM ** -0.5 * math.log2(math.e))
    return pl.pallas_call(
        kern,
        grid=(B, N // TM),
        in_specs=_token_specs(lat_off, D) + [
                  pl.BlockSpec((1, 1, 6 * D), _mod_row(ctx_row)),
                  pl.BlockSpec((D, w_qkv.shape[1]), lambda b, t: (0, 0)),
                  pl.BlockSpec((TM, LANES), lambda b, t: (t, 0)),
                  pl.BlockSpec((TM, LANES), lambda b, t: (t, 0)),
                  pl.BlockSpec((TM, LANES), lambda b, t: (t, 0))],
        out_specs=[pl.BlockSpec((H, 1, 2, LANES, TM), lambda b, t: (0, b, 0, 0, t)),
                   pl.BlockSpec((H, 1, TM, LANES), lambda b, t: (0, b, t, 0)),
                   pl.BlockSpec((H, 1, VT_ROWS, TM), lambda b, t: (0, b, 0, t))],
        out_shape=[jax.ShapeDtypeStruct((H, B, 2, LANES, N), BF16),
                   jax.ShapeDtypeStruct((H, B, N, LANES), BF16),
                   jax.ShapeDtypeStruct((H, B, VT_ROWS, N), BF16)],
        compiler_params=_cparams("parallel", "parallel"),
        name="qkv_proj",
    )(ctx_part, lat_part, mod, w_qkv, cos, sa, sb)


def _rope_tables(n_ctx, n_lat):
    t = jnp.arange(n_lat)
    pos = jnp.stack([t // GRID_W, t % GRID_W], axis=1).astype(F32)
    axis_dims = DA_HEAD_DIM // 2
    inv = ROPE_THETA ** (-jnp.arange(0, axis_dims, 2, dtype=F32) / axis_dims)
    lane = jnp.arange(LANES)
    d = lane % DA_HEAD_DIM
    axis = d // axis_dims
    second = (d % axis_dims) // (axis_dims // 2)
    ang = pos[:, axis] * inv[d % (axis_dims // 2)][None, :]
    cos = jnp.cos(ang)
    sin = jnp.sin(ang)
    sa = jnp.where(second[None, :] == 0, -sin, 0.0)
    sb = jnp.where(second[None, :] == 1, sin, 0.0)
    pad = lambda a, v: jnp.concatenate([jnp.full((n_ctx, LANES), v, F32), a], axis=0)
    return pad(cos, 1.0), pad(sa, 0.0), pad(sb, 0.0)


def _attn_tiles(qt_ref, k_ref, vt_ref, bufs, n_chunks, tk):
    tq = qt_ref.shape[-1]
    heads = range(qt_ref.shape[0])

    def scores(h, off, slot):
        k = k_ref[h, 0, pl.ds(off, tk), :]
        for j in range(2):
            bufs[h][slot][j, 0:tk, :] = jnp.dot(k, qt_ref[h, 0, j], preferred_element_type=F32)

    def softmax_pv(h, off, slot, carry):
        vt = vt_ref[h, 0, :, pl.ds(off, tk)]
        new = []
        for j in range(2):
            m, acc = carry[j]
            s = bufs[h][slot][j, 0:tk, :]
            m_new = jnp.maximum(m, jnp.max(s, axis=0, keepdims=True))
            p = jnp.exp2(s - m_new).astype(BF16)
            acc = jnp.exp2(m - m_new) * acc + jnp.dot(vt, p, preferred_element_type=F32)
            new.append((m_new, acc))
        return tuple(new)

    carries = [tuple((jnp.full((1, tq), -jnp.inf, F32), jnp.zeros((VT_ROWS, tq), F32)) for _ in range(2))
               for _ in heads]
    for h in heads:
        scores(h, 0, 0)
    for c in range(n_chunks):
        for h in heads:
            if c + 1 < n_chunks:
                scores(h, (c + 1) * tk, (c + 1) % 2)
            carries[h] = softmax_pv(h, c * tk, c % 2, carries[h])
    return [tuple(acc[0:DA_V_DIM] / acc[DA_V_DIM:DA_V_DIM + 1] for _, acc in carries[h]) for h in heads]


def _attn_kernel(lam_ref, g_ref, qt_ref, k_ref, vt_ref, o_ref, *s_refs, n_ctx, n_all, tk, lambda_init):
    t = pl.program_id(2)
    bufs = [(s_refs[2 * h], s_refs[2 * h + 1]) for h in range(qt_ref.shape[0])]
    lp = lam_ref[...]
    lam = (jnp.exp(jnp.sum(lp[0:1] * lp[1:2], axis=-1, keepdims=True))
           - jnp.exp(jnp.sum(lp[2:3] * lp[3:4], axis=-1, keepdims=True)) + lambda_init)

    def finish(outs):
        for h, (o0, o1) in enumerate(outs):
            o = o0 - lam * o1
            ms = jnp.mean(o * o, axis=0, keepdims=True)
            o = o * lax.rsqrt(ms + RMS_EPS) * (g_ref[...] * (1.0 - lambda_init))
            o_ref[0, :, h * DA_V_DIM:(h + 1) * DA_V_DIM] = o.T.astype(o_ref.dtype)

    @pl.when(t == 0)
    def _():
        finish(_attn_tiles(qt_ref, k_ref, vt_ref, bufs, 1, n_ctx))

    @pl.when(t > 0)
    def _():
        finish(_attn_tiles(qt_ref, k_ref, vt_ref, bufs, n_all // tk, tk))


def _diff_attention(qt, k, vt, lam_params, subln_g, lambda_init, n_ctx):
    H, B, N, _ = k.shape
    tk = ATTN_TK
    nh = ATTN_HEADS
    assert n_ctx == TM and N % tk == 0 and n_ctx <= tk and H % nh == 0
    kern = functools.partial(_attn_kernel, n_ctx=n_ctx, n_all=N, tk=tk, lambda_init=lambda_init)
    return pl.pallas_call(
        kern,
        grid=(B, H // nh, N // TM),
        in_specs=[pl.BlockSpec(lam_params.shape, lambda b, h, t: (0, 0)),
                  pl.BlockSpec((DA_V_DIM, 1), lambda b, h, t: (0, 0)),
                  pl.BlockSpec((nh, 1, 2, LANES, TM), lambda b, h, t: (h, b, 0, 0, t)),
                  pl.BlockSpec((nh, 1, N, LANES), lambda b, h, t: (h, b, 0, 0)),
                  pl.BlockSpec((nh, 1, VT_ROWS, N), lambda b, h, t: (h, b, 0, 0))],
        out_specs=pl.BlockSpec((1, TM, nh * DA_V_DIM), lambda b, h, t: (b, t, h)),
        out_shape=jax.ShapeDtypeStruct((B, N, H * DA_V_DIM), BF16),
        scratch_shapes=[pltpu.VMEM((2, tk, TM), F32) for _ in range(2 * nh)],
        compiler_params=_cparams("parallel", "parallel", "parallel"),
        name="diff_attention",
    )(lam_params, subln_g.reshape(DA_V_DIM, 1), qt, k, vt)


ROUTE_ID, ROUTE_GATE, ROUTE_RANK = 0, MOE_TOP_K, 2 * MOE_TOP_K
ROUTE_ROWS = SUBLANES
ROUTER_EXPERT_LANE = SUBLANES


def _route_tile(lg, cnt_ref, n_experts):
    G, PER = MOE_GROUPS, MOE_PER_GROUP
    tm = lg.shape[0]
    lt = lg.T
    row = lax.broadcasted_iota(jnp.int32, (SUBLANES, tm), 0)
    ninf = -jnp.inf
    first = lambda hit: jnp.min(jnp.where(hit, row, SUBLANES), axis=0, keepdims=True)
    gl = jnp.where(row < G, lt[0:SUBLANES], ninf)
    gmax = jnp.max(gl, axis=0, keepdims=True)
    g_top = 1.0 / jnp.sum(jnp.exp(gl - gmax), axis=0, keepdims=True)
    g_idx = first(gl == gmax)
    el = lt[ROUTER_EXPERT_LANE:ROUTER_EXPERT_LANE + PER]
    for g in range(1, G):
        el = jnp.where(g_idx == g, lt[ROUTER_EXPERT_LANE + g * PER:ROUTER_EXPERT_LANE + (g + 1) * PER], el)
    e1 = jnp.max(el, axis=0, keepdims=True)
    i1 = first(el == e1)
    el2 = jnp.where(row == i1, ninf, el)
    e2 = jnp.max(el2, axis=0, keepdims=True)
    i2 = first(el2 == e2)
    id1 = g_idx * PER + i1
    id2 = g_idx * PER + i2
    w2 = jnp.exp(e2 - e1)
    den = 1.0 + w2
    erow = lax.broadcasted_iota(jnp.int32, (n_experts, tm), 0)
    hit1 = erow == id1
    hit2 = erow == id2
    onehot = jnp.logical_or(hit1, hit2)
    ui = lax.broadcasted_iota(jnp.int32, (tm, tm), 0)
    ti = lax.broadcasted_iota(jnp.int32, (tm, tm), 1)
    earlier = jnp.dot(onehot.astype(BF16), (ui < ti).astype(BF16), preferred_element_type=F32) + cnt_ref[...]
    r1 = jnp.sum(jnp.where(hit1, earlier, 0.0), axis=0, keepdims=True)
    r2 = jnp.sum(jnp.where(hit2, earlier, 0.0), axis=0, keepdims=True)
    cnt_ref[...] = cnt_ref[...] + jnp.sum(onehot.astype(F32), axis=1, keepdims=True)
    rec = jnp.zeros((ROUTE_ROWS, tm), F32)
    for r, val in ((ROUTE_ID, id1.astype(F32)), (ROUTE_ID + 1, id2.astype(F32)),
                   (ROUTE_GATE, g_top / den), (ROUTE_GATE + 1, g_top * w2 / den),
                   (ROUTE_RANK, r1), (ROUTE_RANK + 1, r2)):
        rec = jnp.where(row == r, val, rec)
    return rec


def _proj_ln_tail(a, c_ref, x_ref, mod_ref, w_ref, lng_ref, lnb_ref, wr_ref, br_ref, x_out, h_out, rt_out, cnt_out, *,
                  d_model, alpha, n_experts):
    D = d_model

    @pl.when(jnp.logical_and(pl.program_id(0) == 0, pl.program_id(1) == 0))
    def _():
        cnt_out[...] = jnp.zeros_like(cnt_out)

    hb = a.shape[0] // 2
    xres = _token_tile(c_ref, x_ref)
    os_ = [jnp.dot(a[r * hb:(r + 1) * hb], w_ref[...], preferred_element_type=F32) for r in range(2)]
    lgs = []
    for r in range(2):
        rows = slice(r * hb, (r + 1) * hb)
        res = alpha * xres[rows, :] + mod_ref[0, :, 2 * D:3 * D] * os_[r]
        xn = _layer_norm(res, lng_ref[...], lnb_ref[...])
        x_out[0, rows, :] = xn
        h2 = xn * (1.0 + mod_ref[0, :, 4 * D:5 * D]) + mod_ref[0, :, 3 * D:4 * D]
        h_out[0, rows, :] = h2.astype(h_out.dtype)
        lgs.append(jnp.dot(h2, wr_ref[...], precision=HIGHEST, preferred_element_type=F32) + br_ref[...])
    lg = jnp.concatenate(lgs, axis=0)
    rt_out[0] = _route_tile(lg, cnt_out, n_experts)


def _attn_out_kernel(a_ref, *refs, **kw):
    _proj_ln_tail(a_ref[0], *refs, **kw)


def _ssm_out_kernel(yf_ref, yb_ref, z_ref, ng_ref, *refs, **kw):
    y = (yf_ref[0].astype(F32) + yb_ref[0].astype(F32)).T
    gz = y * _silu(z_ref[0].astype(F32))
    ms = jnp.mean(gz * gz, axis=-1, keepdims=True)
    a = (gz * lax.rsqrt(ms + RMS_EPS) * ng_ref[...]).astype(BF16)
    _proj_ln_tail(a, *refs, **kw)


def _proj_ln(kernel_fn, lead_args, lead_specs, tokens, mod, w, ln_g, ln_b, w_router, b_router, ctx_row, alpha,
             n_experts, name):
    ctx_part, lat_part, lat_off = tokens
    B, _, D = lat_part.shape
    N = lat_part.shape[1] + lat_off * TM
    row = lambda b, t: (b, t, 0)
    const2 = lambda b, t: (0, 0)
    kern = functools.partial(kernel_fn, d_model=D, alpha=alpha, n_experts=n_experts)
    return pl.pallas_call(
        kern,
        grid=(B, N // TM),
        in_specs=lead_specs + _token_specs(lat_off, D) + [
            pl.BlockSpec((1, 1, 6 * D), _mod_row(ctx_row)),
            pl.BlockSpec(w.shape, const2),
            pl.BlockSpec((1, D), const2),
            pl.BlockSpec((1, D), const2),
            pl.BlockSpec(w_router.shape, const2),
            pl.BlockSpec((1, LANES), const2)],
        out_specs=[pl.BlockSpec((1, TM, D), row), pl.BlockSpec((1, TM, D), row),
                   pl.BlockSpec((1, ROUTE_ROWS, TM), lambda b, t: (b, 0, t)), pl.BlockSpec((n_experts, 1), const2)],
        out_shape=[jax.ShapeDtypeStruct((B, N, D), F32), jax.ShapeDtypeStruct((B, N, D), BF16),
                   jax.ShapeDtypeStruct((B, ROUTE_ROWS, N), F32), jax.ShapeDtypeStruct((n_experts, 1), F32)],
        compiler_params=_cparams("arbitrary", "arbitrary"),
        name=name,
    )(*lead_args, ctx_part, lat_part, mod, w, ln_g.reshape(1, D), ln_b.reshape(1, D), w_router, b_router)


def _moe_kernel(be_ref, nu_ref, x_ref, wg_ref, wu_ref, wd_ref, o_ref, wgu_bf, wd_bf, *, d_ff):
    i = pl.program_id(0)
    active = i < nu_ref[0]
    new_expert = jnp.logical_or(i == 0, be_ref[i] != be_ref[jnp.maximum(i - 1, 0)])

    @pl.when(jnp.logical_and(active, new_expert))
    def _():
        wgu_bf[:, 0:d_ff] = wg_ref[0, 0].astype(BF16)
        wgu_bf[:, d_ff:] = wu_ref[0, 0].astype(BF16)
        wd_bf[...] = wd_ref[0, 0].astype(BF16)

    @pl.when(active)
    def _():
        hb = x_ref.shape[0] // 2
        gus = [jnp.dot(x_ref[r * hb:(r + 1) * hb, :], wgu_bf[...], preferred_element_type=F32) for r in range(2)]
        for r in range(2):
            mid = (_silu(gus[r][:, :d_ff]) * gus[r][:, d_ff:]).astype(BF16)
            o_ref[r * hb:(r + 1) * hb, :] = jnp.dot(mid, wd_bf[...], preferred_element_type=F32).astype(o_ref.dtype)

    @pl.when(jnp.logical_not(active))
    def _():
        o_ref[...] = jnp.zeros_like(o_ref)


def _moe_experts(buf, blk_expert, n_used, w_gate, w_up, w_down, layer):
    n_rows, D = buf.shape
    d_ff = w_down.shape[2]
    grid_spec = pltpu.PrefetchScalarGridSpec(
        num_scalar_prefetch=2,
        grid=(n_rows // MOE_BM,),
        in_specs=[pl.BlockSpec((MOE_BM, D), lambda i, be, nu: (i, 0)),
                  pl.BlockSpec((1, 1, D, d_ff), lambda i, be, nu: (layer, be[i], 0, 0)),
                  pl.BlockSpec((1, 1, D, d_ff), lambda i, be, nu: (layer, be[i], 0, 0)),
                  pl.BlockSpec((1, 1, d_ff, D), lambda i, be, nu: (layer, be[i], 0, 0))],
        out_specs=pl.BlockSpec((MOE_BM, D), lambda i, be, nu: (i, 0)),
        scratch_shapes=[pltpu.VMEM((D, 2 * d_ff), BF16), pltpu.VMEM((d_ff, D), BF16)])
    return pl.pallas_call(
        functools.partial(_moe_kernel, d_ff=d_ff),
        grid_spec=grid_spec,
        out_shape=jax.ShapeDtypeStruct((n_rows, D), BF16),
        compiler_params=_cparams("arbitrary"),
        name="moe_experts",
    )(blk_expert, n_used, buf, w_gate, w_up, w_down)


def _dispatch(eid, rank, sizes):
    K, T = eid.shape
    A = T * K
    E = sizes.shape[0]
    padded = (sizes + MOE_BM - 1) // MOE_BM * MOE_BM
    pad_end = jnp.cumsum(padded)
    pad_start = pad_end - padded
    starts = jnp.cumsum(sizes) - sizes
    hit = eid[:, :, None] == jnp.arange(E, dtype=jnp.int32)
    dest = jnp.sum(jnp.where(hit, pad_start, 0), axis=-1) + rank
    n_blocks = -(-A // MOE_BM) + E
    blk_start = jnp.arange(n_blocks, dtype=jnp.int32) * MOE_BM
    blk_expert = jnp.minimum(jnp.sum(blk_start[:, None] >= pad_end[None, :], axis=1), E - 1).astype(jnp.int32)
    n_used = (pad_end[-1] // MOE_BM).astype(jnp.int32).reshape(1)
    order = jnp.argsort(eid.T.reshape(A))
    within = jnp.arange(MOE_BM, dtype=jnp.int32)[None, :] + (blk_start - pad_start[blk_expert])[:, None]
    src = jnp.clip(starts[blk_expert][:, None] + within, 0, A - 1)
    filler = (blk_start[:, None] + jnp.arange(MOE_BM, dtype=jnp.int32)[None, :]) % T
    slot_tok = jnp.where(within < sizes[blk_expert][:, None], jnp.take(order, src, mode='clip') // K, filler)
    return slot_tok.reshape(-1).astype(jnp.int32), blk_expert, n_used, dest.astype(jnp.int32)


def _combine_kernel(x_ref, y0_ref, y1_ref, rt_ref, mod_ref, lng_ref, lnb_ref, o_ref, *, d_model, alpha):
    D = d_model
    rt = rt_ref[0].T
    f = (rt[:, ROUTE_GATE:ROUTE_GATE + 1] * y0_ref[0, 0].astype(F32)
         + rt[:, ROUTE_GATE + 1:ROUTE_GATE + 2] * y1_ref[0, 0].astype(F32))
    r = alpha * x_ref[0] + mod_ref[0, :, 5 * D:6 * D] * f
    o_ref[0] = _layer_norm(r, lng_ref[...], lnb_ref[...])


def _combine_ln(x_all, ys, route, mod, ln_g, ln_b, ctx_row, alpha, t0):
    B, N, D = x_all.shape
    nt = N // TM - t0
    tiles_per_batch = N // TM
    row = lambda b, t: (b, t + t0, 0)
    const2 = lambda b, t: (0, 0)
    mrow = _mod_row(ctx_row)
    return pl.pallas_call(
        functools.partial(_combine_kernel, d_model=D, alpha=alpha),
        grid=(B, nt),
        in_specs=[pl.BlockSpec((1, TM, D), row),
                  pl.BlockSpec((1, 1, TM, D), lambda b, t: (0, b * tiles_per_batch + t + t0, 0, 0)),
                  pl.BlockSpec((1, 1, TM, D), lambda b, t: (1, b * tiles_per_batch + t + t0, 0, 0)),
                  pl.BlockSpec((1, ROUTE_ROWS, TM), lambda b, t: (b, 0, t + t0)),
                  pl.BlockSpec((1, 1, 6 * D), lambda b, t: mrow(b, t + t0)),
                  pl.BlockSpec((1, D), const2),
                  pl.BlockSpec((1, D), const2)],
        out_specs=pl.BlockSpec((1, TM, D), lambda b, t: (b, t, 0)),
        out_shape=jax.ShapeDtypeStruct((B, nt * TM, D), F32),
        compiler_params=_cparams("parallel", "parallel"),
        name="moe_combine_ln",
    )(x_all, ys, ys, route, mod, ln_g.reshape(1, D), ln_b.reshape(1, D))


def _moe_layer(x1, h2, route, counts, mod, w_gate, w_up, w_down, layer, ln_g, ln_b, ctx_row, alpha, t0, defer=False):
    B, N, D = x1.shape
    T = B * N
    by_row = lambda r: jnp.swapaxes(route[:, r:r + MOE_TOP_K, :], 0, 1).reshape(MOE_TOP_K, T).astype(jnp.int32)
    eid = by_row(ROUTE_ID)
    rank = by_row(ROUTE_RANK)
    sizes = counts[:, 0].astype(jnp.int32)
    slot_tok, blk_expert, n_used, dest = _dispatch(eid, rank, sizes)
    buf = jnp.take(h2.reshape(T, D), slot_tok, axis=0, mode='clip')
    yb = _moe_experts(buf, blk_expert, n_used, w_gate, w_up, w_down, layer)
    ys = jnp.take(yb, dest, axis=0, mode='clip').reshape(MOE_TOP_K, T // TM, TM, D)
    if defer:
        return (x1, ys, route, mod, ln_g, ln_b)
    return _combine_ln(x1, ys, route, mod, ln_g, ln_b, ctx_row, alpha, t0)


def _softplus(v):
    return jnp.maximum(v, 0.0) + jnp.log1p(jnp.exp(-jnp.abs(v)))


def _in_proj_rows(h, rows, w_ref, dtb_ref, z_out, xbc_out, dt_out, dtt_out, d_inner, d_conv):
    z_out[0, rows, :] = jnp.dot(h, w_ref[:, 0:d_inner], preferred_element_type=F32).astype(z_out.dtype)
    cw = 512
    for j in range(d_conv // cw):
        xbc_out[0, rows, j * cw:(j + 1) * cw] = jnp.dot(
            h, w_ref[:, d_inner + j * cw:d_inner + (j + 1) * cw], preferred_element_type=F32).astype(xbc_out.dtype)
    dt = _softplus(jnp.dot(h, w_ref[:, d_inner + d_conv:], preferred_element_type=F32) + dtb_ref[...])
    dt_out[0, rows, :] = dt
    dtt_out[0, :, rows] = dt.T


def _ssm_in_kernel(x_ref, mod_ref, w_ref, dtb_ref, z_out, xbc_out, dt_out, dtt_out, *, d_model, d_inner, d_conv):
    D = d_model
    h = (x_ref[0] * (1.0 + mod_ref[0, :, D:2 * D]) + mod_ref[0, :, 0:D]).astype(BF16)
    _in_proj_rows(h, slice(0, TM), w_ref, dtb_ref, z_out, xbc_out, dt_out, dtt_out, d_inner, d_conv)


def _ssm_in_combine_kernel(x1_ref, y0_ref, y1_ref, rt_ref, modp_ref, lng_ref, lnb_ref, mod_ref, w_ref, dtb_ref,
                           x_out, z_out, xbc_out, dt_out, dtt_out, *, d_model, d_inner, d_conv, alpha):
    D = d_model
    rt = rt_ref[0].T
    hb = TM // 2
    hs = []
    for r in range(2):
        rows = slice(r * hb, (r + 1) * hb)
        f = (rt[rows, ROUTE_GATE:ROUTE_GATE + 1] * y0_ref[0, 0, rows, :].astype(F32)
             + rt[rows, ROUTE_GATE + 1:ROUTE_GATE + 2] * y1_ref[0, 0, rows, :].astype(F32))
        res = alpha * x1_ref[0, rows, :] + modp_ref[0, :, 5 * D:6 * D] * f
        xn = _layer_norm(res, lng_ref[...], lnb_ref[...])
        x_out[0, rows, :] = xn
        hs.append((xn * (1.0 + mod_ref[0, :, D:2 * D]) + mod_ref[0, :, 0:D]).astype(BF16))
    for r in range(2):
        _in_proj_rows(hs[r], slice(r * hb, (r + 1) * hb), w_ref, dtb_ref, z_out, xbc_out, dt_out, dtt_out,
                      d_inner, d_conv)


def _ssm_in_proj(x_all, mod, w_in_p, dt_bias_p, ctx_row, d_inner, d_conv, pending=None, alpha=None):
    B, N, D = (pending[0] if pending is not None else x_all).shape
    row = lambda b, t: (b, t, 0)
    const2 = lambda b, t: (0, 0)
    tiles = N // TM
    proj_specs = [pl.BlockSpec((1, 1, 6 * D), _mod_row(ctx_row)),
                  pl.BlockSpec(w_in_p.shape, const2),
                  pl.BlockSpec((1, LANES), const2)]
    out_specs = [pl.BlockSpec((1, TM, d_inner), row), pl.BlockSpec((1, TM, d_conv), row),
                 pl.BlockSpec((1, TM, LANES), row), pl.BlockSpec((1, LANES, TM), lambda b, t: (b, 0, t))]
    out_shape = [jax.ShapeDtypeStruct((B, N, d_inner), BF16), jax.ShapeDtypeStruct((B, N, d_conv), BF16),
                 jax.ShapeDtypeStruct((B, N, LANES), F32), jax.ShapeDtypeStruct((B, LANES, N), F32)]
    if pending is None:
        kern = functools.partial(_ssm_in_kernel, d_model=D, d_inner=d_inner, d_conv=d_conv)
        in_specs = [pl.BlockSpec((1, TM, D), row)] + proj_specs
        args = (x_all, mod, w_in_p, dt_bias_p)
    else:
        x1, ys, route, mod_prev, ln_g, ln_b = pending
        kern = functools.partial(_ssm_in_combine_kernel, d_model=D, d_inner=d_inner, d_conv=d_conv, alpha=alpha)
        in_specs = [pl.BlockSpec((1, TM, D), row),
                    pl.BlockSpec((1, 1, TM, D), lambda b, t: (0, b * tiles + t, 0, 0)),
                    pl.BlockSpec((1, 1, TM, D), lambda b, t: (1, b * tiles + t, 0, 0)),
                    pl.BlockSpec((1, ROUTE_ROWS, TM), lambda b, t: (b, 0, t)),
                    pl.BlockSpec((1, 1, 6 * D), _mod_row(ctx_row)),
                    pl.BlockSpec((1, D), const2),
                    pl.BlockSpec((1, D), const2)] + proj_specs
        args = (x1, ys, ys, route, mod_prev, ln_g.reshape(1, D), ln_b.reshape(1, D), mod, w_in_p, dt_bias_p)
        out_specs = [pl.BlockSpec((1, TM, D), row)] + out_specs
        out_shape = [jax.ShapeDtypeStruct((B, N, D), F32)] + out_shape
    return pl.pallas_call(
        kern,
        grid=(B, tiles),
        in_specs=in_specs,
        out_specs=out_specs,
        out_shape=out_shape,
        compiler_params=_cparams("parallel", "parallel"),
        name="ssm_in_proj",
    )(*args)


CONV_HALO = 16


def _conv_kernel(x_ref, prev_ref, next_ref, w_ref, b_ref, xt_ref, bc_ref, ext_ref, *, n_tiles, d_inner):
    t = pl.program_id(1)
    halo = CONV_HALO
    pad = SSM_CONV // 2
    has_prev = t >= 2
    has_next = jnp.logical_and(t >= 1, t < n_tiles - 1)
    ext_ref[0:halo, :] = jnp.where(has_prev, prev_ref[0].astype(F32), 0.0)
    ext_ref[halo:halo + TM, :] = x_ref[0].astype(F32)
    ext_ref[halo + TM:, :] = jnp.where(has_next, next_ref[0].astype(F32), 0.0)
    rows = TM + 2 * halo
    cw = LANES
    for j in range(ext_ref.shape[1] // cw):
        cols = slice(j * cw, (j + 1) * cw)
        ext = ext_ref[:, cols]
        acc = b_ref[:, cols] + w_ref[pad:pad + 1, cols] * ext[halo:halo + TM]
        for k in range(SSM_CONV):
            if k != pad:
                acc = acc + w_ref[k:k + 1, cols] * pltpu.roll(ext, (pad - k) % rows, 0)[halo:halo + TM]
        act = _silu(acc)
        if j * cw < d_inner:
            xt_ref[0, j * cw:(j + 1) * cw, :] = act.T.astype(xt_ref.dtype)
        else:
            bc_ref[0, :, j * cw - d_inner:(j + 1) * cw - d_inner] = act.astype(bc_ref.dtype)


def _ssm_conv(xbc, conv_w_p, conv_b, d_inner):
    B, N, C = xbc.shape
    n_tiles = N // TM
    hb = TM // CONV_HALO
    last_hb = N // CONV_HALO - 1
    return pl.pallas_call(
        functools.partial(_conv_kernel, n_tiles=n_tiles, d_inner=d_inner),
        grid=(B, n_tiles),
        in_specs=[pl.BlockSpec((1, TM, C), lambda b, t: (b, t, 0)),
                  pl.BlockSpec((1, CONV_HALO, C), lambda b, t: (b, jnp.maximum(t * hb - 1, 0), 0)),
                  pl.BlockSpec((1, CONV_HALO, C), lambda b, t: (b, jnp.minimum((t + 1) * hb, last_hb), 0)),
                  pl.BlockSpec(conv_w_p.shape, lambda b, t: (0, 0)),
                  pl.BlockSpec((1, C), lambda b, t: (0, 0))],
        out_specs=[pl.BlockSpec((1, d_inner, TM), lambda b, t: (b, 0, t)),
                   pl.BlockSpec((1, TM, C - d_inner), lambda b, t: (b, t, 0))],
        out_shape=[jax.ShapeDtypeStruct((B, d_inner, N), BF16), jax.ShapeDtypeStruct((B, N, C - d_inner), BF16)],
        scratch_shapes=[pltpu.VMEM((TM + 2 * CONV_HALO, C), F32)],
        compiler_params=_cparams("parallel", "parallel"),
        name="ssm_conv",
    )(xbc, xbc, xbc, conv_w_p, conv_b.reshape(1, C))


def _ssd_chunk(sub, xt_ref, bc_ref, dt_ref, dtt_ref, alr_ref, alc_ref, dsk_ref, y_ref, state_ref, xw_ref, *,
               direction, n_heads):
    L = SSM_CHUNK
    P = SSM_HEAD_DIM
    NS = SSM_STATE
    hpg = n_heads // SSM_GROUPS
    gw = hpg * P
    reverse = direction == 1
    l0 = direction * n_heads

    tsl = slice(sub * L, (sub + 1) * L)
    ri = lax.broadcasted_iota(jnp.int32, (L, L), 0)
    ci = lax.broadcasted_iota(jnp.int32, (L, L), 1)
    before = (ri >= ci) if reverse else (ri <= ci)
    after = (ci >= ri) if reverse else (ci <= ri)
    a = jnp.dot(after.astype(F32), dt_ref[0, tsl, :] * -jnp.exp(alr_ref[...]), precision=HIGHEST,
                preferred_element_type=F32)
    at = jnp.dot(dtt_ref[0, :, tsl] * -jnp.exp(alc_ref[...]), before.astype(F32), precision=HIGHEST,
                 preferred_element_type=F32)[l0:l0 + n_heads]
    last = 0 if reverse else L - 1
    tot = jnp.broadcast_to(at[:, last:last + 1], (n_heads, L))
    ea = jnp.exp(at)
    wgt = jnp.exp(tot - at)
    etot = jnp.exp(tot)
    dtt = dtt_ref[0, l0:l0 + n_heads, tsl]
    dsk = dsk_ref[direction]

    for g in range(SSM_GROUPS):
        bg = bc_ref[0, tsl, g * NS:(g + 1) * NS]
        cg = bc_ref[0, tsl, (SSM_GROUPS + g) * NS:(SSM_GROUPS + g + 1) * NS]
        cbt = lax.dot_general(bg, cg, (((1,), (1,)), ((), ())), preferred_element_type=F32)
        yoff = lax.dot_general(state_ref[g * gw:(g + 1) * gw, :].astype(BF16), cg, (((1,), (1,)), ((), ())),
                               preferred_element_type=F32)
        for hh in range(hpg):
            h = g * hpg + hh
            rows = slice(h * P, (h + 1) * P)
            decay = jnp.where(before, jnp.exp(at[h:h + 1, :] - a[:, l0 + h:l0 + h + 1]), 0.0)
            mt = (cbt * decay).astype(BF16)
            xh = xt_ref[0, rows, tsl].astype(F32)
            xdt = xh * dtt[h:h + 1, :]
            y = (jnp.dot(xdt.astype(BF16), mt, preferred_element_type=F32)
                 + yoff[hh * P:(hh + 1) * P, :] * ea[h:h + 1, :] + xh * dsk[h:h + 1, :])
            y_ref[0, rows, tsl] = y.astype(y_ref.dtype)
            xw_ref[hh * P:(hh + 1) * P, :] = (xdt * wgt[h:h + 1, :]).astype(xw_ref.dtype)
        upd = jnp.dot(xw_ref[...], bg, preferred_element_type=F32)
        for hh in range(hpg):
            h = g * hpg + hh
            rows = slice(h * P, (h + 1) * P)
            state_ref[rows, :] = state_ref[rows, :] * etot[h:h + 1, :] + upd[hh * P:(hh + 1) * P, :]


def _ssd_kernel(*refs, direction, n_heads):
    state_ref = refs[-2]

    @pl.when(pl.program_id(1) == 0)
    def _():
        state_ref[...] = jnp.zeros_like(state_ref)

    for sub in ((1, 0) if direction == 1 else (0, 1)):
        _ssd_chunk(sub, *refs, direction=direction, n_heads=n_heads)


def _ssd_scan(xt, bc, dt, dtt, a_log, d_skip, direction, n_ctx):
    B, d_inner, N = xt.shape
    n_heads = a_log.shape[1]
    L = SSM_CHUNK
    LB = 2 * L
    assert n_ctx % LB == 0 and N % LB == 0
    nc = N // LB
    ncc = n_ctx // LB
    if direction == 0:
        chunk = lambda c: c
    else:
        chunk = lambda c: jnp.where(c < ncc, ncc - 1 - c, nc - 1 - (c - ncc))
    kern = functools.partial(_ssd_kernel, direction=direction, n_heads=n_heads)
    a_log_lanes = jnp.pad(a_log.reshape(-1), (0, LANES - a_log.size))
    d_skip_lanes = jnp.broadcast_to(d_skip.astype(F32)[:, :, None], d_skip.shape + (L,))
    return pl.pallas_call(
        kern,
        grid=(B, nc),
        in_specs=[pl.BlockSpec((1, d_inner, LB), lambda b, c: (b, 0, chunk(c))),
                  pl.BlockSpec((1, LB, bc.shape[2]), lambda b, c: (b, chunk(c), 0)),
                  pl.BlockSpec((1, LB, LANES), lambda b, c: (b, chunk(c), 0)),
                  pl.BlockSpec((1, LANES, LB), lambda b, c: (b, 0, chunk(c))),
                  pl.BlockSpec((1, LANES), lambda b, c: (0, 0)),
                  pl.BlockSpec((LANES, 1), lambda b, c: (0, 0)),
                  pl.BlockSpec(d_skip_lanes.shape, lambda b, c: (0, 0, 0))],
        out_specs=pl.BlockSpec((1, d_inner, LB), lambda b, c: (b, 0, chunk(c))),
        out_shape=jax.ShapeDtypeStruct((B, d_inner, N), BF16),
        scratch_shapes=[pltpu.VMEM((d_inner, SSM_STATE), F32),
                        pltpu.VMEM((d_inner // SSM_GROUPS, L), BF16)],
        compiler_params=_cparams("parallel", "arbitrary"),
        name=f"ssd_scan_dir{direction}",
    )(xt, bc, dt, dtt, a_log_lanes.reshape(1, LANES), a_log_lanes.reshape(LANES, 1), d_skip_lanes)


def kernel(x, c, ctx, c_ctx, w_mod, b_mod, ln1_g, ln1_b, ln2_g, ln2_b, attn_w_qkv, attn_w_o, attn_lq1, attn_lk1, attn_lq2, attn_lk2, attn_subln_g, ssm_w_in, ssm_conv_w, ssm_conv_b, ssm_dt_bias, ssm_a_log, ssm_d, ssm_norm_g, ssm_w_out, moe_w_group, moe_b_group, moe_w_expert, moe_b_expert, moe_w_gate, moe_w_up, moe_w_down):
    B, S, D = x.shape
    C = ctx.shape[1]
    depth = w_mod.shape[0]
    E = moe_w_expert.shape[-1]
    assert C == TM and S % TM == 0 and B < SUBLANES
    alpha = (2 * depth) ** 0.25
    ctx_row = B

    cond = jnp.zeros((SUBLANES, D), F32).at[:B].set(c).at[B].set(c_ctx)
    mods = _modulation(cond, w_mod, b_mod)
    tokens = (ctx, x, 1)
    x_all = None
    cos, sa, sb = _rope_tables(C, S)

    pending = None
    for i in range(depth):
        last = i == depth - 1
        j = i // N_MIXERS
        mod = mods[i].reshape(SUBLANES, 1, 6 * D)
        lane_pad = lambda a, n: jnp.pad(a, [(0, 0)] * (a.ndim - 1) + [(0, n - a.shape[-1])])
        w_router = jnp.concatenate([lane_pad(moe_w_group[i], ROUTER_EXPERT_LANE),
                                    lane_pad(moe_w_expert[i], LANES - ROUTER_EXPERT_LANE)], axis=-1)
        b_router = jnp.concatenate([lane_pad(moe_b_group[i], ROUTER_EXPERT_LANE),
                                    lane_pad(moe_b_expert[i], LANES - ROUTER_EXPERT_LANE)]).reshape(1, LANES)
        if i % N_MIXERS == 0:
            lambda_init = 0.8 - 0.6 * math.exp(-0.3 * i)
            qt, k, vt = _qkv_proj(tokens, mod, attn_w_qkv[j].astype(BF16), cos, sa, sb, ctx_row)
            lam_params = jnp.stack([attn_lq1[j], attn_lk1[j], attn_lq2[j], attn_lk2[j]])
            o = _diff_attention(qt, k, vt, lam_params, attn_subln_g[j], lambda_init, C)
            lead_args = [o]
            lead_specs = [pl.BlockSpec((1, TM, o.shape[-1]), lambda b, t: (b, t, 0))]
            x1, h2, route, counts = _proj_ln(_attn_out_kernel, lead_args, lead_specs, tokens, mod,
                                             attn_w_o[j].astype(BF16), ln1_g[i], ln1_b[i], w_router, b_router,
                                             ctx_row, alpha, E, "attn_out_ln")
        else:
            n_heads = ssm_a_log.shape[-1]
            d_inner = n_heads * SSM_HEAD_DIM
            d_conv = d_inner + 2 * SSM_GROUPS * SSM_STATE
            w_in = ssm_w_in[j]
            n_dt = w_in.shape[1] - d_inner - d_conv
            w_in_p = jnp.pad(w_in, ((0, 0), (0, LANES - n_dt))).astype(BF16)
            dt_bias_p = jnp.pad(ssm_dt_bias[j].reshape(1, n_dt), ((0, 0), (0, LANES - n_dt)))
            if pending is None:
                if x_all is None:
                    x_all = jnp.concatenate([ctx, x], axis=1)
                z, xbc, dt, dtt = _ssm_in_proj(x_all, mod, w_in_p, dt_bias_p, ctx_row, d_inner, d_conv)
            else:
                x_all, z, xbc, dt, dtt = _ssm_in_proj(None, mod, w_in_p, dt_bias_p, ctx_row, d_inner, d_conv,
                                                      pending=pending, alpha=alpha)
                pending = None
            conv_w_p = jnp.pad(ssm_conv_w[j], ((0, SUBLANES - SSM_CONV), (0, 0)))
            xt, bc = _ssm_conv(xbc, conv_w_p, ssm_conv_b[j], d_inner)
            y_f = _ssd_scan(xt, bc, dt, dtt, ssm_a_log[j], ssm_d[j], 0, C)
            y_b = _ssd_scan(xt, bc, dt, dtt, ssm_a_log[j], ssm_d[j], 1, C)
            row = lambda b, t: (b, t, 0)
            col = lambda b, t: (b, 0, t)
            lead_args = [y_f, y_b, z, ssm_norm_g[j].reshape(1, d_inner)]
            lead_specs = [pl.BlockSpec((1, d_inner, TM), col), pl.BlockSpec((1, d_inner, TM), col),
                          pl.BlockSpec((1, TM, d_inner), row), pl.BlockSpec((1, d_inner), lambda b, t: (0, 0))]
            x1, h2, route, counts = _proj_ln(_ssm_out_kernel, lead_args, lead_specs, (x_all, x_all, 0), mod,
                                             ssm_w_out[j].astype(BF16), ln1_g[i], ln1_b[i], w_router, b_router,
                                             ctx_row, alpha, E, "ssm_out_ln")
        defer = (not last) and (i + 1) % N_MIXERS == 1
        res = _moe_layer(x1, h2, route, counts, mod, moe_w_gate, moe_w_up, moe_w_down, i,
                         ln2_g[i], ln2_b[i], ctx_row, alpha, 1 if last else 0, defer=defer)
        if defer:
            pending = res
        else:
            x_all = res
            tokens = (x_all, x_all, 0)
    return x_all
```

```python
import functools
import math

import jax
import jax.numpy as jnp
from jax import lax
from jax.experimental import pallas as pl
from jax.experimental.pallas import tpu as pltpu

F32 = jnp.float32
BF16 = jnp.bfloat16
HIGHEST = lax.Precision.HIGHEST

GRID_W = 64
DA_HEADS = 8
DA_HEAD_DIM = 64
DA_V_DIM = 2 * DA_HEAD_DIM
ROPE_THETA = 10000.0
SSM_HEAD_DIM = 64
SSM_GROUPS = 4
SSM_STATE = 128
SSM_CONV = 5
SSM_CHUNK = 128
MOE_GROUPS = 4
MOE_PER_GROUP = 8
MOE_TOP_K = 2
LN_EPS = 1e-5
RMS_EPS = 1e-5
N_MIXERS = 2

LANES = 128
SUBLANES = 8
TM = 256
MOE_BM = 256
ATTN_TK = 2816
ATTN_HEADS = 2
VT_ROWS = DA_V_DIM + 16
VMEM_LIMIT = 56 * 1024 * 1024


def _cparams(*sem):
    return pltpu.CompilerParams(dimension_semantics=sem, vmem_limit_bytes=VMEM_LIMIT)


def _silu(v):
    return v / (1.0 + jnp.exp(-v))


def _layer_norm(r, g, b):
    mu = jnp.mean(r, axis=-1, keepdims=True)
    d = r - mu
    var = jnp.mean(d * d, axis=-1, keepdims=True)
    return d * lax.rsqrt(var + LN_EPS) * g + b


def _token_specs(lat_off, D):
    return [pl.BlockSpec((1, TM, D), lambda b, t: (b, 0, 0)),
            pl.BlockSpec((1, TM, D), lambda b, t: (b, jnp.maximum(t - lat_off, 0), 0))]


def _token_tile(c_ref, x_ref):
    return jnp.where(pl.program_id(1) == 0, c_ref[0], x_ref[0])


def _mod_row(ctx_row):
    return lambda b, t: (jnp.where(t == 0, ctx_row, b), 0, 0)


def _mod_kernel(c_ref, w_ref, b_ref, o_ref):
    s = _silu(c_ref[...])
    o_ref[0] = jnp.dot(s, w_ref[0], precision=HIGHEST, preferred_element_type=F32) + b_ref[0]


def _modulation(cond, w_mod, b_mod):
    L, D, D6 = w_mod.shape
    R = cond.shape[0]
    tn = 1536
    return pl.pallas_call(
        _mod_kernel,
        grid=(L, D6 // tn),
        in_specs=[pl.BlockSpec((R, D), lambda l, j: (0, 0)),
                  pl.BlockSpec((1, D, tn), lambda l, j: (l, 0, j)),
                  pl.BlockSpec((1, 1, tn), lambda l, j: (l, 0, j))],
        out_specs=pl.BlockSpec((1, R, tn), lambda l, j: (l, 0, j)),
        out_shape=jax.ShapeDtypeStruct((L, R, D6), F32),
        compiler_params=_cparams("parallel", "parallel"),
        name="modulation",
    )(cond, w_mod, b_mod.reshape(L, 1, D6))


def _qkv_kernel(c_ref, x_ref, mod_ref, w_ref, cos_ref, sa_ref, sb_ref, qt_ref, k_ref, vt_ref, *, d_model, q_scale):
    D = d_model
    H = k_ref.shape[0]
    x = _token_tile(c_ref, x_ref)
    h = (x * (1.0 + mod_ref[0, :, D:2 * D]) + mod_ref[0, :, 0:D]).astype(BF16)
    cos = cos_ref[...]
    sa = sa_ref[...]
    sb = sb_ref[...]
    first_map = lax.broadcasted_iota(jnp.int32, (LANES, TM), 0) < DA_HEAD_DIM
    ones_rows = (lax.broadcasted_iota(jnp.int32, (VT_ROWS - DA_V_DIM, TM), 0) == 0).astype(BF16)

    def rope(a):
        return a * cos + pltpu.roll(a, LANES - 16, 1) * sa + pltpu.roll(a, 16, 1) * sb

    for jj in range(3 * H // 2):
        acc2 = jnp.dot(h, w_ref[:, jj * 2 * LANES:(jj + 1) * 2 * LANES], preferred_element_type=F32)
        for half in range(2):
            j = 2 * jj + half
            acc = acc2[:, half * LANES:(half + 1) * LANES]
            hd = j % H
            if j < H:
                qt = (rope(acc) * q_scale).T
                qt_ref[hd, 0, 0] = jnp.where(first_map, qt, 0.0).astype(BF16)
                qt_ref[hd, 0, 1] = jnp.where(first_map, 0.0, qt).astype(BF16)
            elif j < 2 * H:
                k_ref[hd, 0] = rope(acc).astype(BF16)
            else:
                vt_ref[hd, 0, 0:DA_V_DIM, :] = acc.T.astype(BF16)
                vt_ref[hd, 0, DA_V_DIM:VT_ROWS, :] = ones_rows


def _qkv_proj(tokens, mod, w_qkv, cos, sa, sb, ctx_row):
    ctx_part, lat_part, lat_off = tokens
    B, _, D = lat_part.shape
    N = lat_part.shape[1] + lat_off * TM
    H = DA_HEADS
    kern = functools.partial(_qkv_kernel, d_model=D, q_scale=DA_HEAD_DIM ** -0.5 * math.log2(math.e))
    return pl.pallas_call(
        kern,
        grid=(B, N // TM),
        in_specs=_token_specs(lat_off, D) + [
                  pl.BlockSpec((1, 1, 6 * D), _mod_row(ctx_row)),
                  pl.BlockSpec((D, w_qkv.shape[1]), lambda b, t: (0, 0)),
                  pl.BlockSpec((TM, LANES), lambda b, t: (t, 0)),
                  pl.BlockSpec((TM, LANES), lambda b, t: (t, 0)),
                  pl.BlockSpec((TM, LANES), lambda b, t: (t, 0))],
        out_specs=[pl.BlockSpec((H, 1, 2, LANES, TM), lambda b, t: (0, b, 0, 0, t)),
                   pl.BlockSpec((H, 1, TM, LANES), lambda b, t: (0, b, t, 0)),
                   pl.BlockSpec((H, 1, VT_ROWS, TM), lambda b, t: (0, b, 0, t))],
        out_shape=[jax.ShapeDtypeStruct((H, B, 2, LANES, N), BF16),
                   jax.ShapeDtypeStruct((H, B, N, LANES), BF16),
                   jax.ShapeDtypeStruct((H, B, VT_ROWS, N), BF16)],
        compiler_params=_cparams("parallel", "parallel"),
        name="qkv_proj",
    )(ctx_part, lat_part, mod, w_qkv, cos, sa, sb)


def _rope_tables(n_ctx, n_lat):
    t = jnp.arange(n_lat)
    pos = jnp.stack([t // GRID_W, t % GRID_W], axis=1).astype(F32)
    axis_dims = DA_HEAD_DIM // 2
    inv = ROPE_THETA ** (-jnp.arange(0, axis_dims, 2, dtype=F32) / axis_dims)
    lane = jnp.arange(LANES)
    d = lane % DA_HEAD_DIM
    axis = d // axis_dims
    second = (d % axis_dims) // (axis_dims // 2)
    ang = pos[:, axis] * inv[d % (axis_dims // 2)][None, :]
    cos = jnp.cos(ang)
    sin = jnp.sin(ang)
    sa = jnp.where(second[None, :] == 0, -sin, 0.0)
    sb = jnp.where(second[None, :] == 1, sin, 0.0)
    pad = lambda a, v: jnp.concatenate([jnp.full((n_ctx, LANES), v, F32), a], axis=0)
    return pad(cos, 1.0), pad(sa, 0.0), pad(sb, 0.0)


def _attn_tiles(qt_ref, k_ref, vt_ref, bufs, n_chunks, tk):
    tq = qt_ref.shape[-1]
    heads = range(qt_ref.shape[0])

    def scores(h, off, slot):
        k = k_ref[h, 0, pl.ds(off, tk), :]
        for j in range(2):
            bufs[h][slot][j, 0:tk, :] = jnp.dot(k, qt_ref[h, 0, j], preferred_element_type=F32)

    def softmax_pv(h, off, slot, carry):
        vt = vt_ref[h, 0, :, pl.ds(off, tk)]
        new = []
        for j in range(2):
            m, acc = carry[j]
            s = bufs[h][slot][j, 0:tk, :]
            m_new = jnp.maximum(m, jnp.max(s, axis=0, keepdims=True))
            p = jnp.exp2(s - m_new).astype(BF16)
            acc = jnp.exp2(m - m_new) * acc + jnp.dot(vt, p, preferred_element_type=F32)
            new.append((m_new, acc))
        return tuple(new)

    carries = [tuple((jnp.full((1, tq), -jnp.inf, F32), jnp.zeros((VT_ROWS, tq), F32)) for _ in range(2))
               for _ in heads]
    for h in heads:
        scores(h, 0, 0)
    for c in range(n_chunks):
        for h in heads:
            if c + 1 < n_chunks:
                scores(h, (c + 1) * tk, (c + 1) % 2)
            carries[h] = softmax_pv(h, c * tk, c % 2, carries[h])
    return [tuple(acc[0:DA_V_DIM] / acc[DA_V_DIM:DA_V_DIM + 1] for _, acc in carries[h]) for h in heads]


def _attn_kernel(lam_ref, g_ref, qt_ref, k_ref, vt_ref, o_ref, *s_refs, n_ctx, n_all, tk, lambda_init):
    t = pl.program_id(2)
    bufs = [(s_refs[2 * h], s_refs[2 * h + 1]) for h in range(qt_ref.shape[0])]
    lp = lam_ref[...]
    lam = (jnp.exp(jnp.sum(lp[0:1] * lp[1:2], axis=-1, keepdims=True))
           - jnp.exp(jnp.sum(lp[2:3] * lp[3:4], axis=-1, keepdims=True)) + lambda_init)

    def finish(outs):
        for h, (o0, o1) in enumerate(outs):
            o = o0 - lam * o1
            ms = jnp.mean(o * o, axis=0, keepdims=True)
            o = o * lax.rsqrt(ms + RMS_EPS) * (g_ref[...] * (1.0 - lambda_init))
            o_ref[0, :, h * DA_V_DIM:(h + 1) * DA_V_DIM] = o.T.astype(o_ref.dtype)

    @pl.when(t == 0)
    def _():
        finish(_attn_tiles(qt_ref, k_ref, vt_ref, bufs, 1, n_ctx))

    @pl.when(t > 0)
    def _():
        finish(_attn_tiles(qt_ref, k_ref, vt_ref, bufs, n_all // tk, tk))


def _diff_attention(qt, k, vt, lam_params, subln_g, lambda_init, n_ctx):
    H, B, N, _ = k.shape
    tk = ATTN_TK
    nh = ATTN_HEADS
    assert n_ctx == TM and N % tk == 0 and n_ctx <= tk and H % nh == 0
    kern = functools.partial(_attn_kernel, n_ctx=n_ctx, n_all=N, tk=tk, lambda_init=lambda_init)
    return pl.pallas_call(
        kern,
        grid=(B, H // nh, N // TM),
        in_specs=[pl.BlockSpec(lam_params.shape, lambda b, h, t: (0, 0)),
                  pl.BlockSpec((DA_V_DIM, 1), lambda b, h, t: (0, 0)),
                  pl.BlockSpec((nh, 1, 2, LANES, TM), lambda b, h, t: (h, b, 0, 0, t)),
                  pl.BlockSpec((nh, 1, N, LANES), lambda b, h, t: (h, b, 0, 0)),
                  pl.BlockSpec((nh, 1, VT_ROWS, N), lambda b, h, t: (h, b, 0, 0))],
        out_specs=pl.BlockSpec((1, TM, nh * DA_V_DIM), lambda b, h, t: (b, t, h)),
        out_shape=jax.ShapeDtypeStruct((B, N, H * DA_V_DIM), BF16),
        scratch_shapes=[pltpu.VMEM((2, tk, TM), F32) for _ in range(2 * nh)],
        compiler_params=_cparams("parallel", "parallel", "parallel"),
        name="diff_attention",
    )(lam_params, subln_g.reshape(DA_V_DIM, 1), qt, k, vt)


ROUTE_ID, ROUTE_GATE, ROUTE_RANK = 0, MOE_TOP_K, 2 * MOE_TOP_K
ROUTE_ROWS = SUBLANES
ROUTER_EXPERT_LANE = SUBLANES


def _route_tile(lg, cnt_ref, n_experts):
    G, PER = MOE_GROUPS, MOE_PER_GROUP
    tm = lg.shape[0]
    lt = lg.T
    row = lax.broadcasted_iota(jnp.int32, (SUBLANES, tm), 0)
    ninf = -jnp.inf
    first = lambda hit: jnp.min(jnp.where(hit, row, SUBLANES), axis=0, keepdims=True)
    gl = jnp.where(row < G, lt[0:SUBLANES], ninf)
    gmax = jnp.max(gl, axis=0, keepdims=True)
    g_top = 1.0 / jnp.sum(jnp.exp(gl - gmax), axis=0, keepdims=True)
    g_idx = first(gl == gmax)
    el = lt[ROUTER_EXPERT_LANE:ROUTER_EXPERT_LANE + PER]
    for g in range(1, G):
        el = jnp.where(g_idx == g, lt[ROUTER_EXPERT_LANE + g * PER:ROUTER_EXPERT_LANE + (g + 1) * PER], el)
    e1 = jnp.max(el, axis=0, keepdims=True)
    i1 = first(el == e1)
    el2 = jnp.where(row == i1, ninf, el)
    e2 = jnp.max(el2, axis=0, keepdims=True)
    i2 = first(el2 == e2)
    id1 = g_idx * PER + i1
    id2 = g_idx * PER + i2
    w2 = jnp.exp(e2 - e1)
    den = 1.0 + w2
    erow = lax.broadcasted_iota(jnp.int32, (n_experts, tm), 0)
    hit1 = erow == id1
    hit2 = erow == id2
    onehot = jnp.logical_or(hit1, hit2)
    ui = lax.broadcasted_iota(jnp.int32, (tm, tm), 0)
    ti = lax.broadcasted_iota(jnp.int32, (tm, tm), 1)
    earlier = jnp.dot(onehot.astype(BF16), (ui < ti).astype(BF16), preferred_element_type=F32) + cnt_ref[...]
    r1 = jnp.sum(jnp.where(hit1, earlier, 0.0), axis=0, keepdims=True)
    r2 = jnp.sum(jnp.where(hit2, earlier, 0.0), axis=0, keepdims=True)
    cnt_ref[...] = cnt_ref[...] + jnp.sum(onehot.astype(F32), axis=1, keepdims=True)
    rec = jnp.zeros((ROUTE_ROWS, tm), F32)
    for r, val in ((ROUTE_ID, id1.astype(F32)), (ROUTE_ID + 1, id2.astype(F32)),
                   (ROUTE_GATE, g_top / den), (ROUTE_GATE + 1, g_top * w2 / den),
                   (ROUTE_RANK, r1), (ROUTE_RANK + 1, r2)):
        rec = jnp.where(row == r, val, rec)
    return rec


def _proj_ln_tail(a, c_ref, x_ref, mod_ref, w_ref, lng_ref, lnb_ref, wr_ref, br_ref, x_out, h_out, rt_out, cnt_out, *,
                  d_model, alpha, n_experts):
    D = d_model

    @pl.when(jnp.logical_and(pl.program_id(0) == 0, pl.program_id(1) == 0))
    def _():
        cnt_out[...] = jnp.zeros_like(cnt_out)

    hb = a.shape[0] // 2
    xres = _token_tile(c_ref, x_ref)
    os_ = [jnp.dot(a[r * hb:(r + 1) * hb], w_ref[...], preferred_element_type=F32) for r in range(2)]
    lgs = []
    for r in range(2):
        rows = slice(r * hb, (r + 1) * hb)
        res = alpha * xres[rows, :] + mod_ref[0, :, 2 * D:3 * D] * os_[r]
        xn = _layer_norm(res, lng_ref[...], lnb_ref[...])
        x_out[0, rows, :] = xn
        h2 = xn * (1.0 + mod_ref[0, :, 4 * D:5 * D]) + mod_ref[0, :, 3 * D:4 * D]
        h_out[0, rows, :] = h2.astype(h_out.dtype)
        h_hi = h2.astype(BF16)
        h_lo = (h2 - h_hi.astype(F32)).astype(BF16)
        lgs.append(jnp.dot(h_hi, wr_ref[0], preferred_element_type=F32)
                   + jnp.dot(h_hi, wr_ref[1], preferred_element_type=F32)
                   + jnp.dot(h_lo, wr_ref[0], preferred_element_type=F32) + br_ref[...])
    lg = jnp.concatenate(lgs, axis=0)
    rt_out[0] = _route_tile(lg, cnt_out, n_experts)


def _attn_out_kernel(a_ref, *refs, **kw):
    _proj_ln_tail(a_ref[0], *refs, **kw)


def _ssm_out_kernel(yf_ref, yb_ref, z_ref, ng_ref, *refs, **kw):
    y = (yf_ref[0].astype(F32) + yb_ref[0].astype(F32)).T
    gz = y * _silu(z_ref[0].astype(F32))
    ms = jnp.mean(gz * gz, axis=-1, keepdims=True)
    a = (gz * lax.rsqrt(ms + RMS_EPS) * ng_ref[...]).astype(BF16)
    _proj_ln_tail(a, *refs, **kw)


def _proj_ln(kernel_fn, lead_args, lead_specs, tokens, mod, w, ln_g, ln_b, w_router, b_router, ctx_row, alpha,
             n_experts, name):
    ctx_part, lat_part, lat_off = tokens
    B, _, D = lat_part.shape
    N = lat_part.shape[1] + lat_off * TM
    row = lambda b, t: (b, t, 0)
    const2 = lambda b, t: (0, 0)
    kern = functools.partial(kernel_fn, d_model=D, alpha=alpha, n_experts=n_experts)
    return pl.pallas_call(
        kern,
        grid=(B, N // TM),
        in_specs=lead_specs + _token_specs(lat_off, D) + [
            pl.BlockSpec((1, 1, 6 * D), _mod_row(ctx_row)),
            pl.BlockSpec(w.shape, const2),
            pl.BlockSpec((1, D), const2),
            pl.BlockSpec((1, D), const2),
            pl.BlockSpec(w_router.shape, lambda b, t: (0, 0, 0)),
            pl.BlockSpec((1, LANES), const2)],
        out_specs=[pl.BlockSpec((1, TM, D), row), pl.BlockSpec((1, TM, D), row),
                   pl.BlockSpec((1, ROUTE_ROWS, TM), lambda b, t: (b, 0, t)), pl.BlockSpec((n_experts, 1), const2)],
        out_shape=[jax.ShapeDtypeStruct((B, N, D), F32), jax.ShapeDtypeStruct((B, N, D), BF16),
                   jax.ShapeDtypeStruct((B, ROUTE_ROWS, N), F32), jax.ShapeDtypeStruct((n_experts, 1), F32)],
        compiler_params=_cparams("arbitrary", "arbitrary"),
        name=name,
    )(*lead_args, ctx_part, lat_part, mod, w, ln_g.reshape(1, D), ln_b.reshape(1, D), w_router, b_router)


def _moe_kernel(be_ref, nu_ref, x_ref, wg_ref, wu_ref, wd_ref, o_ref, wgu_bf, wd_bf, *, d_ff):
    i = pl.program_id(0)
    active = i < nu_ref[0]
    new_expert = jnp.logical_or(i == 0, be_ref[i] != be_ref[jnp.maximum(i - 1, 0)])

    @pl.when(jnp.logical_and(active, new_expert))
    def _():
        wgu_bf[:, 0:d_ff] = wg_ref[0, 0].astype(BF16)
        wgu_bf[:, d_ff:] = wu_ref[0, 0].astype(BF16)
        wd_bf[...] = wd_ref[0, 0].astype(BF16)

    @pl.when(active)
    def _():
        hb = x_ref.shape[0] // 2
        gus = [jnp.dot(x_ref[r * hb:(r + 1) * hb, :], wgu_bf[...], preferred_element_type=F32) for r in range(2)]
        for r in range(2):
            mid = (_silu(gus[r][:, :d_ff]) * gus[r][:, d_ff:]).astype(BF16)
            o_ref[r * hb:(r + 1) * hb, :] = jnp.dot(mid, wd_bf[...], preferred_element_type=F32).astype(o_ref.dtype)

    @pl.when(jnp.logical_not(active))
    def _():
        o_ref[...] = jnp.zeros_like(o_ref)


def _moe_experts(buf, blk_expert, n_used, w_gate, w_up, w_down, layer):
    n_rows, D = buf.shape
    d_ff = w_down.shape[2]
    grid_spec = pltpu.PrefetchScalarGridSpec(
        num_scalar_prefetch=2,
        grid=(n_rows // MOE_BM,),
        in_specs=[pl.BlockSpec((MOE_BM, D), lambda i, be, nu: (i, 0)),
                  pl.BlockSpec((1, 1, D, d_ff), lambda i, be, nu: (layer, be[i], 0, 0)),
                  pl.BlockSpec((1, 1, D, d_ff), lambda i, be, nu: (layer, be[i], 0, 0)),
                  pl.BlockSpec((1, 1, d_ff, D), lambda i, be, nu: (layer, be[i], 0, 0))],
        out_specs=pl.BlockSpec((MOE_BM, D), lambda i, be, nu: (i, 0)),
        scratch_shapes=[pltpu.VMEM((D, 2 * d_ff), BF16), pltpu.VMEM((d_ff, D), BF16)])
    return pl.pallas_call(
        functools.partial(_moe_kernel, d_ff=d_ff),
        grid_spec=grid_spec,
        out_shape=jax.ShapeDtypeStruct((n_rows, D), BF16),
        compiler_params=_cparams("arbitrary"),
        name="moe_experts",
    )(blk_expert, n_used, buf, w_gate, w_up, w_down)


def _dispatch(eid, rank, sizes):
    K, T = eid.shape
    A = T * K
    E = sizes.shape[0]
    padded = (sizes + MOE_BM - 1) // MOE_BM * MOE_BM
    pad_end = jnp.cumsum(padded)
    pad_start = pad_end - padded
    starts = jnp.cumsum(sizes) - sizes
    hit = eid[:, :, None] == jnp.arange(E, dtype=jnp.int32)
    dest = jnp.sum(jnp.where(hit, pad_start, 0), axis=-1) + rank
    n_blocks = -(-A // MOE_BM) + E
    blk_start = jnp.arange(n_blocks, dtype=jnp.int32) * MOE_BM
    blk_expert = jnp.minimum(jnp.sum(blk_start[:, None] >= pad_end[None, :], axis=1), E - 1).astype(jnp.int32)
    n_used = (pad_end[-1] // MOE_BM).astype(jnp.int32).reshape(1)
    order = jnp.argsort(eid.T.reshape(A))
    within = jnp.arange(MOE_BM, dtype=jnp.int32)[None, :] + (blk_start - pad_start[blk_expert])[:, None]
    src = jnp.clip(starts[blk_expert][:, None] + within, 0, A - 1)
    filler = (blk_start[:, None] + jnp.arange(MOE_BM, dtype=jnp.int32)[None, :]) % T
    slot_tok = jnp.where(within < sizes[blk_expert][:, None], jnp.take(order, src, mode='clip') // K, filler)
    return slot_tok.reshape(-1).astype(jnp.int32), blk_expert, n_used, dest.astype(jnp.int32)


def _combine_kernel(x_ref, y0_ref, y1_ref, rt_ref, mod_ref, lng_ref, lnb_ref, o_ref, *, d_model, alpha):
    D = d_model
    rt = rt_ref[0].T
    f = (rt[:, ROUTE_GATE:ROUTE_GATE + 1] * y0_ref[0, 0].astype(F32)
         + rt[:, ROUTE_GATE + 1:ROUTE_GATE + 2] * y1_ref[0, 0].astype(F32))
    r = alpha * x_ref[0] + mod_ref[0, :, 5 * D:6 * D] * f
    o_ref[0] = _layer_norm(r, lng_ref[...], lnb_ref[...])


def _combine_ln(x_all, ys, route, mod, ln_g, ln_b, ctx_row, alpha, t0):
    B, N, D = x_all.shape
    nt = N // TM - t0
    tiles_per_batch = N // TM
    row = lambda b, t: (b, t + t0, 0)
    const2 = lambda b, t: (0, 0)
    mrow = _mod_row(ctx_row)
    return pl.pallas_call(
        functools.partial(_combine_kernel, d_model=D, alpha=alpha),
        grid=(B, nt),
        in_specs=[pl.BlockSpec((1, TM, D), row),
                  pl.BlockSpec((1, 1, TM, D), lambda b, t: (0, b * tiles_per_batch + t + t0, 0, 0)),
                  pl.BlockSpec((1, 1, TM, D), lambda b, t: (1, b * tiles_per_batch + t + t0, 0, 0)),
                  pl.BlockSpec((1, ROUTE_ROWS, TM), lambda b, t: (b, 0, t + t0)),
                  pl.BlockSpec((1, 1, 6 * D), lambda b, t: mrow(b, t + t0)),
                  pl.BlockSpec((1, D), const2),
                  pl.BlockSpec((1, D), const2)],
        out_specs=pl.BlockSpec((1, TM, D), lambda b, t: (b, t, 0)),
        out_shape=jax.ShapeDtypeStruct((B, nt * TM, D), F32),
        compiler_params=_cparams("parallel", "parallel"),
        name="moe_combine_ln",
    )(x_all, ys, ys, route, mod, ln_g.reshape(1, D), ln_b.reshape(1, D))


def _moe_layer(x1, h2, route, counts, mod, w_gate, w_up, w_down, layer, ln_g, ln_b, ctx_row, alpha, t0, defer=False):
    B, N, D = x1.shape
    T = B * N
    by_row = lambda r: jnp.swapaxes(route[:, r:r + MOE_TOP_K, :], 0, 1).reshape(MOE_TOP_K, T).astype(jnp.int32)
    eid = by_row(ROUTE_ID)
    rank = by_row(ROUTE_RANK)
    sizes = counts[:, 0].astype(jnp.int32)
    slot_tok, blk_expert, n_used, dest = _dispatch(eid, rank, sizes)
    buf = jnp.take(h2.reshape(T, D), slot_tok, axis=0, mode='clip')
    yb = _moe_experts(buf, blk_expert, n_used, w_gate, w_up, w_down, layer)
    ys = jnp.take(yb, dest, axis=0, mode='clip').reshape(MOE_TOP_K, T // TM, TM, D)
    if defer:
        return (x1, ys, route, mod, ln_g, ln_b)
    return _combine_ln(x1, ys, route, mod, ln_g, ln_b, ctx_row, alpha, t0)


def _softplus(v):
    return jnp.maximum(v, 0.0) + jnp.log1p(jnp.exp(-jnp.abs(v)))


def _in_proj_rows(h, rows, w_ref, dtb_ref, z_out, xbc_out, dt_out, dtt_out, d_inner, d_conv):
    z_out[0, rows, :] = jnp.dot(h, w_ref[:, 0:d_inner], preferred_element_type=F32).astype(z_out.dtype)
    cw = 512
    for j in range(d_conv // cw):
        xbc_out[0, rows, j * cw:(j + 1) * cw] = jnp.dot(
            h, w_ref[:, d_inner + j * cw:d_inner + (j + 1) * cw], preferred_element_type=F32).astype(xbc_out.dtype)
    dt = _softplus(jnp.dot(h, w_ref[:, d_inner + d_conv:], preferred_element_type=F32) + dtb_ref[...])
    dt_out[0, rows, :] = dt
    dtt_out[0, :, rows] = dt.T


def _ssm_in_kernel(x_ref, mod_ref, w_ref, dtb_ref, z_out, xbc_out, dt_out, dtt_out, *, d_model, d_inner, d_conv):
    D = d_model
    h = (x_ref[0] * (1.0 + mod_ref[0, :, D:2 * D]) + mod_ref[0, :, 0:D]).astype(BF16)
    _in_proj_rows(h, slice(0, TM), w_ref, dtb_ref, z_out, xbc_out, dt_out, dtt_out, d_inner, d_conv)


def _ssm_in_combine_kernel(x1_ref, y0_ref, y1_ref, rt_ref, modp_ref, lng_ref, lnb_ref, mod_ref, w_ref, dtb_ref,
                           x_out, z_out, xbc_out, dt_out, dtt_out, *, d_model, d_inner, d_conv, alpha):
    D = d_model
    rt = rt_ref[0].T
    hb = TM // 2
    hs = []
    for r in range(2):
        rows = slice(r * hb, (r + 1) * hb)
        f = (rt[rows, ROUTE_GATE:ROUTE_GATE + 1] * y0_ref[0, 0, rows, :].astype(F32)
             + rt[rows, ROUTE_GATE + 1:ROUTE_GATE + 2] * y1_ref[0, 0, rows, :].astype(F32))
        res = alpha * x1_ref[0, rows, :] + modp_ref[0, :, 5 * D:6 * D] * f
        xn = _layer_norm(res, lng_ref[...], lnb_ref[...])
        x_out[0, rows, :] = xn
        hs.append((xn * (1.0 + mod_ref[0, :, D:2 * D]) + mod_ref[0, :, 0:D]).astype(BF16))
    for r in range(2):
        _in_proj_rows(hs[r], slice(r * hb, (r + 1) * hb), w_ref, dtb_ref, z_out, xbc_out, dt_out, dtt_out,
                      d_inner, d_conv)


def _ssm_in_proj(x_all, mod, w_in_p, dt_bias_p, ctx_row, d_inner, d_conv, pending=None, alpha=None):
    B, N, D = (pending[0] if pending is not None else x_all).shape
    row = lambda b, t: (b, t, 0)
    const2 = lambda b, t: (0, 0)
    tiles = N // TM
    proj_specs = [pl.BlockSpec((1, 1, 6 * D), _mod_row(ctx_row)),
                  pl.BlockSpec(w_in_p.shape, const2),
                  pl.BlockSpec((1, LANES), const2)]
    out_specs = [pl.BlockSpec((1, TM, d_inner), row), pl.BlockSpec((1, TM, d_conv), row),
                 pl.BlockSpec((1, TM, LANES), row), pl.BlockSpec((1, LANES, TM), lambda b, t: (b, 0, t))]
    out_shape = [jax.ShapeDtypeStruct((B, N, d_inner), BF16), jax.ShapeDtypeStruct((B, N, d_conv), BF16),
                 jax.ShapeDtypeStruct((B, N, LANES), F32), jax.ShapeDtypeStruct((B, LANES, N), F32)]
    if pending is None:
        kern = functools.partial(_ssm_in_kernel, d_model=D, d_inner=d_inner, d_conv=d_conv)
        in_specs = [pl.BlockSpec((1, TM, D), row)] + proj_specs
        args = (x_all, mod, w_in_p, dt_bias_p)
    else:
        x1, ys, route, mod_prev, ln_g, ln_b = pending
        kern = functools.partial(_ssm_in_combine_kernel, d_model=D, d_inner=d_inner, d_conv=d_conv, alpha=alpha)
        in_specs = [pl.BlockSpec((1, TM, D), row),
                    pl.BlockSpec((1, 1, TM, D), lambda b, t: (0, b * tiles + t, 0, 0)),
                    pl.BlockSpec((1, 1, TM, D), lambda b, t: (1, b * tiles + t, 0, 0)),
                    pl.BlockSpec((1, ROUTE_ROWS, TM), lambda b, t: (b, 0, t)),
                    pl.BlockSpec((1, 1, 6 * D), _mod_row(ctx_row)),
                    pl.BlockSpec((1, D), const2),
                    pl.BlockSpec((1, D), const2)] + proj_specs
        args = (x1, ys, ys, route, mod_prev, ln_g.reshape(1, D), ln_b.reshape(1, D), mod, w_in_p, dt_bias_p)
        out_specs = [pl.BlockSpec((1, TM, D), row)] + out_specs
        out_shape = [jax.ShapeDtypeStruct((B, N, D), F32)] + out_shape
    return pl.pallas_call(
        kern,
        grid=(B, tiles),
        in_specs=in_specs,
        out_specs=out_specs,
        out_shape=out_shape,
        compiler_params=_cparams("parallel", "parallel"),
        name="ssm_in_proj",
    )(*args)


CONV_HALO = 16


def _conv_kernel(x_ref, prev_ref, next_ref, w_ref, b_ref, xt_ref, bc_ref, ext_ref, *, n_tiles, d_inner):
    t = pl.program_id(1)
    halo = CONV_HALO
    pad = SSM_CONV // 2
    has_prev = t >= 2
    has_next = jnp.logical_and(t >= 1, t < n_tiles - 1)
    ext_ref[0:halo, :] = jnp.where(has_prev, prev_ref[0].astype(F32), 0.0)
    ext_ref[halo:halo + TM, :] = x_ref[0].astype(F32)
    ext_ref[halo + TM:, :] = jnp.where(has_next, next_ref[0].astype(F32), 0.0)
    rows = TM + 2 * halo
    cw = LANES
    for j in range(ext_ref.shape[1] // cw):
        cols = slice(j * cw, (j + 1) * cw)
        ext = ext_ref[:, cols]
        acc = b_ref[:, cols] + w_ref[pad:pad + 1, cols] * ext[halo:halo + TM]
        for k in range(SSM_CONV):
            if k != pad:
                acc = acc + w_ref[k:k + 1, cols] * pltpu.roll(ext, (pad - k) % rows, 0)[halo:halo + TM]
        act = _silu(acc)
        if j * cw < d_inner:
            xt_ref[0, j * cw:(j + 1) * cw, :] = act.T.astype(xt_ref.dtype)
        else:
            bc_ref[0, :, j * cw - d_inner:(j + 1) * cw - d_inner] = act.astype(bc_ref.dtype)


def _ssm_conv(xbc, conv_w_p, conv_b, d_inner):
    B, N, C = xbc.shape
    n_tiles = N // TM
    hb = TM // CONV_HALO
    last_hb = N // CONV_HALO - 1
    return pl.pallas_call(
        functools.partial(_conv_kernel, n_tiles=n_tiles, d_inner=d_inner),
        grid=(B, n_tiles),
        in_specs=[pl.BlockSpec((1, TM, C), lambda b, t: (b, t, 0)),
                  pl.BlockSpec((1, CONV_HALO, C), lambda b, t: (b, jnp.maximum(t * hb - 1, 0), 0)),
                  pl.BlockSpec((1, CONV_HALO, C), lambda b, t: (b, jnp.minimum((t + 1) * hb, last_hb), 0)),
                  pl.BlockSpec(conv_w_p.shape, lambda b, t: (0, 0)),
                  pl.BlockSpec((1, C), lambda b, t: (0, 0))],
        out_specs=[pl.BlockSpec((1, d_inner, TM), lambda b, t: (b, 0, t)),
                   pl.BlockSpec((1, TM, C - d_inner), lambda b, t: (b, t, 0))],
        out_shape=[jax.ShapeDtypeStruct((B, d_inner, N), BF16), jax.ShapeDtypeStruct((B, N, C - d_inner), BF16)],
        scratch_shapes=[pltpu.VMEM((TM + 2 * CONV_HALO, C), F32)],
        compiler_params=_cparams("parallel", "parallel"),
        name="ssm_conv",
    )(xbc, xbc, xbc, conv_w_p, conv_b.reshape(1, C))


def _ssd_chunk(sub, xt_ref, bc_ref, dt_ref, dtt_ref, alr_ref, alc_ref, dsk_ref, y_ref, state_ref, xw_ref, *,
               direction, n_heads):
    L = SSM_CHUNK
    P = SSM_HEAD_DIM
    NS = SSM_STATE
    hpg = n_heads // SSM_GROUPS
    gw = hpg * P
    reverse = direction == 1
    l0 = direction * n_heads

    tsl = slice(sub * L, (sub + 1) * L)
    ri = lax.broadcasted_iota(jnp.int32, (L, L), 0)
    ci = lax.broadcasted_iota(jnp.int32, (L, L), 1)
    before = (ri >= ci) if reverse else (ri <= ci)
    after = (ci >= ri) if reverse else (ci <= ri)
    a = jnp.dot(after.astype(F32), dt_ref[0, tsl, :] * -jnp.exp(alr_ref[...]), precision=HIGHEST,
                preferred_element_type=F32)
    at = jnp.dot(dtt_ref[0, :, tsl] * -jnp.exp(alc_ref[...]), before.astype(F32), precision=HIGHEST,
                 preferred_element_type=F32)[l0:l0 + n_heads]
    last = 0 if reverse else L - 1
    tot = jnp.broadcast_to(at[:, last:last + 1], (n_heads, L))
    ea = jnp.exp(at)
    wgt = jnp.exp(tot - at)
    etot = jnp.exp(tot)
    dtt = dtt_ref[0, l0:l0 + n_heads, tsl]
    dsk = dsk_ref[direction]

    for g in range(SSM_GROUPS):
        bg = bc_ref[0, tsl, g * NS:(g + 1) * NS]
        cg = bc_ref[0, tsl, (SSM_GROUPS + g) * NS:(SSM_GROUPS + g + 1) * NS]
        cbt = lax.dot_general(bg, cg, (((1,), (1,)), ((), ())), preferred_element_type=F32)
        yoff = lax.dot_general(state_ref[g * gw:(g + 1) * gw, :].astype(BF16), cg, (((1,), (1,)), ((), ())),
                               preferred_element_type=F32)
        for hh in range(hpg):
            h = g * hpg + hh
            rows = slice(h * P, (h + 1) * P)
            decay = jnp.where(before, jnp.exp(at[h:h + 1, :] - a[:, l0 + h:l0 + h + 1]), 0.0)
            mt = (cbt * decay).astype(BF16)
            xh = xt_ref[0, rows, tsl].astype(F32)
            xdt = xh * dtt[h:h + 1, :]
            y = (jnp.dot(xdt.astype(BF16), mt, preferred_element_type=F32)
                 + yoff[hh * P:(hh + 1) * P, :] * ea[h:h + 1, :] + xh * dsk[h:h + 1, :])
            y_ref[0, rows, tsl] = y.astype(y_ref.dtype)
            xw_ref[hh * P:(hh + 1) * P, :] = (xdt * wgt[h:h + 1, :]).astype(xw_ref.dtype)
        upd = jnp.dot(xw_ref[...], bg, preferred_element_type=F32)
        for hh in range(hpg):
            h = g * hpg + hh
            rows = slice(h * P, (h + 1) * P)
            state_ref[rows, :] = state_ref[rows, :] * etot[h:h + 1, :] + upd[hh * P:(hh + 1) * P, :]


def _ssd_kernel(*refs, direction, n_heads):
    state_ref = refs[-2]

    @pl.when(pl.program_id(1) == 0)
    def _():
        state_ref[...] = jnp.zeros_like(state_ref)

    for sub in ((1, 0) if direction == 1 else (0, 1)):
        _ssd_chunk(sub, *refs, direction=direction, n_heads=n_heads)


def _ssd_scan(xt, bc, dt, dtt, a_log, d_skip, direction, n_ctx):
    B, d_inner, N = xt.shape
    n_heads = a_log.shape[1]
    L = SSM_CHUNK
    LB = 2 * L
    assert n_ctx % LB == 0 and N % LB == 0
    nc = N // LB
    ncc = n_ctx // LB
    if direction == 0:
        chunk = lambda c: c
    else:
        chunk = lambda c: jnp.where(c < ncc, ncc - 1 - c, nc - 1 - (c - ncc))
    kern = functools.partial(_ssd_kernel, direction=direction, n_heads=n_heads)
    a_log_lanes = jnp.pad(a_log.reshape(-1), (0, LANES - a_log.size))
    d_skip_lanes = jnp.broadcast_to(d_skip.astype(F32)[:, :, None], d_skip.shape + (L,))
    return pl.pallas_call(
        kern,
        grid=(B, nc),
        in_specs=[pl.BlockSpec((1, d_inner, LB), lambda b, c: (b, 0, chunk(c))),
                  pl.BlockSpec((1, LB, bc.shape[2]), lambda b, c: (b, chunk(c), 0)),
                  pl.BlockSpec((1, LB, LANES), lambda b, c: (b, chunk(c), 0)),
                  pl.BlockSpec((1, LANES, LB), lambda b, c: (b, 0, chunk(c))),
                  pl.BlockSpec((1, LANES), lambda b, c: (0, 0)),
                  pl.BlockSpec((LANES, 1), lambda b, c: (0, 0)),
                  pl.BlockSpec(d_skip_lanes.shape, lambda b, c: (0, 0, 0))],
        out_specs=pl.BlockSpec((1, d_inner, LB), lambda b, c: (b, 0, chunk(c))),
        out_shape=jax.ShapeDtypeStruct((B, d_inner, N), BF16),
        scratch_shapes=[pltpu.VMEM((d_inner, SSM_STATE), F32),
                        pltpu.VMEM((d_inner // SSM_GROUPS, L), BF16)],
        compiler_params=_cparams("parallel", "arbitrary"),
        name=f"ssd_scan_dir{direction}",
    )(xt, bc, dt, dtt, a_log_lanes.reshape(1, LANES), a_log_lanes.reshape(LANES, 1), d_skip_lanes)


def kernel(x, c, ctx, c_ctx, w_mod, b_mod, ln1_g, ln1_b, ln2_g, ln2_b, attn_w_qkv, attn_w_o, attn_lq1, attn_lk1, attn_lq2, attn_lk2, attn_subln_g, ssm_w_in, ssm_conv_w, ssm_conv_b, ssm_dt_bias, ssm_a_log, ssm_d, ssm_norm_g, ssm_w_out, moe_w_group, moe_b_group, moe_w_expert, moe_b_expert, moe_w_gate, moe_w_up, moe_w_down):
    B, S, D = x.shape
    C = ctx.shape[1]
    depth = w_mod.shape[0]
    E = moe_w_expert.shape[-1]
    assert C == TM and S % TM == 0 and B < SUBLANES
    alpha = (2 * depth) ** 0.25
    ctx_row = B

    cond = jnp.zeros((SUBLANES, D), F32).at[:B].set(c).at[B].set(c_ctx)
    mods = _modulation(cond, w_mod, b_mod)
    tokens = (ctx, x, 1)
    x_all = None
    cos, sa, sb = _rope_tables(C, S)

    pending = None
    for i in range(depth):
        last = i == depth - 1
        j = i // N_MIXERS
        mod = mods[i].reshape(SUBLANES, 1, 6 * D)
        lane_pad = lambda a, n: jnp.pad(a, [(0, 0)] * (a.ndim - 1) + [(0, n - a.shape[-1])])
        w_router = jnp.concatenate([lane_pad(moe_w_group[i], ROUTER_EXPERT_LANE),
                                    lane_pad(moe_w_expert[i], LANES - ROUTER_EXPERT_LANE)], axis=-1)
        w_router_hi = w_router.astype(BF16)
        w_router = jnp.stack([w_router_hi, (w_router - w_router_hi.astype(F32)).astype(BF16)])
        b_router = jnp.concatenate([lane_pad(moe_b_group[i], ROUTER_EXPERT_LANE),
                                    lane_pad(moe_b_expert[i], LANES - ROUTER_EXPERT_LANE)]).reshape(1, LANES)
        if i % N_MIXERS == 0:
            lambda_init = 0.8 - 0.6 * math.exp(-0.3 * i)
            qt, k, vt = _qkv_proj(tokens, mod, attn_w_qkv[j].astype(BF16), cos, sa, sb, ctx_row)
            lam_params = jnp.stack([attn_lq1[j], attn_lk1[j], attn_lq2[j], attn_lk2[j]])
            o = _diff_attention(qt, k, vt, lam_params, attn_subln_g[j], lambda_init, C)
            lead_args = [o]
            lead_specs = [pl.BlockSpec((1, TM, o.shape[-1]), lambda b, t: (b, t, 0))]
            x1, h2, route, counts = _proj_ln(_attn_out_kernel, lead_args, lead_specs, tokens, mod,
                                             attn_w_o[j].astype(BF16), ln1_g[i], ln1_b[i], w_router, b_router,
                                             ctx_row, alpha, E, "attn_out_ln")
        else:
            n_heads = ssm_a_log.shape[-1]
            d_inner = n_heads * SSM_HEAD_DIM
            d_conv = d_inner + 2 * SSM_GROUPS * SSM_STATE
            w_in = ssm_w_in[j]
            n_dt = w_in.shape[1] - d_inner - d_conv
            w_in_p = jnp.pad(w_in, ((0, 0), (0, LANES - n_dt))).astype(BF16)
            dt_bias_p = jnp.pad(ssm_dt_bias[j].reshape(1, n_dt), ((0, 0), (0, LANES - n_dt)))
            if pending is None:
                if x_all is None:
                    x_all = jnp.concatenate([ctx, x], axis=1)
                z, xbc, dt, dtt = _ssm_in_proj(x_all, mod, w_in_p, dt_bias_p, ctx_row, d_inner, d_conv)
            else:
                x_all, z, xbc, dt, dtt = _ssm_in_proj(None, mod, w_in_p, dt_bias_p, ctx_row, d_inner, d_conv,
                                                      pending=pending, alpha=alpha)
                pending = None
            conv_w_p = jnp.pad(ssm_conv_w[j], ((0, SUBLANES - SSM_CONV), (0, 0)))
            xt, bc = _ssm_conv(xbc, conv_w_p, ssm_conv_b[j], d_inner)
            y_f = _ssd_scan(xt, bc, dt, dtt, ssm_a_log[j], ssm_d[j], 0, C)
            y_b = _ssd_scan(xt, bc, dt, dtt, ssm_a_log[j], ssm_d[j], 1, C)
            row = lambda b, t: (b, t, 0)
            col = lambda b, t: (b, 0, t)
            lead_args = [y_f, y_b, z, ssm_norm_g[j].reshape(1, d_inner)]
            lead_specs = [pl.BlockSpec((1, d_inner, TM), col), pl.BlockSpec((1, d_inner, TM), col),
                          pl.BlockSpec((1, TM, d_inner), row), pl.BlockSpec((1, d_inner), lambda b, t: (0, 0))]
            x1, h2, route, counts = _proj_ln(_ssm_out_kernel, lead_args, lead_specs, (x_all, x_all, 0), mod,
                                             ssm_w_out[j].astype(BF16), ln1_g[i], ln1_b[i], w_router, b_router,
                                             ctx_row, alpha, E, "ssm_out_ln")
        defer = (not last) and (i + 1) % N_MIXERS == 1
        res = _moe_layer(x1, h2, route, counts, mod, moe_w_gate, moe_w_up, moe_w_down, i,
                         ln2_g[i], ln2_b[i], ctx_row, alpha, 1 if last else 0, defer=defer)
        if defer:
            pending = res
        else:
            x_all = res
            tokens = (x_all, x_all, 0)
    return x_all
```
